```python
import math
import jax, jax.numpy as jnp
from jax import lax
import numpy as np

D_MODEL = 1024
BATCH = 16
SEQ = 2048
DEPTH = 4
DEC_BATCH = 8
DEC_SEQ = 4096
PAST_LEN = 128

GRID_W = 64
N_MIXERS = 2
N_HYENA = (DEPTH + 1) // 2
N_ATTN = DEPTH // 2
HY_WIDTH = D_MODEL
HY_ORDER = 2
SHORT_CONV = 3
POS_EMB_DIM = 33
POS_BANDS = (POS_EMB_DIM - 1) // 2
FILTER_WIDTH = 64
DECAY_TARGET = 1e-2
FAST_DECAY_PCT = 0.3
SLOW_DECAY_PCT = 1.5
FILTER_EPS = 1e-6
HEAD_DIM = 128
N_Q_HEADS = D_MODEL // HEAD_DIM
N_KV_HEADS = 2
GROUP = N_Q_HEADS // N_KV_HEADS
ATTN_WIDTH = N_Q_HEADS * HEAD_DIM
KV_WIDTH = N_KV_HEADS * HEAD_DIM
AXIS_DIM = HEAD_DIM // 2
ROPE_THETA = 10000.0
Q_BLOCK = 128
RMS_EPS = 1e-6
LN_EPS = 1e-5
DEEPNORM_ALPHA = (2.0 * DEPTH) ** 0.25
DEEPNORM_BETA = (8.0 * DEPTH) ** -0.25

kernel_name = "hyena_axial_gqa_deepnorm_encoder"

F32 = jnp.float32


def layer_norm(x, g, b):
    xf = x.astype(F32)
    mu = jnp.mean(xf, axis=-1, keepdims=True)
    xc = xf - mu
    var = jnp.mean(xc * xc, axis=-1, keepdims=True)
    return (xc * lax.rsqrt(var + LN_EPS) * g.astype(F32) + b.astype(F32)).astype(x.dtype)


def short_conv(u, w, b):
    L = u.shape[1]
    up = jnp.pad(u, ((0, 0), (1, 1), (0, 0)))
    return up[:, :L] * w[0] + up[:, 1:L + 1] * w[1] + up[:, 2:] * w[2] + b


def hyena_filters(L, w1, b1, freq, w2, b2, w3, decay):
    t = jnp.linspace(0.0, 1.0, L, dtype=F32)[:, None]
    w = 2.0 * math.pi * jnp.arange(L, dtype=F32)[:, None] / L
    f = jnp.linspace(1e-4, POS_BANDS - 1, POS_BANDS, dtype=F32)[None]
    feats = jnp.concatenate([t, jnp.cos(f * w), -jnp.sin(f * w)], axis=-1)
    fr = freq.astype(F32)
    h = jnp.sin(fr * (feats @ w1.astype(F32) + b1.astype(F32)))
    h = jnp.sin(fr * (h @ w2.astype(F32) + b2.astype(F32)))
    h = h @ w3.astype(F32)
    h = h * jnp.exp(-t * jnp.abs(decay.astype(F32)))
    h = h.reshape(L, HY_ORDER, 2, HY_WIDTH)
    h = h * lax.rsqrt(jnp.sum(h * h, axis=(0, 2), keepdims=True) + FILTER_EPS)
    fwd = h[:, :, 0]
    bwd = h[:, :, 1]
    k = jnp.concatenate([fwd, jnp.zeros((1, HY_ORDER, HY_WIDTH), F32), bwd[:0:-1]], axis=0)
    return jnp.fft.rfft(k, axis=0)


def long_conv(z, kf):
    L = z.shape[1]
    zf = jnp.fft.rfft(z, n=2 * L, axis=1)
    return jnp.fft.irfft(zf * kf[None], n=2 * L, axis=1)[:, :L]


def hyena_mixer(x, w_in, b_in, conv_w, conv_b, f_w1, f_b1, f_freq, f_w2, f_b2, f_w3, decay, filt_bias, w_out, b_out):
    L = x.shape[1]
    u = x @ w_in + b_in
    proj = short_conv(u[..., :3 * HY_WIDTH], conv_w, conv_b)
    gate = u[..., 3 * HY_WIDTH:]
    v = proj[..., :HY_WIDTH]
    gates = (proj[..., HY_WIDTH:2 * HY_WIDTH], proj[..., 2 * HY_WIDTH:])
    kf = hyena_filters(L, f_w1, f_b1, f_freq, f_w2, f_b2, f_w3, decay)
    z = v.astype(F32)
    for o in range(HY_ORDER):
        z = gates[o].astype(F32) * (long_conv(z, kf[:, o]) + z * filt_bias[o].astype(F32))
    y = z.astype(x.dtype) * jax.nn.silu(gate)
    return y @ w_out + b_out


def axial_rope(L):
    rows = L // GRID_W
    row = jnp.repeat(jnp.arange(rows, dtype=F32), GRID_W)
    col = jnp.tile(jnp.arange(GRID_W, dtype=F32), rows)
    inv = ROPE_THETA ** (-jnp.arange(0, AXIS_DIM, 2, dtype=F32) / AXIS_DIM)
    ang = jnp.concatenate([row[:, None] * inv, col[:, None] * inv], axis=-1)
    return jnp.cos(ang), jnp.sin(ang)


def apply_rope(x, cos, sin):
    c = cos[None, :, None]
    s = sin[None, :, None]
    x1 = x[..., :AXIS_DIM]
    x2 = x[..., AXIS_DIM:]
    return jnp.concatenate([x1 * c - x2 * s, x2 * c + x1 * s], axis=-1)


def rms_heads(x, g):
    xf = x.astype(F32)
    return xf * lax.rsqrt(jnp.mean(xf * xf, axis=-1, keepdims=True) + RMS_EPS) * g.astype(F32)


def attention_mixer(x, w_in, b_in, q_norm, k_norm, w_out, b_out):
    B, L, _ = x.shape
    u = x @ w_in + b_in
    q = u[..., :ATTN_WIDTH].reshape(B, L, N_Q_HEADS, HEAD_DIM)
    k = u[..., ATTN_WIDTH:ATTN_WIDTH + KV_WIDTH].reshape(B, L, N_KV_HEADS, HEAD_DIM)
    v = u[..., ATTN_WIDTH + KV_WIDTH:ATTN_WIDTH + 2 * KV_WIDTH].reshape(B, L, N_KV_HEADS, HEAD_DIM)
    gate = u[..., ATTN_WIDTH + 2 * KV_WIDTH:]
    cos, sin = axial_rope(L)
    q = apply_rope(rms_heads(q, q_norm), cos, sin).astype(x.dtype)
    k = apply_rope(rms_heads(k, k_norm), cos, sin).astype(x.dtype)
    nb = L // Q_BLOCK
    qb = q.reshape(B, nb, Q_BLOCK, N_KV_HEADS, GROUP, HEAD_DIM).transpose(1, 0, 3, 4, 2, 5)
    kt = k.transpose(0, 2, 1, 3)
    vt = v.transpose(0, 2, 1, 3)
    scale = HEAD_DIM ** -0.5

    def block(qi):
        s = jnp.einsum("bkgqd,bksd->bkgqs", qi, kt).astype(F32) * scale
        p = jax.nn.softmax(s, axis=-1).astype(vt.dtype)
        return jnp.einsum("bkgqs,bksd->bkgqd", p, vt)

    o = lax.map(block, qb)
    o = o.transpose(1, 0, 4, 2, 3, 5).reshape(B, L, ATTN_WIDTH)
    y = o * jax.nn.silu(gate)
    return y @ w_out + b_out


def trunk(x, hy_w_in, hy_b_in, hy_conv_w, hy_conv_b, hy_f_w1, hy_f_b1, hy_f_freq, hy_f_w2, hy_f_b2, hy_f_w3,
          hy_decay, hy_filt_bias, hy_w_out, hy_b_out, at_w_in, at_b_in, at_q_norm, at_k_norm, at_w_out, at_b_out,
          ln_g, ln_b):
    for i in range(DEPTH):
        j = i // N_MIXERS
        if i % N_MIXERS == 0:
            s = hyena_mixer(x, hy_w_in[j], hy_b_in[j], hy_conv_w[j], hy_conv_b[j], hy_f_w1[j], hy_f_b1[j],
                            hy_f_freq[j], hy_f_w2[j], hy_f_b2[j], hy_f_w3[j], hy_decay[j], hy_filt_bias[j],
                            hy_w_out[j], hy_b_out[j])
        else:
            s = attention_mixer(x, at_w_in[j], at_b_in[j], at_q_norm[j], at_k_norm[j], at_w_out[j], at_b_out[j])
        x = layer_norm(DEEPNORM_ALPHA * x + s, ln_g[i], ln_b[i])
    return x


def setup_inputs(seed: int = 0) -> dict:
    key = jax.random.key(seed)
    ks = jax.random.split(key, 24)
    E = HY_WIDTH
    nrm = lambda k, shape, sc: jax.random.normal(k, shape, F32) * sc
    base_decay = jnp.linspace(abs(math.log(DECAY_TARGET)) / SLOW_DECAY_PCT,
                              abs(math.log(DECAY_TARGET)) / FAST_DECAY_PCT, E, dtype=F32)
    attn_in = ATTN_WIDTH + 2 * KV_WIDTH + ATTN_WIDTH
    return {
        "x_prompt": nrm(ks[0], (BATCH, SEQ, D_MODEL), 1.0),
        "x_sample": nrm(ks[1], (DEC_BATCH, DEC_SEQ, D_MODEL), 1.0),
        "hy_w_in": nrm(ks[2], (N_HYENA, D_MODEL, 4 * E), D_MODEL ** -0.5),
        "hy_b_in": nrm(ks[3], (N_HYENA, 4 * E), 0.01),
        "hy_conv_w": nrm(ks[4], (N_HYENA, SHORT_CONV, 3 * E), SHORT_CONV ** -0.5),
        "hy_conv_b": nrm(ks[5], (N_HYENA, 3 * E), 0.01),
        "hy_f_w1": nrm(ks[6], (N_HYENA, POS_EMB_DIM, FILTER_WIDTH), POS_EMB_DIM ** -0.5),
        "hy_f_b1": nrm(ks[7], (N_HYENA, FILTER_WIDTH), 0.01),
        "hy_f_freq": 1.0 + nrm(ks[8], (N_HYENA, FILTER_WIDTH), 0.01),
        "hy_f_w2": nrm(ks[9], (N_HYENA, FILTER_WIDTH, FILTER_WIDTH), FILTER_WIDTH ** -0.5),
        "hy_f_b2": nrm(ks[10], (N_HYENA, FILTER_WIDTH), 0.01),
        "hy_f_w3": nrm(ks[11], (N_HYENA, FILTER_WIDTH, 2 * HY_ORDER * E), FILTER_WIDTH ** -0.5),
        "hy_decay": jnp.tile(base_decay, 2 * HY_ORDER)[None] + nrm(ks[12], (N_HYENA, 2 * HY_ORDER * E), 0.1),
        "hy_filt_bias": nrm(ks[13], (N_HYENA, HY_ORDER, E), 1.0),
        "hy_w_out": nrm(ks[14], (N_HYENA, E, D_MODEL), DEEPNORM_BETA * E ** -0.5),
        "hy_b_out": nrm(ks[15], (N_HYENA, D_MODEL), 0.01),
        "at_w_in": nrm(ks[16], (N_ATTN, D_MODEL, attn_in), D_MODEL ** -0.5),
        "at_b_in": nrm(ks[17], (N_ATTN, attn_in), 0.01),
        "at_q_norm": 1.0 + nrm(ks[18], (N_ATTN, HEAD_DIM), 0.01),
        "at_k_norm": 1.0 + nrm(ks[19], (N_ATTN, HEAD_DIM), 0.01),
        "at_w_out": nrm(ks[20], (N_ATTN, ATTN_WIDTH, D_MODEL), DEEPNORM_BETA * ATTN_WIDTH ** -0.5),
        "at_b_out": nrm(ks[21], (N_ATTN, D_MODEL), 0.01),
        "ln_g": 1.0 + nrm(ks[22], (DEPTH, D_MODEL), 0.01),
        "ln_b": nrm(ks[23], (DEPTH, D_MODEL), 0.01),
    }


def reference(x_prompt, x_sample, hy_w_in, hy_b_in, hy_conv_w, hy_conv_b, hy_f_w1, hy_f_b1, hy_f_freq, hy_f_w2,
              hy_f_b2, hy_f_w3, hy_decay, hy_filt_bias, hy_w_out, hy_b_out, at_w_in, at_b_in, at_q_norm, at_k_norm,
              at_w_out, at_b_out, ln_g, ln_b):
    y_prompt = trunk(x_prompt, hy_w_in, hy_b_in, hy_conv_w, hy_conv_b, hy_f_w1, hy_f_b1, hy_f_freq, hy_f_w2,
                     hy_f_b2, hy_f_w3, hy_decay, hy_filt_bias, hy_w_out, hy_b_out, at_w_in, at_b_in, at_q_norm,
                     at_k_norm, at_w_out, at_b_out, ln_g, ln_b)
    y_sample = trunk(x_sample, hy_w_in, hy_b_in, hy_conv_w, hy_conv_b, hy_f_w1, hy_f_b1, hy_f_freq, hy_f_w2,
                     hy_f_b2, hy_f_w3, hy_decay, hy_filt_bias, hy_w_out, hy_b_out, at_w_in, at_b_in, at_q_norm,
                     at_k_norm, at_w_out, at_b_out, ln_g, ln_b)
    return (y_prompt, y_sample)
```

```python
import cmath
import functools
import math

import jax
import jax.numpy as jnp
from jax import lax
from jax.experimental import pallas as pl
from jax.experimental.pallas import tpu as pltpu

F32 = jnp.float32
BF16 = jnp.bfloat16

D_MODEL = 1024
DEPTH = 4
GRID_W = 64
HY_WIDTH = D_MODEL
HY_ORDER = 2
POS_EMB_DIM = 33
POS_BANDS = (POS_EMB_DIM - 1) // 2
FILTER_WIDTH = 64
FILTER_EPS = 1e-6
HEAD_DIM = 128
N_Q_HEADS = D_MODEL // HEAD_DIM
N_KV_HEADS = 2
GROUP = N_Q_HEADS // N_KV_HEADS
ATTN_WIDTH = N_Q_HEADS * HEAD_DIM
KV_WIDTH = N_KV_HEADS * HEAD_DIM
AXIS_DIM = HEAD_DIM // 2
ROPE_THETA = 10000.0
RMS_EPS = 1e-6
LN_EPS = 1e-5
DEEPNORM_ALPHA = (2.0 * DEPTH) ** 0.25

V7X_LANES = 128
V7X_SUBLANES = 8
V7X_MXU_DIM = 256
V7X_VMEM_BYTES = 64 * 1024 * 1024

CLASS_ROWS = V7X_MXU_DIM
CONV_TC = V7X_MXU_DIM
ATTN_TQ = 256
PROJ_TM = 256
FILT_PAD_K = 64


def _cparams(n_axes, vmem_mb):
    return pltpu.CompilerParams(
        dimension_semantics=("arbitrary",) * n_axes,
        vmem_limit_bytes=vmem_mb * 1024 * 1024,
    )


def _const_spec(block, index_map):
    return pl.BlockSpec(block, index_map, pipeline_mode=pl.Buffered(1))


def _add(a, b):
    if a is None:
        return b
    if b is None:
        return a
    return a + b


def _sub(a, b):
    if b is None:
        return a
    if a is None:
        return -b
    return a - b


def _scale(a, c):
    return None if a is None else a * c


def _twiddle(x, w):
    re, im = x
    c, s = w.real, w.imag
    if abs(s) < 1e-12:
        return (re, im) if c > 0 else (_sub(None, re), _sub(None, im))
    if abs(c) < 1e-12:
        return (_sub(None, im), re) if s > 0 else (im, _sub(None, re))
    if abs(abs(c) - abs(s)) < 1e-12:
        a = abs(c)
        total = _add(re, im)
        if c > 0 and s > 0:
            return (_scale(_sub(re, im), a), _scale(total, a))
        if c > 0:
            return (_scale(total, a), _scale(_sub(im, re), a))
        if s > 0:
            return (_scale(total, -a), _scale(_sub(re, im), a))
        return (_scale(_sub(im, re), a), _scale(total, -a))
    nre = _sub(_scale(re, c), _scale(im, s))
    nim = _add(_scale(re, s), _scale(im, c))
    return (nre, nim)


def _fft(xs, sign, first_half_only=False):
    n = len(xs)
    if n == 1:
        return list(xs)
    ev = _fft(xs[0::2], sign)
    od = _fft(xs[1::2], sign)
    out = [None] * n
    for k in range(n // 2):
        t = _twiddle(od[k], cmath.exp(sign * 2j * math.pi * k / n))
        out[k] = (_add(ev[k][0], t[0]), _add(ev[k][1], t[1]))
        if not first_half_only:
            out[k + n // 2] = (_sub(ev[k][0], t[0]), _sub(ev[k][1], t[1]))
    return out[: n // 2] if first_half_only else out


def _proj_kernel(x_ref, w_ref, b_ref, o_ref):
    x = x_ref[0].astype(BF16)
    acc = jnp.dot(x, w_ref[...], preferred_element_type=F32) + b_ref[...]
    o_ref[0] = acc.astype(o_ref.dtype)


def _hyena_proj(xv, w, b, r):
    B, F, _ = xv.shape
    L = F * r
    n_out = w.shape[1]
    tm = PROJ_TM
    nt = F // tm
    return pl.pallas_call(
        _proj_kernel,
        grid=(B, r, nt),
        in_specs=[
            pl.BlockSpec((1, tm, D_MODEL), lambda b, q, t: (b, t, q)),
            _const_spec((D_MODEL, n_out), lambda b, q, t: (0, 0)),
            _const_spec((1, n_out), lambda b, q, t: (0, 0)),
        ],
        out_specs=pl.BlockSpec((1, tm, n_out), lambda b, q, t: (b, q * nt + t, 0)),
        out_shape=jax.ShapeDtypeStruct((B, L, n_out), BF16),
        compiler_params=_cparams(3, 40),
        name="hyena_in_proj",
    )(xv, w, b)


def _attn_proj_kernel(x_ref, w_ref, b_ref, qn_ref, kn_ref, cos_ref, sin_ref, o_ref):
    x = x_ref[0].astype(BF16)
    acc = jnp.dot(x, w_ref[...], preferred_element_type=F32) + b_ref[...]
    cos2 = cos_ref[...]
    sin2 = sin_ref[...]

    def head(col, gain, scale):
        xh = acc[:, col:col + HEAD_DIM]
        ms = jnp.mean(xh * xh, axis=-1, keepdims=True)
        xn = xh * lax.rsqrt(ms + RMS_EPS) * gain
        rot = xn * cos2 + pltpu.roll(xn, AXIS_DIM, axis=1) * sin2
        if scale != 1.0:
            rot = rot * scale
        o_ref[0, :, col:col + HEAD_DIM] = rot.astype(o_ref.dtype)

    for h in range(N_Q_HEADS):
        head(h * HEAD_DIM, qn_ref[...], HEAD_DIM ** -0.5)
    o_ref[0, :, ATTN_WIDTH:2 * ATTN_WIDTH] = acc[:, ATTN_WIDTH:2 * ATTN_WIDTH].astype(o_ref.dtype)
    for h in range(N_KV_HEADS):
        head(2 * ATTN_WIDTH + h * HEAD_DIM, kn_ref[...], 1.0)
    v0 = 2 * ATTN_WIDTH + KV_WIDTH
    o_ref[0, :, v0:v0 + KV_WIDTH] = acc[:, v0:v0 + KV_WIDTH].astype(o_ref.dtype)


def _attn_proj(x, w, b, qn, kn, cos2, sin2):
    B, L, _ = x.shape
    n_out = w.shape[1]
    tm = PROJ_TM
    return pl.pallas_call(
        _attn_proj_kernel,
        grid=(B, L // tm),
        in_specs=[
            pl.BlockSpec((1, tm, D_MODEL), lambda b, t: (b, t, 0)),
            _const_spec((D_MODEL, n_out), lambda b, t: (0, 0)),
            _const_spec((1, n_out), lambda b, t: (0, 0)),
            _const_spec((1, HEAD_DIM), lambda b, t: (0, 0)),
            _const_spec((1, HEAD_DIM), lambda b, t: (0, 0)),
            pl.BlockSpec((tm, HEAD_DIM), lambda b, t: (t, 0)),
            pl.BlockSpec((tm, HEAD_DIM), lambda b, t: (t, 0)),
        ],
        out_specs=pl.BlockSpec((1, tm, n_out), lambda b, t: (b, t, 0)),
        out_shape=jax.ShapeDtypeStruct((B, L, n_out), BF16),
        compiler_params=_cparams(2, 40),
        name="attn_in_proj",
    )(x, w, b, qn, kn, cos2, sin2)


def _out_kernel(y_ref, g_ref, x_ref, w_ref, b_ref, lg_ref, lb_ref, o_ref):
    y = y_ref[0].astype(F32)
    g = g_ref[0].astype(F32)
    yg = (y * (g * jax.nn.sigmoid(g))).astype(BF16)
    s = jnp.dot(yg, w_ref[...], preferred_element_type=F32) + b_ref[...]
    v = DEEPNORM_ALPHA * x_ref[0] + s
    mu = jnp.mean(v, axis=-1, keepdims=True)
    vc = v - mu
    var = jnp.mean(vc * vc, axis=-1, keepdims=True)
    o_ref[0] = vc * lax.rsqrt(var + LN_EPS) * lg_ref[...] + lb_ref[...]


def _out_call(grid, y_spec, g_spec, x_spec, y, u, x, w, b, lg, lb, name):
    n = len(grid)
    zero = (lambda *a: (0, 0))
    return pl.pallas_call(
        _out_kernel,
        grid=grid,
        in_specs=[
            y_spec, g_spec, x_spec,
            _const_spec((D_MODEL, D_MODEL), zero),
            _const_spec((1, D_MODEL), zero),
            _const_spec((1, D_MODEL), zero),
            _const_spec((1, D_MODEL), zero),
        ],
        out_specs=x_spec,
        out_shape=jax.ShapeDtypeStruct(x.shape, F32),
        compiler_params=_cparams(n, 32),
        name=name,
    )(y, u, x, w, b, lg, lb)


def _hyena_out(z, u, xv, w, b, lg, lb, r):
    B, F, _ = xv.shape
    tm = PROJ_TM
    nt = F // tm
    gate_block = 3 * HY_WIDTH // D_MODEL
    return _out_call(
        (B, r, nt),
        pl.BlockSpec((1, tm, D_MODEL), lambda b_, q, t: (b_, q * nt + t, 0)),
        pl.BlockSpec((1, tm, D_MODEL), lambda b_, q, t: (b_, q * nt + t, gate_block)),
        pl.BlockSpec((1, tm, D_MODEL), lambda b_, q, t: (b_, t, q)),
        z, u, xv, w, b, lg, lb, "hyena_out_ln")


def _attn_out(o, u, x, w, b, lg, lb):
    B, L, _ = x.shape
    tm = PROJ_TM
    return _out_call(
        (B, L // tm),
        pl.BlockSpec((1, tm, D_MODEL), lambda b_, t: (b_, t, 0)),
        pl.BlockSpec((1, tm, D_MODEL), lambda b_, t: (b_, t, 1)),
        pl.BlockSpec((1, tm, D_MODEL), lambda b_, t: (b_, t, 0)),
        o, u, x, w, b, lg, lb, "attn_out_ln")


def _filter_kernel(feat_ref, w1_ref, b1_ref, fr_ref, w2_ref, b2_ref, w3f_ref, w3b_ref,
                   dcf_ref, dcb_ref, hf_ref, hb_ref):
    hi = lax.Precision.HIGHEST
    feats = feat_ref[...]
    fr = fr_ref[...]
    h = jnp.sin(fr * (jnp.dot(feats, w1_ref[...], precision=hi, preferred_element_type=F32) + b1_ref[...]))
    h = jnp.sin(fr * (jnp.dot(h, w2_ref[...], precision=hi, preferred_element_type=F32) + b2_ref[...]))
    t = feats[:, 0:1]
    hf = jnp.dot(h, w3f_ref[...], precision=hi, preferred_element_type=F32) * jnp.exp(-t * jnp.abs(dcf_ref[...]))
    hb = jnp.dot(h, w3b_ref[...], precision=hi, preferred_element_type=F32) * jnp.exp(-t * jnp.abs(dcb_ref[...]))
    ss = jnp.sum(hf * hf + hb * hb, axis=0, keepdims=True)
    sc = lax.rsqrt(ss + FILTER_EPS)
    hf_ref[...] = hf * sc
    row = lax.broadcasted_iota(jnp.int32, hb.shape, 0)
    hb_ref[...] = jnp.where(row == 0, 0.0, hb * sc)


def _hyena_filters(feats, w1p, b1, fr, w2, b2, w3, decay):
    L = feats.shape[0]
    E = HY_WIDTH
    tc = CONV_TC
    nct = E // tc
    zero = lambda o, c: (0, 0)
    col = lambda d: (lambda o, c: (0, (2 * o + d) * nct + c))
    out_spec = pl.BlockSpec((L, tc), lambda o, c: (0, o * nct + c))
    return pl.pallas_call(
        _filter_kernel,
        grid=(HY_ORDER, nct),
        in_specs=[
            _const_spec((L, FILT_PAD_K), zero),
            _const_spec((FILT_PAD_K, FILTER_WIDTH), zero),
            _const_spec((1, FILTER_WIDTH), zero),
            _const_spec((1, FILTER_WIDTH), zero),
            _const_spec((FILTER_WIDTH, FILTER_WIDTH), zero),
            _const_spec((1, FILTER_WIDTH), zero),
            pl.BlockSpec((FILTER_WIDTH, tc), col(0)),
            pl.BlockSpec((FILTER_WIDTH, tc), col(1)),
            pl.BlockSpec((1, tc), col(0)),
            pl.BlockSpec((1, tc), col(1)),
        ],
        out_specs=[out_spec, out_spec],
        out_shape=[jax.ShapeDtypeStruct((L, HY_ORDER * E), F32)] * 2,
        compiler_params=_cparams(2, 48),
        name="hyena_filter_taps",
    )(feats, w1p, b1, fr, w2, b2, w3, w3, decay, decay)


def _class_slices(ref, q, F, rows, lanes):
    return (ref[q, rows, lanes], ref[q, pl.ds(rows.start + F, rows.size), lanes])


def _kf_kernel(hf_ref, hb_ref, tf_ref, kf_ref, pa_s, pb_s, *, r, F, tc, inv_len):
    def fwd(src_ref, dst_ref):
        for q in range(r):
            h = src_ref[q * F:(q + 1) * F, :]
            h_hi = h.astype(BF16)
            h_lo = (h - h_hi.astype(F32)).astype(BF16)
            t = tf_ref[q]
            dst_ref[q] = (jnp.dot(t, h_hi, preferred_element_type=F32)
                          + jnp.dot(t, h_lo, preferred_element_type=F32))

    fwd(hf_ref, pa_s)
    fwd(hb_ref, pb_s)

    def body(i, carry):
        rows = pl.ds(pl.multiple_of(i * V7X_SUBLANES, V7X_SUBLANES), V7X_SUBLANES)
        for lc in range(tc // V7X_LANES):
            lanes = slice(lc * V7X_LANES, (lc + 1) * V7X_LANES)
            xa = _fft([_class_slices(pa_s, q, F, rows, lanes) for q in range(r)], -1.0)
            xb = _fft([_class_slices(pb_s, q, F, rows, lanes) for q in range(r)], -1.0)
            for j in range(r):
                kf_ref[0, j, rows, lanes] = (xa[j][0] + xb[j][0]) * inv_len
                kf_ref[0, j, pl.ds(rows.start + F, rows.size), lanes] = (xa[j][1] - xb[j][1]) * inv_len
        return carry

    lax.fori_loop(0, F // V7X_SUBLANES, body, 0)


def _filter_spectrum(hf, hb, tf, r):
    L = hf.shape[0]
    F = L // r
    E = HY_WIDTH
    tc = CONV_TC
    nct = E // tc
    kern = functools.partial(_kf_kernel, r=r, F=F, tc=tc, inv_len=1.0 / L)
    return pl.pallas_call(
        kern,
        grid=(HY_ORDER, nct),
        in_specs=[
            pl.BlockSpec((L, tc), lambda o, c: (0, o * nct + c)),
            pl.BlockSpec((L, tc), lambda o, c: (0, o * nct + c)),
            _const_spec((r, 2 * F, F), lambda o, c: (0, 0, 0)),
        ],
        out_specs=pl.BlockSpec((1, r, 2 * F, tc), lambda o, c: (o, 0, 0, c)),
        out_shape=jax.ShapeDtypeStruct((HY_ORDER, r, 2 * F, E), F32),
        scratch_shapes=[pltpu.VMEM((r, 2 * F, tc), F32), pltpu.VMEM((r, 2 * F, tc), F32)],
        compiler_params=_cparams(2, 56),
        name="hyena_filter_spectrum",
    )(hf, hb, tf)


def _short_conv_class(u_ref, w, b, q, r, F):
    def cls(k):
        return u_ref[0, k * F:(k + 1) * F, :].astype(F32)

    cur = cls(q)
    row = lax.broadcasted_iota(jnp.int32, cur.shape, 0)
    if q > 0:
        prev = cls(q - 1)
    else:
        prev = jnp.where(row == 0, 0.0, pltpu.roll(cls(r - 1), 1, axis=0))
    if q < r - 1:
        nxt = cls(q + 1)
    else:
        nxt = jnp.where(row == F - 1, 0.0, pltpu.roll(cls(0), F - 1, axis=0))
    return prev * w[0:1] + cur * w[1:2] + nxt * w[2:3] + b


def _conv_kernel(uv_ref, ug0_ref, ug1_ref, kf_ref, tf_ref, ti_ref, cw_ref, cb_ref, fb_ref,
                 out_ref, z_s, p_s, *, r, F, tc):
    for q in range(r):
        z_s[q] = _short_conv_class(uv_ref, cw_ref[0], cb_ref[0:1, :], q, r, F)

    gate_refs = (ug0_ref, ug1_ref)
    for o in range(HY_ORDER):
        for q in range(r):
            p_s[q] = jnp.dot(tf_ref[q], z_s[q].astype(BF16), preferred_element_type=F32)

        def body(i, carry, o=o):
            rows = pl.ds(pl.multiple_of(i * V7X_SUBLANES, V7X_SUBLANES), V7X_SUBLANES)
            rows_im = pl.ds(rows.start + F, rows.size)
            for lc in range(tc // V7X_LANES):
                lanes = slice(lc * V7X_LANES, (lc + 1) * V7X_LANES)
                x = _fft([_class_slices(p_s, q, F, rows, lanes) for q in range(r)], -1.0)
                y = []
                for j in range(r):
                    kre = kf_ref[o, j, rows, lanes]
                    kim = kf_ref[o, j, rows_im, lanes]
                    xre, xim = x[j]
                    y.append((xre * kre - xim * kim, xre * kim + xim * kre))
                a = _fft(y, 1.0)
                for q in range(r):
                    p_s[q, rows, lanes] = a[q][0]
                    p_s[q, rows_im, lanes] = a[q][1]
            return carry

        lax.fori_loop(0, F // V7X_SUBLANES, body, 0)

        for q in range(r):
            y = jnp.dot(ti_ref[q], p_s[q].astype(BF16), preferred_element_type=F32)
            g = _short_conv_class(gate_refs[o], cw_ref[1 + o], cb_ref[1 + o:2 + o, :], q, r, F)
            z_new = g * (y + z_s[q] * fb_ref[o:o + 1, :])
            if o == HY_ORDER - 1:
                out_ref[0, q * F:(q + 1) * F, :] = z_new.astype(out_ref.dtype)
            else:
                z_s[q] = z_new


def _hyena_conv(u, kf, tf, ti, cw, cb, fb, r):
    B, L, _ = u.shape
    F = L // r
    E = HY_WIDTH
    tc = CONV_TC
    nct = E // tc
    kern = functools.partial(_conv_kernel, r=r, F=F, tc=tc)
    sec = lambda s: pl.BlockSpec((1, L, tc), lambda c, b: (b, 0, s * nct + c))
    return pl.pallas_call(
        kern,
        grid=(nct, B),
        in_specs=[
            sec(0), sec(1), sec(2),
            _const_spec((HY_ORDER, r, 2 * F, tc), lambda c, b: (0, 0, 0, c)),
            _const_spec((r, 2 * F, F), lambda c, b: (0, 0, 0)),
            _const_spec((r, F, 2 * F), lambda c, b: (0, 0, 0)),
            _const_spec((3, 3, tc), lambda c, b: (0, 0, c)),
            _const_spec((3, tc), lambda c, b: (0, c)),
            _const_spec((HY_ORDER, tc), lambda c, b: (0, c)),
        ],
        out_specs=pl.BlockSpec((1, L, tc), lambda c, b: (b, 0, c)),
        out_shape=jax.ShapeDtypeStruct((B, L, E), BF16),
        scratch_shapes=[pltpu.VMEM((r, F, tc), F32), pltpu.VMEM((r, 2 * F, tc), F32)],
        compiler_params=_cparams(2, 56),
        name="hyena_long_conv",
    )(u, u, u, kf, tf, ti, cw, cb, fb)


def _attn_kernel(q_ref, k_ref, v_ref, o_ref):
    k = k_ref[0]
    v = v_ref[0]
    for g in range(GROUP):
        q = q_ref[0, :, g * HEAD_DIM:(g + 1) * HEAD_DIM]
        s = lax.dot_general(q, k, (((1,), (1,)), ((), ())), preferred_element_type=F32)
        m = jnp.max(s, axis=-1, keepdims=True)
        p = jnp.exp(s - m)
        denom = jnp.sum(p, axis=-1, keepdims=True)
        o = jnp.dot(p.astype(BF16), v, preferred_element_type=F32) / denom
        o_ref[0, :, g * HEAD_DIM:(g + 1) * HEAD_DIM] = o.astype(o_ref.dtype)


def _attention(u):
    B, L, _ = u.shape
    tq = ATTN_TQ
    gw = GROUP * HEAD_DIM
    k_block = 2 * ATTN_WIDTH // HEAD_DIM
    v_block = (2 * ATTN_WIDTH + KV_WIDTH) // HEAD_DIM
    return pl.pallas_call(
        _attn_kernel,
        grid=(B, N_KV_HEADS, L // tq),
        in_specs=[
            pl.BlockSpec((1, tq, gw), lambda b, h, t: (b, t, h)),
            pl.BlockSpec((1, L, HEAD_DIM), lambda b, h, t: (b, 0, k_block + h)),
            pl.BlockSpec((1, L, HEAD_DIM), lambda b, h, t: (b, 0, v_block + h)),
        ],
        out_specs=pl.BlockSpec((1, tq, gw), lambda b, h, t: (b, t, h)),
        out_shape=jax.ShapeDtypeStruct((B, L, ATTN_WIDTH), BF16),
        compiler_params=_cparams(3, 48),
        name="gqa_attention",
    )(u, u, u)


def _class_dft_tables(L, r):
    F = L // r
    f = jnp.arange(F, dtype=jnp.int32)
    m = jnp.arange(F, dtype=jnp.int32)
    q = jnp.arange(r, dtype=jnp.int32)
    n = ((2 * f + 1)[None, :, None] * (r * m[None, None, :] + q[:, None, None])) % (4 * L)
    theta = n.astype(F32) * (math.pi / (2 * L))
    c = jnp.cos(theta)
    s = jnp.sin(theta)
    tf = jnp.concatenate([c, -s], axis=1).astype(BF16)
    ti = jnp.concatenate([c.transpose(0, 2, 1), -s.transpose(0, 2, 1)], axis=2).astype(BF16)
    return tf, ti


def _filter_features(L, r):
    t = jnp.linspace(0.0, 1.0, L, dtype=F32)[:, None]
    w = 2.0 * math.pi * jnp.arange(L, dtype=F32)[:, None] / L
    f = jnp.linspace(1e-4, POS_BANDS - 1, POS_BANDS, dtype=F32)[None]
    feats = jnp.concatenate([t, jnp.cos(f * w), -jnp.sin(f * w)], axis=-1)
    feats = jnp.pad(feats, ((0, 0), (0, FILT_PAD_K - POS_EMB_DIM)))
    F = L // r
    return feats.reshape(F, r, FILT_PAD_K).transpose(1, 0, 2).reshape(L, FILT_PAD_K)


def _rope_tables(L):
    rows = L // GRID_W
    row = jnp.repeat(jnp.arange(rows, dtype=F32), GRID_W)
    col = jnp.tile(jnp.arange(GRID_W, dtype=F32), rows)
    inv = ROPE_THETA ** (-jnp.arange(0, AXIS_DIM, 2, dtype=F32) / AXIS_DIM)
    ang = jnp.concatenate([row[:, None] * inv, col[:, None] * inv], axis=-1)
    c, s = jnp.cos(ang), jnp.sin(ang)
    return jnp.concatenate([c, c], axis=-1), jnp.concatenate([-s, s], axis=-1)


def _trunk(x, p):
    B, L, _ = x.shape
    r = L // CLASS_ROWS
    F = CLASS_ROWS
    tf, ti = _class_dft_tables(L, r)
    feats = _filter_features(L, r)
    cos2, sin2 = _rope_tables(L)
    for i in range(DEPTH):
        j = i // 2
        lg = p["ln_g"][i][None]
        lb = p["ln_b"][i][None]
        if i % 2 == 0:
            w1p = jnp.pad(p["hy_f_w1"][j], ((0, FILT_PAD_K - POS_EMB_DIM), (0, 0)))
            hf, hb = _hyena_filters(feats, w1p, p["hy_f_b1"][j][None], p["hy_f_freq"][j][None],
                                    p["hy_f_w2"][j], p["hy_f_b2"][j][None], p["hy_f_w3"][j],
                                    p["hy_decay"][j][None])
            kf = _filter_spectrum(hf, hb, tf, r)
            xv = x.reshape(B, F, r * D_MODEL)
            u = _hyena_proj(xv, p["hy_w_in"][j].astype(BF16), p["hy_b_in"][j][None], r)
            cw = p["hy_conv_w"][j].reshape(3, 3, HY_WIDTH).transpose(1, 0, 2)
            cb = p["hy_conv_b"][j].reshape(3, HY_WIDTH)
            z = _hyena_conv(u, kf, tf, ti, cw, cb, p["hy_filt_bias"][j], r)
            xv = _hyena_out(z, u, xv, p["hy_w_out"][j].astype(BF16), p["hy_b_out"][j][None], lg, lb, r)
            x = xv.reshape(B, L, D_MODEL)
        else:
            w = p["at_w_in"][j]
            bi = p["at_b_in"][j]
            kv0 = ATTN_WIDTH
            g0 = ATTN_WIDTH + 2 * KV_WIDTH
            w = jnp.concatenate([w[:, :kv0], w[:, g0:], w[:, kv0:g0]], axis=1).astype(BF16)
            bi = jnp.concatenate([bi[:kv0], bi[g0:], bi[kv0:g0]])[None]
            u = _attn_proj(x, w, bi, p["at_q_norm"][j][None], p["at_k_norm"][j][None], cos2, sin2)
            o = _attention(u)
            x = _attn_out(o, u, x, p["at_w_out"][j].astype(BF16), p["at_b_out"][j][None], lg, lb)
    return x


def kernel(x_prompt, x_sample, hy_w_in, hy_b_in, hy_conv_w, hy_conv_b, hy_f_w1, hy_f_b1, hy_f_freq, hy_f_w2,
           hy_f_b2, hy_f_w3, hy_decay, hy_filt_bias, hy_w_out, hy_b_out, at_w_in, at_b_in, at_q_norm, at_k_norm,
           at_w_out, at_b_out, ln_g, ln_b):
    p = dict(hy_w_in=hy_w_in, hy_b_in=hy_b_in, hy_conv_w=hy_conv_w, hy_conv_b=hy_conv_b, hy_f_w1=hy_f_w1,
             hy_f_b1=hy_f_b1, hy_f_freq=hy_f_freq, hy_f_w2=hy_f_w2, hy_f_b2=hy_f_b2, hy_f_w3=hy_f_w3,
             hy_decay=hy_decay, hy_filt_bias=hy_filt_bias, hy_w_out=hy_w_out, hy_b_out=hy_b_out,
             at_w_in=at_w_in, at_b_in=at_b_in, at_q_norm=at_q_norm, at_k_norm=at_k_norm, at_w_out=at_w_out,
             at_b_out=at_b_out, ln_g=ln_g, ln_b=ln_b)
    return (_trunk(x_prompt, p), _trunk(x_sample, p))
```

```python
import cmath
import functools
import math

import jax
import jax.numpy as jnp
from jax import lax
from jax.experimental import pallas as pl
from jax.experimental.pallas import tpu as pltpu

F32 = jnp.float32
BF16 = jnp.bfloat16

D_MODEL = 1024
DEPTH = 4
GRID_W = 64
HY_WIDTH = D_MODEL
HY_ORDER = 2
POS_EMB_DIM = 33
POS_BANDS = (POS_EMB_DIM - 1) // 2
FILTER_WIDTH = 64
FILTER_EPS = 1e-6
HEAD_DIM = 128
N_Q_HEADS = D_MODEL // HEAD_DIM
N_KV_HEADS = 2
GROUP = N_Q_HEADS // N_KV_HEADS
ATTN_WIDTH = N_Q_HEADS * HEAD_DIM
KV_WIDTH = N_KV_HEADS * HEAD_DIM
AXIS_DIM = HEAD_DIM // 2
ROPE_THETA = 10000.0
RMS_EPS = 1e-6
LN_EPS = 1e-5
DEEPNORM_ALPHA = (2.0 * DEPTH) ** 0.25

V7X_LANES = 128
V7X_SUBLANES = 8
V7X_MXU_DIM = 256
V7X_VMEM_BYTES = 64 * 1024 * 1024

CLASS_ROWS = V7X_MXU_DIM
CONV_TC = V7X_MXU_DIM
ATTN_TQ = 256
PROJ_TM = 256
OUT_TM = 512
HYENA_TM = 512
FILT_PAD_K = 64


def _cparams(n_axes, vmem_mb):
    return pltpu.CompilerParams(
        dimension_semantics=("arbitrary",) * n_axes,
        vmem_limit_bytes=vmem_mb * 1024 * 1024,
    )


def _const_spec(block, index_map):
    return pl.BlockSpec(block, index_map, pipeline_mode=pl.Buffered(1))


def _add(a, b):
    if a is None:
        return b
    if b is None:
        return a
    return a + b


def _sub(a, b):
    if b is None:
        return a
    if a is None:
        return -b
    return a - b


def _scale(a, c):
    return None if a is None else a * c


def _twiddle(x, w):
    re, im = x
    c, s = w.real, w.imag
    if abs(s) < 1e-12:
        return (re, im) if c > 0 else (_sub(None, re), _sub(None, im))
    if abs(c) < 1e-12:
        return (_sub(None, im), re) if s > 0 else (im, _sub(None, re))
    if abs(abs(c) - abs(s)) < 1e-12:
        a = abs(c)
        total = _add(re, im)
        if c > 0 and s > 0:
            return (_scale(_sub(re, im), a), _scale(total, a))
        if c > 0:
            return (_scale(total, a), _scale(_sub(im, re), a))
        if s > 0:
            return (_scale(total, -a), _scale(_sub(re, im), a))
        return (_scale(_sub(im, re), a), _scale(total, -a))
    nre = _sub(_scale(re, c), _scale(im, s))
    nim = _add(_scale(re, s), _scale(im, c))
    return (nre, nim)


def _fft(xs, sign, first_half_only=False):
    n = len(xs)
    if n == 1:
        return list(xs)
    ev = _fft(xs[0::2], sign)
    od = _fft(xs[1::2], sign)
    out = [None] * n
    for k in range(n // 2):
        t = _twiddle(od[k], cmath.exp(sign * 2j * math.pi * k / n))
        out[k] = (_add(ev[k][0], t[0]), _add(ev[k][1], t[1]))
        if not first_half_only:
            out[k + n // 2] = (_sub(ev[k][0], t[0]), _sub(ev[k][1], t[1]))
    return out[: n // 2] if first_half_only else out


def _proj_kernel(x_ref, w_ref, b_ref, o_ref):
    x = x_ref[0].astype(BF16)
    acc = jnp.dot(x, w_ref[...], preferred_element_type=F32) + b_ref[...]
    o_ref[0] = acc.astype(o_ref.dtype)


def _hyena_proj(x, w, b):
    B, L, _ = x.shape
    n_out = w.shape[1]
    tm = HYENA_TM
    return pl.pallas_call(
        _proj_kernel,
        grid=(B, L // tm),
        in_specs=[
            pl.BlockSpec((1, tm, D_MODEL), lambda b, t: (b, t, 0)),
            _const_spec((D_MODEL, n_out), lambda b, t: (0, 0)),
            _const_spec((1, n_out), lambda b, t: (0, 0)),
        ],
        out_specs=pl.BlockSpec((1, tm, n_out), lambda b, t: (b, t, 0)),
        out_shape=jax.ShapeDtypeStruct((B, L, n_out), BF16),
        compiler_params=_cparams(2, 48),
        name="hyena_in_proj",
    )(x, w, b)


def _attn_proj_kernel(x_ref, w_ref, b_ref, qn_ref, kn_ref, cos_ref, sin_ref, o_ref):
    x = x_ref[0].astype(BF16)
    acc = jnp.dot(x, w_ref[...], preferred_element_type=F32) + b_ref[...]
    cos2 = cos_ref[...]
    sin2 = sin_ref[...]

    def head(col, gain, scale):
        xh = acc[:, col:col + HEAD_DIM]
        ms = jnp.mean(xh * xh, axis=-1, keepdims=True)
        xn = xh * lax.rsqrt(ms + RMS_EPS) * gain
        rot = xn * cos2 + pltpu.roll(xn, AXIS_DIM, axis=1) * sin2
        if scale != 1.0:
            rot = rot * scale
        o_ref[0, :, col:col + HEAD_DIM] = rot.astype(o_ref.dtype)

    for h in range(N_Q_HEADS):
        head(h * HEAD_DIM, qn_ref[...], HEAD_DIM ** -0.5 * math.log2(math.e))
    o_ref[0, :, ATTN_WIDTH:2 * ATTN_WIDTH] = acc[:, ATTN_WIDTH:2 * ATTN_WIDTH].astype(o_ref.dtype)
    for h in range(N_KV_HEADS):
        head(2 * ATTN_WIDTH + h * HEAD_DIM, kn_ref[...], 1.0)
    v0 = 2 * ATTN_WIDTH + KV_WIDTH
    o_ref[0, :, v0:v0 + KV_WIDTH] = acc[:, v0:v0 + KV_WIDTH].astype(o_ref.dtype)


def _attn_proj(x, w, b, qn, kn, cos2, sin2):
    B, L, _ = x.shape
    n_out = w.shape[1]
    tm = PROJ_TM
    return pl.pallas_call(
        _attn_proj_kernel,
        grid=(B, L // tm),
        in_specs=[
            pl.BlockSpec((1, tm, D_MODEL), lambda b, t: (b, t, 0)),
            _const_spec((D_MODEL, n_out), lambda b, t: (0, 0)),
            _const_spec((1, n_out), lambda b, t: (0, 0)),
            _const_spec((1, HEAD_DIM), lambda b, t: (0, 0)),
            _const_spec((1, HEAD_DIM), lambda b, t: (0, 0)),
            pl.BlockSpec((tm, HEAD_DIM), lambda b, t: (t, 0)),
            pl.BlockSpec((tm, HEAD_DIM), lambda b, t: (t, 0)),
        ],
        out_specs=pl.BlockSpec((1, tm, n_out), lambda b, t: (b, t, 0)),
        out_shape=jax.ShapeDtypeStruct((B, L, n_out), BF16),
        compiler_params=_cparams(2, 40),
        name="attn_in_proj",
    )(x, w, b, qn, kn, cos2, sin2)


def _gated_out_ln(y, g, x, w_ref, b_ref, lg_ref, lb_ref):
    y = y.astype(F32)
    g = g.astype(F32)
    yg = (y * (g * jax.nn.sigmoid(g))).astype(BF16)
    s = jnp.dot(yg, w_ref[...], preferred_element_type=F32) + b_ref[...]
    v = DEEPNORM_ALPHA * x + s
    mu = jnp.mean(v, axis=-1, keepdims=True)
    vc = v - mu
    var = jnp.mean(vc * vc, axis=-1, keepdims=True)
    return vc * lax.rsqrt(var + LN_EPS) * lg_ref[...] + lb_ref[...]


def _out_kernel(y_ref, g_ref, x_ref, w_ref, b_ref, lg_ref, lb_ref, o_ref):
    o_ref[0] = _gated_out_ln(y_ref[0], g_ref[0], x_ref[0], w_ref, b_ref, lg_ref, lb_ref)


def _out_proj_ln(y, u, gate_block, x, w, b, lg, lb, name):
    B, L, _ = x.shape
    tm = OUT_TM
    x_spec = pl.BlockSpec((1, tm, D_MODEL), lambda b_, t: (b_, t, 0))
    zero = lambda b_, t: (0, 0)
    return pl.pallas_call(
        _out_kernel,
        grid=(B, L // tm),
        in_specs=[
            x_spec,
            pl.BlockSpec((1, tm, D_MODEL), lambda b_, t: (b_, t, gate_block)),
            x_spec,
            _const_spec((D_MODEL, D_MODEL), zero),
            _const_spec((1, D_MODEL), zero),
            _const_spec((1, D_MODEL), zero),
            _const_spec((1, D_MODEL), zero),
        ],
        out_specs=x_spec,
        out_shape=jax.ShapeDtypeStruct(x.shape, F32),
        compiler_params=_cparams(2, 40),
        name=name,
    )(y, u, x, w, b, lg, lb)


def _filter_kernel(feat_ref, w1_ref, b1_ref, fr_ref, w2_ref, b2_ref, w3f_ref, w3b_ref,
                   dcf_ref, dcb_ref, hf_ref, hb_ref, h_s):
    hi = lax.Precision.HIGHEST

    @pl.when((pl.program_id(0) == 0) & (pl.program_id(1) == 0))
    def _():
        fr = fr_ref[...]
        h = jnp.sin(fr * (jnp.dot(feat_ref[...], w1_ref[...], precision=hi, preferred_element_type=F32)
                          + b1_ref[...]))
        h_s[...] = jnp.sin(fr * (jnp.dot(h, w2_ref[...], precision=hi, preferred_element_type=F32)
                                 + b2_ref[...]))

    h = h_s[...]
    t = feat_ref[:, 0:1]
    hf = jnp.dot(h, w3f_ref[...], precision=hi, preferred_element_type=F32) * jnp.exp(-t * jnp.abs(dcf_ref[...]))
    hb = jnp.dot(h, w3b_ref[...], precision=hi, preferred_element_type=F32) * jnp.exp(-t * jnp.abs(dcb_ref[...]))
    ss = jnp.sum(hf * hf + hb * hb, axis=0, keepdims=True)
    sc = lax.rsqrt(ss + FILTER_EPS)
    hf_ref[...] = hf * sc
    row = lax.broadcasted_iota(jnp.int32, hb.shape, 0)
    hb_ref[...] = jnp.where(row == 0, 0.0, hb * sc)


def _hyena_filters(feats, w1p, b1, fr, w2, b2, w3, decay):
    L = feats.shape[0]
    E = HY_WIDTH
    tc = CONV_TC
    nct = E // tc
    zero = lambda o, c: (0, 0)
    col = lambda d: (lambda o, c: (0, (2 * o + d) * nct + c))
    out_spec = pl.BlockSpec((L, tc), lambda o, c: (0, o * nct + c))
    return pl.pallas_call(
        _filter_kernel,
        grid=(HY_ORDER, nct),
        in_specs=[
            _const_spec((L, FILT_PAD_K), zero),
            _const_spec((FILT_PAD_K, FILTER_WIDTH), zero),
            _const_spec((1, FILTER_WIDTH), zero),
            _const_spec((1, FILTER_WIDTH), zero),
            _const_spec((FILTER_WIDTH, FILTER_WIDTH), zero),
            _const_spec((1, FILTER_WIDTH), zero),
            pl.BlockSpec((FILTER_WIDTH, tc), col(0)),
            pl.BlockSpec((FILTER_WIDTH, tc), col(1)),
            pl.BlockSpec((1, tc), col(0)),
            pl.BlockSpec((1, tc), col(1)),
        ],
        out_specs=[out_spec, out_spec],
        out_shape=[jax.ShapeDtypeStruct((L, HY_ORDER * E), F32)] * 2,
        scratch_shapes=[pltpu.VMEM((L, FILTER_WIDTH), F32)],
        compiler_params=_cparams(2, 48),
        name="hyena_filter_taps",
    )(feats, w1p, b1, fr, w2, b2, w3, w3, decay, decay)


def _class_slices(ref, q, F, rows, lanes):
    return (ref[q, rows, lanes], ref[q, pl.ds(rows.start + F, rows.size), lanes])


def _kf_kernel(hf_ref, hb_ref, tf_ref, kf_ref, pa_s, pb_s, *, r, F, tc, inv_len):
    def fwd(src_ref, dst_ref):
        for q in range(r):
            h = src_ref[q * F:(q + 1) * F, :]
            h_hi = h.astype(BF16)
            h_lo = (h - h_hi.astype(F32)).astype(BF16)
            t = tf_ref[q]
            dst_ref[q] = (jnp.dot(t, h_hi, preferred_element_type=F32)
                          + jnp.dot(t, h_lo, preferred_element_type=F32))

    fwd(hf_ref, pa_s)
    fwd(hb_ref, pb_s)

    def body(i, carry):
        rows = pl.ds(pl.multiple_of(i * V7X_SUBLANES, V7X_SUBLANES), V7X_SUBLANES)
        for lc in range(tc // V7X_LANES):
            lanes = slice(lc * V7X_LANES, (lc + 1) * V7X_LANES)
            xa = _fft([_class_slices(pa_s, q, F, rows, lanes) for q in range(r)], -1.0)
            xb = _fft([_class_slices(pb_s, q, F, rows, lanes) for q in range(r)], -1.0)
            for j in range(r):
                kf_ref[0, j, rows, lanes] = (xa[j][0] + xb[j][0]) * inv_len
                kf_ref[0, j, pl.ds(rows.start + F, rows.size), lanes] = (xa[j][1] - xb[j][1]) * inv_len
        return carry

    lax.fori_loop(0, F // V7X_SUBLANES, body, 0)


def _filter_spectrum(hf, hb, tf, r):
    L = hf.shape[0]
    F = L // r
    E = HY_WIDTH
    tc = CONV_TC
    nct = E // tc
    kern = functools.partial(_kf_kernel, r=r, F=F, tc=tc, inv_len=1.0 / L)
    return pl.pallas_call(
        kern,
        grid=(HY_ORDER, nct),
        in_specs=[
            pl.BlockSpec((L, tc), lambda o, c: (0, o * nct + c)),
            pl.BlockSpec((L, tc), lambda o, c: (0, o * nct + c)),
            _const_spec((r, 2 * F, F), lambda o, c: (0, 0, 0)),
        ],
        out_specs=pl.BlockSpec((1, r, 2 * F, tc), lambda o, c: (o, 0, 0, c)),
        out_shape=jax.ShapeDtypeStruct((HY_ORDER, r, 2 * F, E), F32),
        scratch_shapes=[pltpu.VMEM((r, 2 * F, tc), F32), pltpu.VMEM((r, 2 * F, tc), F32)],
        compiler_params=_cparams(2, 56),
        name="hyena_filter_spectrum",
    )(hf, hb, tf)


def _short_conv_class(u_ref, w, b, q, r, F):
    def cls(k):
        return u_ref[0, k * F:(k + 1) * F, :].astype(F32)

    cur = cls(q)
    row = lax.broadcasted_iota(jnp.int32, cur.shape, 0)
    if q > 0:
        prev = cls(q - 1)
    else:
        prev = jnp.where(row == 0, 0.0, pltpu.roll(cls(r - 1), 1, axis=0))
    if q < r - 1:
        nxt = cls(q + 1)
    else:
        nxt = jnp.where(row == F - 1, 0.0, pltpu.roll(cls(0), F - 1, axis=0))
    return prev * w[0:1] + cur * w[1:2] + nxt * w[2:3] + b


def _conv_kernel(uv_ref, ug0_ref, ug1_ref, kf_ref, tf_ref, ti_ref, cw_ref, cb_ref, fb_ref,
                 out_ref, z_s, p_s, *, r, F, tc):
    for q in range(r):
        z_s[q] = _short_conv_class(uv_ref, cw_ref[0], cb_ref[0:1, :], q, r, F)

    gate_refs = (ug0_ref, ug1_ref)
    for o in range(HY_ORDER):
        for q in range(r):
            p_s[q] = jnp.dot(tf_ref[q], z_s[q].astype(BF16), preferred_element_type=F32)

        def body(i, carry, o=o):
            rows = pl.ds(pl.multiple_of(i * V7X_SUBLANES, V7X_SUBLANES), V7X_SUBLANES)
            rows_im = pl.ds(rows.start + F, rows.size)
            for lc in range(tc // V7X_LANES):
                lanes = slice(lc * V7X_LANES, (lc + 1) * V7X_LANES)
                x = _fft([_class_slices(p_s, q, F, rows, lanes) for q in range(r)], -1.0)
                y = []
                for j in range(r):
                    kre = kf_ref[o, j, rows, lanes]
                    kim = kf_ref[o, j, rows_im, lanes]
                    xre, xim = x[j]
                    y.append((xre * kre - xim * kim, xre * kim + xim * kre))
                a = _fft(y, 1.0)
                for q in range(r):
                    p_s[q, rows, lanes] = a[q][0]
                    p_s[q, rows_im, lanes] = a[q][1]
            return carry

        lax.fori_loop(0, F // V7X_SUBLANES, body, 0, unroll=2)

        for q in range(r):
            y = jnp.dot(ti_ref[q], p_s[q].astype(BF16), preferred_element_type=F32)
            g = _short_conv_class(gate_refs[o], cw_ref[1 + o], cb_ref[1 + o:2 + o, :], q, r, F)
            z_new = g * (y + z_s[q] * fb_ref[o:o + 1, :])
            if o == HY_ORDER - 1:
                out_ref[0, q * F:(q + 1) * F, :] = z_new.astype(out_ref.dtype)
            else:
                z_s[q] = z_new


def _hyena_conv(u, kf, tf, ti, cw, cb, fb, r):
    B, L, _ = u.shape
    F = L // r
    E = HY_WIDTH
    tc = CONV_TC
    nct = E // tc
    kern = functools.partial(_conv_kernel, r=r, F=F, tc=tc)
    sec = lambda s: pl.BlockSpec((1, L, tc), lambda c, b: (b, 0, s * nct + c))
    return pl.pallas_call(
        kern,
        grid=(nct, B),
        in_specs=[
            sec(0), sec(1), sec(2),
            _const_spec((HY_ORDER, r, 2 * F, tc), lambda c, b: (0, 0, 0, c)),
            _const_spec((r, 2 * F, F), lambda c, b: (0, 0, 0)),
            _const_spec((r, F, 2 * F), lambda c, b: (0, 0, 0)),
            _const_spec((3, 3, tc), lambda c, b: (0, 0, c)),
            _const_spec((3, tc), lambda c, b: (0, c)),
            _const_spec((HY_ORDER, tc), lambda c, b: (0, c)),
        ],
        out_specs=pl.BlockSpec((1, L, tc), lambda c, b: (b, 0, c)),
        out_shape=jax.ShapeDtypeStruct((B, L, E), BF16),
        scratch_shapes=[pltpu.VMEM((r, F, tc), F32), pltpu.VMEM((r, 2 * F, tc), F32)],
        compiler_params=_cparams(2, 56),
        name="hyena_long_conv",
    )(u, u, u, kf, tf, ti, cw, cb, fb)


def _attn_kernel(q_ref, k_ref, v_ref, o_ref, vt_s, st_s, p_s):
    @pl.when(pl.program_id(2) == 0)
    def _():
        vt_s[...] = v_ref[0].T

    k = k_ref[0]
    col_max = [None] * GROUP
    inv = [None] * GROUP

    def scores(g):
        q = q_ref[0, :, g * HEAD_DIM:(g + 1) * HEAD_DIM]
        st = lax.dot_general(k, q, (((1,), (1,)), ((), ())), preferred_element_type=F32)
        st_s[g] = st
        col_max[g] = jnp.max(st, axis=0, keepdims=True)

    def probs(g):
        p = jnp.exp2(st_s[g] - col_max[g])
        inv[g] = 1.0 / jnp.sum(p, axis=0, keepdims=True)
        p_s[g] = p.astype(BF16)

    def values(g):
        ot = jnp.dot(vt_s[...], p_s[g], preferred_element_type=F32) * inv[g]
        o_ref[0, :, g * HEAD_DIM:(g + 1) * HEAD_DIM] = ot.T.astype(o_ref.dtype)

    for stage in (scores, probs, values):
        for g in range(GROUP):
            stage(g)


def _attention(u):
    B, L, _ = u.shape
    tq = ATTN_TQ
    gw = GROUP * HEAD_DIM
    k_block = 2 * ATTN_WIDTH // HEAD_DIM
    v_block = (2 * ATTN_WIDTH + KV_WIDTH) // HEAD_DIM
    return pl.pallas_call(
        _attn_kernel,
        grid=(B, N_KV_HEADS, L // tq),
        in_specs=[
            pl.BlockSpec((1, tq, gw), lambda b, h, t: (b, t, h)),
            pl.BlockSpec((1, L, HEAD_DIM), lambda b, h, t: (b, 0, k_block + h)),
            pl.BlockSpec((1, L, HEAD_DIM), lambda b, h, t: (b, 0, v_block + h)),
        ],
        out_specs=pl.BlockSpec((1, tq, gw), lambda b, h, t: (b, t, h)),
        out_shape=jax.ShapeDtypeStruct((B, L, ATTN_WIDTH), BF16),
        scratch_shapes=[pltpu.VMEM((HEAD_DIM, L), BF16),
                        pltpu.VMEM((GROUP, L, tq), F32),
                        pltpu.VMEM((GROUP, L, tq), BF16)],
        compiler_params=_cparams(3, 56),
        name="gqa_attention",
    )(u, u, u)


def _class_dft_tables(L, r):
    F = L // r
    f = jnp.arange(F, dtype=jnp.int32)
    m = jnp.arange(F, dtype=jnp.int32)
    q = jnp.arange(r, dtype=jnp.int32)
    n = ((2 * f + 1)[None, :, None] * (r * m[None, None, :] + q[:, None, None])) % (4 * L)
    theta = n.astype(F32) * (math.pi / (2 * L))
    c = jnp.cos(theta)
    s = jnp.sin(theta)
    tf = jnp.concatenate([c, -s], axis=1).astype(BF16)
    ti = jnp.concatenate([c.transpose(0, 2, 1), -s.transpose(0, 2, 1)], axis=2).astype(BF16)
    return tf, ti


def _filter_features(L):
    t = jnp.linspace(0.0, 1.0, L, dtype=F32)[:, None]
    w = 2.0 * math.pi * jnp.arange(L, dtype=F32)[:, None] / L
    f = jnp.linspace(1e-4, POS_BANDS - 1, POS_BANDS, dtype=F32)[None]
    feats = jnp.concatenate([t, jnp.cos(f * w), -jnp.sin(f * w)], axis=-1)
    return jnp.pad(feats, ((0, 0), (0, FILT_PAD_K - POS_EMB_DIM)))


def _rope_tables(L):
    rows = L // GRID_W
    row = jnp.repeat(jnp.arange(rows, dtype=F32), GRID_W)
    col = jnp.tile(jnp.arange(GRID_W, dtype=F32), rows)
    inv = ROPE_THETA ** (-jnp.arange(0, AXIS_DIM, 2, dtype=F32) / AXIS_DIM)
    ang = jnp.concatenate([row[:, None] * inv, col[:, None] * inv], axis=-1)
    c, s = jnp.cos(ang), jnp.sin(ang)
    return jnp.concatenate([c, c], axis=-1), jnp.concatenate([-s, s], axis=-1)


def _to_class_major(a, r):
    lead, L, C = a.shape[:-2], a.shape[-2], a.shape[-1]
    return a.reshape(lead + (L // r, r, C)).swapaxes(-3, -2).reshape(a.shape)


def _from_class_major(a, r):
    lead, L, C = a.shape[:-2], a.shape[-2], a.shape[-1]
    return a.reshape(lead + (r, L // r, C)).swapaxes(-3, -2).reshape(a.shape)


def _trunk(x, p):
    B, L, _ = x.shape
    r = L // CLASS_ROWS
    tf, ti = _class_dft_tables(L, r)
    feats = _to_class_major(_filter_features(L), r)
    cos2, sin2 = (_to_class_major(t, r) for t in _rope_tables(L))
    x = _to_class_major(x, r)
    for i in range(DEPTH):
        j = i // 2
        lg = p["ln_g"][i][None]
        lb = p["ln_b"][i][None]
        if i % 2 == 0:
            w1p = jnp.pad(p["hy_f_w1"][j], ((0, FILT_PAD_K - POS_EMB_DIM), (0, 0)))
            hf, hb = _hyena_filters(feats, w1p, p["hy_f_b1"][j][None], p["hy_f_freq"][j][None],
                                    p["hy_f_w2"][j], p["hy_f_b2"][j][None], p["hy_f_w3"][j],
                                    p["hy_decay"][j][None])
            kf = _filter_spectrum(hf, hb, tf, r)
            u = _hyena_proj(x, p["hy_w_in"][j].astype(BF16), p["hy_b_in"][j][None])
            cw = p["hy_conv_w"][j].reshape(3, 3, HY_WIDTH).transpose(1, 0, 2)
            cb = p["hy_conv_b"][j].reshape(3, HY_WIDTH)
            z = _hyena_conv(u, kf, tf, ti, cw, cb, p["hy_filt_bias"][j], r)
            x = _out_proj_ln(z, u, 3 * HY_WIDTH // D_MODEL, x, p["hy_w_out"][j].astype(BF16),
                             p["hy_b_out"][j][None], lg, lb, "hyena_out_ln")
        else:
            w = p["at_w_in"][j]
            bi = p["at_b_in"][j]
            kv0 = ATTN_WIDTH
            g0 = ATTN_WIDTH + 2 * KV_WIDTH
            w = jnp.concatenate([w[:, :kv0], w[:, g0:], w[:, kv0:g0]], axis=1).astype(BF16)
            bi = jnp.concatenate([bi[:kv0], bi[g0:], bi[kv0:g0]])[None]
            u = _attn_proj(x, w, bi, p["at_q_norm"][j][None], p["at_k_norm"][j][None], cos2, sin2)
            o = _attention(u)
            x = _out_proj_ln(o, u, 1, x, p["at_w_out"][j].astype(BF16), p["at_b_out"][j][None], lg, lb,
                             "attn_out_ln")
    return _from_class_major(x, r)


def kernel(x_prompt, x_sample, hy_w_in, hy_b_in, hy_conv_w, hy_conv_b, hy_f_w1, hy_f_b1, hy_f_freq, hy_f_w2,
           hy_f_b2, hy_f_w3, hy_decay, hy_filt_bias, hy_w_out, hy_b_out, at_w_in, at_b_in, at_q_norm, at_k_norm,
           at_w_out, at_b_out, ln_g, ln_b):
    p = dict(hy_w_in=hy_w_in, hy_b_in=hy_b_in, hy_conv_w=hy_conv_w, hy_conv_b=hy_conv_b, hy_f_w1=hy_f_w1,
             hy_f_b1=hy_f_b1, hy_f_freq=hy_f_freq, hy_f_w2=hy_f_w2, hy_f_b2=hy_f_b2, hy_f_w3=hy_f_w3,
             hy_decay=hy_decay, hy_filt_bias=hy_filt_bias, hy_w_out=hy_w_out, hy_b_out=hy_b_out,
             at_w_in=at_w_in, at_b_in=at_b_in, at_q_norm=at_q_norm, at_k_norm=at_k_norm, at_w_out=at_w_out,
             at_b_out=at_b_out, ln_g=ln_g, ln_b=ln_b)
    return (_trunk(x_prompt, p), _trunk(x_sample, p))
```

```python
import cmath
import functools
import math

import jax
import jax.numpy as jnp
from jax import lax
from jax.experimental import pallas as pl
from jax.experimental.pallas import tpu as pltpu

F32 = jnp.float32
BF16 = jnp.bfloat16

D_MODEL = 1024
DEPTH = 4
GRID_W = 64
HY_WIDTH = D_MODEL
HY_ORDER = 2
POS_EMB_DIM = 33
POS_BANDS = (POS_EMB_DIM - 1) // 2
FILTER_WIDTH = 64
FILTER_EPS = 1e-6
HEAD_DIM = 128
N_Q_HEADS = D_MODEL // HEAD_DIM
N_KV_HEADS = 2
GROUP = N_Q_HEADS // N_KV_HEADS
ATTN_WIDTH = N_Q_HEADS * HEAD_DIM
KV_WIDTH = N_KV_HEADS * HEAD_DIM
AXIS_DIM = HEAD_DIM // 2
ROPE_THETA = 10000.0
RMS_EPS = 1e-6
LN_EPS = 1e-5
DEEPNORM_ALPHA = (2.0 * DEPTH) ** 0.25

V7X_LANES = 128
V7X_SUBLANES = 8
V7X_MXU_DIM = 256
V7X_VMEM_BYTES = 64 * 1024 * 1024

CLASS_ROWS = V7X_MXU_DIM
CONV_TC = V7X_MXU_DIM
SPEC_ROWS = 2 * V7X_SUBLANES
ATTN_TQ = 256
PROJ_TM = 256
OUT_TM = 512
HYENA_TM = 512
FILT_PAD_K = 64
BOUND_SLACK = 1.02
MIN_SAFE_DENOM = 2.0 ** -90


def _cparams(n_axes, vmem_mb):
    return pltpu.CompilerParams(
        dimension_semantics=("arbitrary",) * n_axes,
        vmem_limit_bytes=vmem_mb * 1024 * 1024,
    )


def _const_spec(block, index_map):
    return pl.BlockSpec(block, index_map, pipeline_mode=pl.Buffered(1))


def _add(a, b):
    if a is None:
        return b
    if b is None:
        return a
    return a + b


def _sub(a, b):
    if b is None:
        return a
    if a is None:
        return -b
    return a - b


def _scale(a, c):
    return None if a is None else a * c


def _twiddle(x, w):
    re, im = x
    c, s = w.real, w.imag
    if abs(s) < 1e-12:
        return (re, im) if c > 0 else (_sub(None, re), _sub(None, im))
    if abs(c) < 1e-12:
        return (_sub(None, im), re) if s > 0 else (im, _sub(None, re))
    if abs(abs(c) - abs(s)) < 1e-12:
        a = abs(c)
        total = _add(re, im)
        if c > 0 and s > 0:
            return (_scale(_sub(re, im), a), _scale(total, a))
        if c > 0:
            return (_scale(total, a), _scale(_sub(im, re), a))
        if s > 0:
            return (_scale(total, -a), _scale(_sub(re, im), a))
        return (_scale(_sub(im, re), a), _scale(total, -a))
    nre = _sub(_scale(re, c), _scale(im, s))
    nim = _add(_scale(re, s), _scale(im, c))
    return (nre, nim)


def _fft(xs, sign, first_half_only=False):
    n = len(xs)
    if n == 1:
        return list(xs)
    ev = _fft(xs[0::2], sign)
    od = _fft(xs[1::2], sign)
    out = [None] * n
    for k in range(n // 2):
        t = _twiddle(od[k], cmath.exp(sign * 2j * math.pi * k / n))
        out[k] = (_add(ev[k][0], t[0]), _add(ev[k][1], t[1]))
        if not first_half_only:
            out[k + n // 2] = (_sub(ev[k][0], t[0]), _sub(ev[k][1], t[1]))
    return out[: n // 2] if first_half_only else out


def _proj_kernel(x_ref, w_ref, b_ref, o_ref):
    x = x_ref[0].astype(BF16)
    acc = jnp.dot(x, w_ref[...], preferred_element_type=F32) + b_ref[...]
    o_ref[0] = acc.astype(o_ref.dtype)


def _hyena_proj(x, w, b):
    B, L, _ = x.shape
    n_out = w.shape[1]
    tm = HYENA_TM
    return pl.pallas_call(
        _proj_kernel,
        grid=(B, L // tm),
        in_specs=[
            pl.BlockSpec((1, tm, D_MODEL), lambda b, t: (b, t, 0)),
            _const_spec((D_MODEL, n_out), lambda b, t: (0, 0)),
            _const_spec((1, n_out), lambda b, t: (0, 0)),
        ],
        out_specs=pl.BlockSpec((1, tm, n_out), lambda b, t: (b, t, 0)),
        out_shape=jax.ShapeDtypeStruct((B, L, n_out), BF16),
        compiler_params=_cparams(2, 48),
        name="hyena_in_proj",
    )(x, w, b)


def _attn_proj_kernel(x_ref, w_ref, b_ref, qn_ref, kn_ref, cos_ref, sin_ref, o_ref):
    x = x_ref[0].astype(BF16)
    acc = jnp.dot(x, w_ref[...], preferred_element_type=F32) + b_ref[...]
    cos2 = cos_ref[...]
    sin2 = sin_ref[...]

    def head(col, gain, scale):
        xh = acc[:, col:col + HEAD_DIM]
        ms = jnp.mean(xh * xh, axis=-1, keepdims=True)
        xn = xh * lax.rsqrt(ms + RMS_EPS) * gain
        rot = xn * cos2 + pltpu.roll(xn, AXIS_DIM, axis=1) * sin2
        if scale != 1.0:
            rot = rot * scale
        o_ref[0, :, col:col + HEAD_DIM] = rot.astype(o_ref.dtype)

    for h in range(N_Q_HEADS):
        head(h * HEAD_DIM, qn_ref[...], HEAD_DIM ** -0.5 * math.log2(math.e))
    o_ref[0, :, ATTN_WIDTH:2 * ATTN_WIDTH] = acc[:, ATTN_WIDTH:2 * ATTN_WIDTH].astype(o_ref.dtype)
    for h in range(N_KV_HEADS):
        head(2 * ATTN_WIDTH + h * HEAD_DIM, kn_ref[...], 1.0)
    v0 = 2 * ATTN_WIDTH + KV_WIDTH
    o_ref[0, :, v0:v0 + KV_WIDTH] = acc[:, v0:v0 + KV_WIDTH].astype(o_ref.dtype)


def _attn_proj(x, w, b, qn, kn, cos2, sin2):
    B, L, _ = x.shape
    n_out = w.shape[1]
    tm = PROJ_TM
    return pl.pallas_call(
        _attn_proj_kernel,
        grid=(B, L // tm),
        in_specs=[
            pl.BlockSpec((1, tm, D_MODEL), lambda b, t: (b, t, 0)),
            _const_spec((D_MODEL, n_out), lambda b, t: (0, 0)),
            _const_spec((1, n_out), lambda b, t: (0, 0)),
            _const_spec((1, HEAD_DIM), lambda b, t: (0, 0)),
            _const_spec((1, HEAD_DIM), lambda b, t: (0, 0)),
            pl.BlockSpec((tm, HEAD_DIM), lambda b, t: (t, 0)),
            pl.BlockSpec((tm, HEAD_DIM), lambda b, t: (t, 0)),
        ],
        out_specs=pl.BlockSpec((1, tm, n_out), lambda b, t: (b, t, 0)),
        out_shape=jax.ShapeDtypeStruct((B, L, n_out), BF16),
        compiler_params=_cparams(2, 40),
        name="attn_in_proj",
    )(x, w, b, qn, kn, cos2, sin2)


def _gated_out_ln(y, g, x, w_ref, b_ref, lg_ref, lb_ref):
    y = y.astype(F32)
    g = g.astype(F32)
    yg = (y * (g * jax.nn.sigmoid(g))).astype(BF16)
    s = jnp.dot(yg, w_ref[...], preferred_element_type=F32) + b_ref[...]
    v = DEEPNORM_ALPHA * x + s
    mu = jnp.mean(v, axis=-1, keepdims=True)
    vc = v - mu
    var = jnp.mean(vc * vc, axis=-1, keepdims=True)
    return vc * lax.rsqrt(var + LN_EPS) * lg_ref[...] + lb_ref[...]


def _out_kernel(y_ref, g_ref, x_ref, w_ref, b_ref, lg_ref, lb_ref, o_ref):
    o_ref[0] = _gated_out_ln(y_ref[0], g_ref[0], x_ref[0], w_ref, b_ref, lg_ref, lb_ref)


def _out_proj_ln(y, u, gate_block, x, w, b, lg, lb, name):
    B, L, _ = x.shape
    tm = OUT_TM
    x_spec = pl.BlockSpec((1, tm, D_MODEL), lambda b_, t: (b_, t, 0))
    zero = lambda b_, t: (0, 0)
    return pl.pallas_call(
        _out_kernel,
        grid=(B, L // tm),
        in_specs=[
            x_spec,
            pl.BlockSpec((1, tm, D_MODEL), lambda b_, t: (b_, t, gate_block)),
            x_spec,
            _const_spec((D_MODEL, D_MODEL), zero),
            _const_spec((1, D_MODEL), zero),
            _const_spec((1, D_MODEL), zero),
            _const_spec((1, D_MODEL), zero),
        ],
        out_specs=x_spec,
        out_shape=jax.ShapeDtypeStruct(x.shape, F32),
        compiler_params=_cparams(2, 40),
        name=name,
    )(y, u, x, w, b, lg, lb)


def _filter_kernel(feat_ref, w1_ref, b1_ref, fr_ref, w2_ref, b2_ref, w3f_ref, w3b_ref,
                   dcf_ref, dcb_ref, hf_ref, hb_ref, h_s):
    hi = lax.Precision.HIGHEST

    @pl.when((pl.program_id(0) == 0) & (pl.program_id(1) == 0))
    def _():
        fr = fr_ref[...]
        h = jnp.sin(fr * (jnp.dot(feat_ref[...], w1_ref[...], precision=hi, preferred_element_type=F32)
                          + b1_ref[...]))
        h_s[...] = jnp.sin(fr * (jnp.dot(h, w2_ref[...], precision=hi, preferred_element_type=F32)
                                 + b2_ref[...]))

    h = h_s[...]
    t = feat_ref[:, 0:1]
    hf = jnp.dot(h, w3f_ref[...], precision=hi, preferred_element_type=F32) * jnp.exp(-t * jnp.abs(dcf_ref[...]))
    hb = jnp.dot(h, w3b_ref[...], precision=hi, preferred_element_type=F32) * jnp.exp(-t * jnp.abs(dcb_ref[...]))
    ss = jnp.sum(hf * hf + hb * hb, axis=0, keepdims=True)
    sc = lax.rsqrt(ss + FILTER_EPS)
    hf_ref[...] = hf * sc
    row = lax.broadcasted_iota(jnp.int32, hb.shape, 0)
    hb_ref[...] = jnp.where(row == 0, 0.0, hb * sc)


def _hyena_filters(feats, w1p, b1, fr, w2, b2, w3, decay):
    L = feats.shape[0]
    E = HY_WIDTH
    tc = CONV_TC
    nct = E // tc
    zero = lambda o, c: (0, 0)
    col = lambda d: (lambda o, c: (0, (2 * o + d) * nct + c))
    out_spec = pl.BlockSpec((L, tc), lambda o, c: (0, o * nct + c))
    return pl.pallas_call(
        _filter_kernel,
        grid=(HY_ORDER, nct),
        in_specs=[
            _const_spec((L, FILT_PAD_K), zero),
            _const_spec((FILT_PAD_K, FILTER_WIDTH), zero),
            _const_spec((1, FILTER_WIDTH), zero),
            _const_spec((1, FILTER_WIDTH), zero),
            _const_spec((FILTER_WIDTH, FILTER_WIDTH), zero),
            _const_spec((1, FILTER_WIDTH), zero),
            pl.BlockSpec((FILTER_WIDTH, tc), col(0)),
            pl.BlockSpec((FILTER_WIDTH, tc), col(1)),
            pl.BlockSpec((1, tc), col(0)),
            pl.BlockSpec((1, tc), col(1)),
        ],
        out_specs=[out_spec, out_spec],
        out_shape=[jax.ShapeDtypeStruct((L, HY_ORDER * E), F32)] * 2,
        scratch_shapes=[pltpu.VMEM((L, FILTER_WIDTH), F32)],
        compiler_params=_cparams(2, 48),
        name="hyena_filter_taps",
    )(feats, w1p, b1, fr, w2, b2, w3, w3, decay, decay)


def _class_slices(ref, q, F, rows, lanes):
    return (ref[q, rows, lanes], ref[q, pl.ds(rows.start + F, rows.size), lanes])


def _kf_kernel(hf_ref, hb_ref, tf_ref, kf_ref, pa_s, pb_s, *, r, F, tc, inv_len):
    def fwd(src_ref, dst_ref):
        for q in range(r):
            h = src_ref[q * F:(q + 1) * F, :]
            h_hi = h.astype(BF16)
            h_lo = (h - h_hi.astype(F32)).astype(BF16)
            t = tf_ref[q]
            dst_ref[q] = (jnp.dot(t, h_hi, preferred_element_type=F32)
                          + jnp.dot(t, h_lo, preferred_element_type=F32))

    fwd(hf_ref, pa_s)
    fwd(hb_ref, pb_s)

    def body(i, carry):
        rows = pl.ds(pl.multiple_of(i * SPEC_ROWS, SPEC_ROWS), SPEC_ROWS)
        rows_im = pl.ds(rows.start + F, rows.size)
        for lc in range(tc // V7X_LANES):
            lanes = slice(lc * V7X_LANES, (lc + 1) * V7X_LANES)
            xa = _fft([_class_slices(pa_s, q, F, rows, lanes) for q in range(r)], -1.0)
            xb = _fft([_class_slices(pb_s, q, F, rows, lanes) for q in range(r)], -1.0)
            for j in range(r):
                kf_ref[0, j, rows, lanes] = ((xa[j][0] + xb[j][0]) * inv_len).astype(kf_ref.dtype)
                kf_ref[0, j, rows_im, lanes] = ((xa[j][1] - xb[j][1]) * inv_len).astype(kf_ref.dtype)
        return carry

    lax.fori_loop(0, F // SPEC_ROWS, body, 0)


def _filter_spectrum(hf, hb, tf, r):
    L = hf.shape[0]
    F = L // r
    E = HY_WIDTH
    tc = CONV_TC
    nct = E // tc
    kern = functools.partial(_kf_kernel, r=r, F=F, tc=tc, inv_len=1.0 / L)
    return pl.pallas_call(
        kern,
        grid=(HY_ORDER, nct),
        in_specs=[
            pl.BlockSpec((L, tc), lambda o, c: (0, o * nct + c)),
            pl.BlockSpec((L, tc), lambda o, c: (0, o * nct + c)),
            _const_spec((r, 2 * F, F), lambda o, c: (0, 0, 0)),
        ],
        out_specs=pl.BlockSpec((1, r, 2 * F, tc), lambda o, c: (o, 0, 0, c)),
        out_shape=jax.ShapeDtypeStruct((HY_ORDER, r, 2 * F, E), BF16),
        scratch_shapes=[pltpu.VMEM((r, 2 * F, tc), F32), pltpu.VMEM((r, 2 * F, tc), F32)],
        compiler_params=_cparams(2, 56),
        name="hyena_filter_spectrum",
    )(hf, hb, tf)


def _short_conv_classes(u_ref, w, b, r, F):
    def cls(k):
        return u_ref[0, k * F:(k + 1) * F, :].astype(F32)

    first = cls(0)
    row = lax.broadcasted_iota(jnp.int32, first.shape, 0)
    prev = jnp.where(row == 0, 0.0, pltpu.roll(cls(r - 1), 1, axis=0))
    cur = first
    for q in range(r):
        if q < r - 1:
            nxt = cls(q + 1)
        else:
            nxt = jnp.where(row == F - 1, 0.0, pltpu.roll(first, F - 1, axis=0))
        yield q, prev * w[0:1] + cur * w[1:2] + nxt * w[2:3] + b
        prev, cur = cur, nxt


def _conv_kernel(uv_ref, ug0_ref, ug1_ref, kf_ref, tf_ref, ti_ref, cw_ref, cb_ref, fb_ref,
                 out_ref, z_s, p_s, a_s, *, r, F, tc):
    for q, v in _short_conv_classes(uv_ref, cw_ref[0], cb_ref[0:1, :], r, F):
        z_s[q] = v

    gate_refs = (ug0_ref, ug1_ref)
    for o in range(HY_ORDER):
        for q in range(r):
            p_s[q] = jnp.dot(tf_ref[q], z_s[q].astype(BF16), preferred_element_type=F32)

        def body(i, carry, o=o):
            rows = pl.ds(pl.multiple_of(i * SPEC_ROWS, SPEC_ROWS), SPEC_ROWS)
            rows_im = pl.ds(rows.start + F, rows.size)
            for lc in range(tc // V7X_LANES):
                lanes = slice(lc * V7X_LANES, (lc + 1) * V7X_LANES)
                x = _fft([_class_slices(p_s, q, F, rows, lanes) for q in range(r)], -1.0)
                y = []
                for j in range(r):
                    kre = kf_ref[o, j, rows, lanes].astype(F32)
                    kim = kf_ref[o, j, rows_im, lanes].astype(F32)
                    xre, xim = x[j]
                    y.append((xre * kre - xim * kim, xre * kim + xim * kre))
                a = _fft(y, 1.0)
                for q in range(r):
                    a_s[q, rows, lanes] = a[q][0].astype(a_s.dtype)
                    a_s[q, rows_im, lanes] = a[q][1].astype(a_s.dtype)
            return carry

        lax.fori_loop(0, F // SPEC_ROWS, body, 0)

        gates = _short_conv_classes(gate_refs[o], cw_ref[1 + o], cb_ref[1 + o:2 + o, :], r, F)
        for q, g in gates:
            y = jnp.dot(ti_ref[q], a_s[q], preferred_element_type=F32)
            z_new = g * (y + z_s[q] * fb_ref[o:o + 1, :])
            if o == HY_ORDER - 1:
                out_ref[0, q * F:(q + 1) * F, :] = z_new.astype(out_ref.dtype)
            else:
                z_s[q] = z_new


def _hyena_conv(u, kf, tf, ti, cw, cb, fb, r):
    B, L, _ = u.shape
    F = L // r
    E = HY_WIDTH
    tc = CONV_TC
    nct = E // tc
    kern = functools.partial(_conv_kernel, r=r, F=F, tc=tc)
    sec = lambda s: pl.BlockSpec((1, L, tc), lambda c, b: (b, 0, s * nct + c))
    return pl.pallas_call(
        kern,
        grid=(nct, B),
        in_specs=[
            sec(0), sec(1), sec(2),
            _const_spec((HY_ORDER, r, 2 * F, tc), lambda c, b: (0, 0, 0, c)),
            _const_spec((r, 2 * F, F), lambda c, b: (0, 0, 0)),
            _const_spec((r, F, 2 * F), lambda c, b: (0, 0, 0)),
            _const_spec((3, 3, tc), lambda c, b: (0, 0, c)),
            _const_spec((3, tc), lambda c, b: (0, c)),
            _const_spec((HY_ORDER, tc), lambda c, b: (0, c)),
        ],
        out_specs=pl.BlockSpec((1, L, tc), lambda c, b: (b, 0, c)),
        out_shape=jax.ShapeDtypeStruct((B, L, E), BF16),
        scratch_shapes=[pltpu.VMEM((r, F, tc), F32), pltpu.VMEM((r, 2 * F, tc), F32),
                        pltpu.VMEM((r, 2 * F, tc), BF16)],
        compiler_params=_cparams(2, 56),
        name="hyena_long_conv",
    )(u, u, u, kf, tf, ti, cw, cb, fb)


def _attn_kernel(q_ref, k_ref, v_ref, o_ref, vt_s, kmax_s, p_s):
    @pl.when(pl.program_id(2) == 0)
    def _():
        vt_s[...] = v_ref[0].T
        kf = k_ref[0].astype(F32)
        row_sq = jnp.sum(kf * kf, axis=1, keepdims=True)
        kmax_s[...] = jnp.broadcast_to(jnp.max(row_sq, axis=0, keepdims=True), kmax_s.shape)

    k = k_ref[0]
    dims_nt = (((1,), (1,)), ((), ()))

    def q_head(g):
        return q_ref[0, :, g * HEAD_DIM:(g + 1) * HEAD_DIM]

    def finish(g, p_bf16, denom):
        ot = jnp.dot(vt_s[...], p_bf16, preferred_element_type=F32) * (1.0 / denom)
        o_ref[0, :, g * HEAD_DIM:(g + 1) * HEAD_DIM] = ot.T.astype(o_ref.dtype)

    ones = jnp.ones((V7X_SUBLANES, HEAD_DIM), BF16)
    denoms = []
    for g in range(GROUP):
        q = q_head(g)
        qf = q.astype(F32)
        q_sq = lax.dot_general(ones, (qf * qf).astype(BF16), dims_nt, preferred_element_type=F32)
        bound = BOUND_SLACK * jnp.sqrt(q_sq[0:1] * kmax_s[0:1])
        st = lax.dot_general(k, q, dims_nt, preferred_element_type=F32)
        p = jnp.exp2(st - bound)
        denoms.append(jnp.sum(p, axis=0, keepdims=True))
        p_s[g] = p.astype(BF16)
    for g in range(GROUP):
        finish(g, p_s[g], denoms[g])

    smallest = jnp.min(jnp.minimum(jnp.minimum(denoms[0], denoms[1]), jnp.minimum(denoms[2], denoms[3])))

    @pl.when(jnp.logical_not(smallest >= MIN_SAFE_DENOM))
    def _():
        for g in range(GROUP):
            st = lax.dot_general(k, q_head(g), dims_nt, preferred_element_type=F32)
            p = jnp.exp2(st - jnp.max(st, axis=0, keepdims=True))
            finish(g, p.astype(BF16), jnp.sum(p, axis=0, keepdims=True))


def _attention(u):
    B, L, _ = u.shape
    tq = ATTN_TQ
    gw = GROUP * HEAD_DIM
    k_block = 2 * ATTN_WIDTH // HEAD_DIM
    v_block = (2 * ATTN_WIDTH + KV_WIDTH) // HEAD_DIM
    return pl.pallas_call(
        _attn_kernel,
        grid=(B, N_KV_HEADS, L // tq),
        in_specs=[
            pl.BlockSpec((1, tq, gw), lambda b, h, t: (b, t, h)),
            pl.BlockSpec((1, L, HEAD_DIM), lambda b, h, t: (b, 0, k_block + h)),
            pl.BlockSpec((1, L, HEAD_DIM), lambda b, h, t: (b, 0, v_block + h)),
        ],
        out_specs=pl.BlockSpec((1, tq, gw), lambda b, h, t: (b, t, h)),
        out_shape=jax.ShapeDtypeStruct((B, L, ATTN_WIDTH), BF16),
        scratch_shapes=[pltpu.VMEM((HEAD_DIM, L), BF16),
                        pltpu.VMEM((V7X_SUBLANES, tq), F32),
                        pltpu.VMEM((GROUP, L, tq), BF16)],
        compiler_params=_cparams(3, 56),
        name="gqa_attention",
    )(u, u, u)


def _class_dft_tables(L, r):
    F = L // r
    f = jnp.arange(F, dtype=jnp.int32)
    m = jnp.arange(F, dtype=jnp.int32)
    q = jnp.arange(r, dtype=jnp.int32)
    n = ((2 * f + 1)[None, :, None] * (r * m[None, None, :] + q[:, None, None])) % (4 * L)
    theta = n.astype(F32) * (math.pi / (2 * L))
    c = jnp.cos(theta)
    s = jnp.sin(theta)
    tf = jnp.concatenate([c, -s], axis=1).astype(BF16)
    ti = jnp.concatenate([c.transpose(0, 2, 1), -s.transpose(0, 2, 1)], axis=2).astype(BF16)
    return tf, ti


def _filter_features(L):
    t = jnp.linspace(0.0, 1.0, L, dtype=F32)[:, None]
    w = 2.0 * math.pi * jnp.arange(L, dtype=F32)[:, None] / L
    f = jnp.linspace(1e-4, POS_BANDS - 1, POS_BANDS, dtype=F32)[None]
    feats = jnp.concatenate([t, jnp.cos(f * w), -jnp.sin(f * w)], axis=-1)
    return jnp.pad(feats, ((0, 0), (0, FILT_PAD_K - POS_EMB_DIM)))


def _rope_tables(L):
    rows = L // GRID_W
    row = jnp.repeat(jnp.arange(rows, dtype=F32), GRID_W)
    col = jnp.tile(jnp.arange(GRID_W, dtype=F32), rows)
    inv = ROPE_THETA ** (-jnp.arange(0, AXIS_DIM, 2, dtype=F32) / AXIS_DIM)
    ang = jnp.concatenate([row[:, None] * inv, col[:, None] * inv], axis=-1)
    c, s = jnp.cos(ang), jnp.sin(ang)
    return jnp.concatenate([c, c], axis=-1), jnp.concatenate([-s, s], axis=-1)


def _to_class_major(a, r):
    lead, L, C = a.shape[:-2], a.shape[-2], a.shape[-1]
    return a.reshape(lead + (L // r, r, C)).swapaxes(-3, -2).reshape(a.shape)


def _from_class_major(a, r):
    lead, L, C = a.shape[:-2], a.shape[-2], a.shape[-1]
    return a.reshape(lead + (r, L // r, C)).swapaxes(-3, -2).reshape(a.shape)


def _trunk(x, p):
    B, L, _ = x.shape
    r = L // CLASS_ROWS
    tf, ti = _class_dft_tables(L, r)
    feats = _to_class_major(_filter_features(L), r)
    cos2, sin2 = (_to_class_major(t, r) for t in _rope_tables(L))
    x = _to_class_major(x, r)
    for i in range(DEPTH):
        j = i // 2
        lg = p["ln_g"][i][None]
        lb = p["ln_b"][i][None]
        if i % 2 == 0:
            w1p = jnp.pad(p["hy_f_w1"][j], ((0, FILT_PAD_K - POS_EMB_DIM), (0, 0)))
            hf, hb = _hyena_filters(feats, w1p, p["hy_f_b1"][j][None], p["hy_f_freq"][j][None],
                                    p["hy_f_w2"][j], p["hy_f_b2"][j][None], p["hy_f_w3"][j],
                                    p["hy_decay"][j][None])
            kf = _filter_spectrum(hf, hb, tf, r)
            u = _hyena_proj(x, p["hy_w_in"][j].astype(BF16), p["hy_b_in"][j][None])
            cw = p["hy_conv_w"][j].reshape(3, 3, HY_WIDTH).transpose(1, 0, 2)
            cb = p["hy_conv_b"][j].reshape(3, HY_WIDTH)
            z = _hyena_conv(u, kf, tf, ti, cw, cb, p["hy_filt_bias"][j], r)
            x = _out_proj_ln(z, u, 3 * HY_WIDTH // D_MODEL, x, p["hy_w_out"][j].astype(BF16),
                             p["hy_b_out"][j][None], lg, lb, "hyena_out_ln")
        else:
            w = p["at_w_in"][j]
            bi = p["at_b_in"][j]
            kv0 = ATTN_WIDTH
            g0 = ATTN_WIDTH + 2 * KV_WIDTH
            w = jnp.concatenate([w[:, :kv0], w[:, g0:], w[:, kv0:g0]], axis=1).astype(BF16)
            bi = jnp.concatenate([bi[:kv0], bi[g0:], bi[kv0:g0]])[None]
            u = _attn_proj(x, w, bi, p["at_q_norm"][j][None], p["at_k_norm"][j][None], cos2, sin2)
            o = _attention(u)
            x = _out_proj_ln(o, u, 1, x, p["at_w_out"][j].astype(BF16), p["at_b_out"][j][None], lg, lb,
                             "attn_out_ln")
    return _from_class_major(x, r)


def kernel(x_prompt, x_sample, hy_w_in, hy_b_in, hy_conv_w, hy_conv_b, hy_f_w1, hy_f_b1, hy_f_freq, hy_f_w2,
           hy_f_b2, hy_f_w3, hy_decay, hy_filt_bias, hy_w_out, hy_b_out, at_w_in, at_b_in, at_q_norm, at_k_norm,
           at_w_out, at_b_out, ln_g, ln_b):
    p = dict(hy_w_in=hy_w_in, hy_b_in=hy_b_in, hy_conv_w=hy_conv_w, hy_conv_b=hy_conv_b, hy_f_w1=hy_f_w1,
             hy_f_b1=hy_f_b1, hy_f_freq=hy_f_freq, hy_f_w2=hy_f_w2, hy_f_b2=hy_f_b2, hy_f_w3=hy_f_w3,
             hy_decay=hy_decay, hy_filt_bias=hy_filt_bias, hy_w_out=hy_w_out, hy_b_out=hy_b_out,
             at_w_in=at_w_in, at_b_in=at_b_in, at_q_norm=at_q_norm, at_k_norm=at_k_norm, at_w_out=at_w_out,
             at_b_out=at_b_out, ln_g=ln_g, ln_b=ln_b)
    return (_trunk(x_prompt, p), _trunk(x_sample, p))
```

```python
import cmath
import functools
import math

import jax
import jax.numpy as jnp
from jax import lax
from jax.experimental import pallas as pl
from jax.experimental.pallas import tpu as pltpu

F32 = jnp.float32
BF16 = jnp.bfloat16

D_MODEL = 1024
DEPTH = 4
GRID_W = 64
HY_WIDTH = D_MODEL
HY_ORDER = 2
POS_EMB_DIM = 33
POS_BANDS = (POS_EMB_DIM - 1) // 2
FILTER_WIDTH = 64
FILTER_EPS = 1e-6
HEAD_DIM = 128
N_Q_HEADS = D_MODEL // HEAD_DIM
N_KV_HEADS = 2
GROUP = N_Q_HEADS // N_KV_HEADS
ATTN_WIDTH = N_Q_HEADS * HEAD_DIM
KV_WIDTH = N_KV_HEADS * HEAD_DIM
AXIS_DIM = HEAD_DIM // 2
ROPE_THETA = 10000.0
RMS_EPS = 1e-6
LN_EPS = 1e-5
DEEPNORM_ALPHA = (2.0 * DEPTH) ** 0.25

V7X_LANES = 128
V7X_SUBLANES = 8
V7X_MXU_DIM = 256
V7X_VMEM_BYTES = 64 * 1024 * 1024

CLASS_ROWS = V7X_MXU_DIM
CONV_TC = V7X_MXU_DIM
SPEC_ROWS = 2 * V7X_SUBLANES
ATTN_TQ = 256
OUT_TM = 512
BOUNDARY_SUB_ROWS = 256
HYENA_TM = 512
FILT_PAD_K = 64
BOUND_SLACK = 1.02
MIN_SAFE_DENOM = 2.0 ** -90


def _cparams(n_axes, vmem_mb):
    return pltpu.CompilerParams(
        dimension_semantics=("arbitrary",) * n_axes,
        vmem_limit_bytes=vmem_mb * 1024 * 1024,
    )


def _const_spec(block, index_map):
    return pl.BlockSpec(block, index_map, pipeline_mode=pl.Buffered(1))


def _add(a, b):
    if a is None:
        return b
    if b is None:
        return a
    return a + b


def _sub(a, b):
    if b is None:
        return a
    if a is None:
        return -b
    return a - b


def _scale(a, c):
    return None if a is None else a * c


def _twiddle(x, w):
    re, im = x
    c, s = w.real, w.imag
    if abs(s) < 1e-12:
        return (re, im) if c > 0 else (_sub(None, re), _sub(None, im))
    if abs(c) < 1e-12:
        return (_sub(None, im), re) if s > 0 else (im, _sub(None, re))
    if abs(abs(c) - abs(s)) < 1e-12:
        a = abs(c)
        total = _add(re, im)
        if c > 0 and s > 0:
            return (_scale(_sub(re, im), a), _scale(total, a))
        if c > 0:
            return (_scale(total, a), _scale(_sub(im, re), a))
        if s > 0:
            return (_scale(total, -a), _scale(_sub(re, im), a))
        return (_scale(_sub(im, re), a), _scale(total, -a))
    nre = _sub(_scale(re, c), _scale(im, s))
    nim = _add(_scale(re, s), _scale(im, c))
    return (nre, nim)


def _fft(xs, sign, first_half_only=False):
    n = len(xs)
    if n == 1:
        return list(xs)
    ev = _fft(xs[0::2], sign)
    od = _fft(xs[1::2], sign)
    out = [None] * n
    for k in range(n // 2):
        t = _twiddle(od[k], cmath.exp(sign * 2j * math.pi * k / n))
        out[k] = (_add(ev[k][0], t[0]), _add(ev[k][1], t[1]))
        if not first_half_only:
            out[k + n // 2] = (_sub(ev[k][0], t[0]), _sub(ev[k][1], t[1]))
    return out[: n // 2] if first_half_only else out


def _proj_kernel(x_ref, w_ref, b_ref, o_ref):
    x = x_ref[0].astype(BF16)
    acc = jnp.dot(x, w_ref[...], preferred_element_type=F32) + b_ref[...]
    o_ref[0] = acc.astype(o_ref.dtype)


def _hyena_proj(x, w, b):
    B, L, _ = x.shape
    n_out = w.shape[1]
    tm = HYENA_TM
    return pl.pallas_call(
        _proj_kernel,
        grid=(B, L // tm),
        in_specs=[
            pl.BlockSpec((1, tm, D_MODEL), lambda b, t: (b, t, 0)),
            _const_spec((D_MODEL, n_out), lambda b, t: (0, 0)),
            _const_spec((1, n_out), lambda b, t: (0, 0)),
        ],
        out_specs=pl.BlockSpec((1, tm, n_out), lambda b, t: (b, t, 0)),
        out_shape=jax.ShapeDtypeStruct((B, L, n_out), BF16),
        compiler_params=_cparams(2, 48),
        name="hyena_in_proj",
    )(x, w, b)


def _attn_in_epilogue(acc, q_gain, k_gain, cos2, sin2, store):
    def head(col, gain, scale):
        xh = acc[:, col:col + HEAD_DIM]
        ms = jnp.mean(xh * xh, axis=-1, keepdims=True)
        xn = xh * lax.rsqrt(ms + RMS_EPS) * gain
        rot = xn * cos2 + pltpu.roll(xn, AXIS_DIM, axis=1) * sin2
        store(col, rot * scale if scale != 1.0 else rot)

    for h in range(N_Q_HEADS):
        head(h * HEAD_DIM, q_gain, HEAD_DIM ** -0.5 * math.log2(math.e))
    store(ATTN_WIDTH, acc[:, ATTN_WIDTH:2 * ATTN_WIDTH])
    for h in range(N_KV_HEADS):
        head(2 * ATTN_WIDTH + h * HEAD_DIM, k_gain, 1.0)
    v0 = 2 * ATTN_WIDTH + KV_WIDTH
    store(v0, acc[:, v0:v0 + KV_WIDTH])


def _gated_out_ln(y, g, x, w_ref, b_ref, lg_ref, lb_ref):
    y = y.astype(F32)
    g = g.astype(F32)
    yg = (y * (g * jax.nn.sigmoid(g))).astype(BF16)
    s = jnp.dot(yg, w_ref[...], preferred_element_type=F32) + b_ref[...]
    v = DEEPNORM_ALPHA * x + s
    mu = jnp.mean(v, axis=-1, keepdims=True)
    vc = v - mu
    var = jnp.mean(vc * vc, axis=-1, keepdims=True)
    return vc * lax.rsqrt(var + LN_EPS) * lg_ref[...] + lb_ref[...]


def _boundary_kernel(*refs, attn_next, sub_rows):
    y_ref, g_ref, x_ref, wo_ref, bo_ref, lg_ref, lb_ref, wi_ref, bi_ref = refs[:9]
    if attn_next:
        qn_ref, kn_ref, cos_ref, sin_ref, xo_ref, u_ref = refs[9:]
    else:
        xo_ref, u_ref = refs[9:]
    sub_tiles = [slice(r0, r0 + sub_rows) for r0 in range(0, x_ref.shape[1], sub_rows)]
    normed = []
    for rows in sub_tiles:
        xn = _gated_out_ln(y_ref[0, rows, :], g_ref[0, rows, :], x_ref[0, rows, :],
                           wo_ref, bo_ref, lg_ref, lb_ref)
        xo_ref[0, rows, :] = xn
        normed.append(xn.astype(BF16))
    for rows, xb in zip(sub_tiles, normed):
        acc = jnp.dot(xb, wi_ref[...], preferred_element_type=F32) + bi_ref[...]
        if attn_next:
            def store(col, val, rows=rows):
                u_ref[0, rows, col:col + val.shape[1]] = val.astype(u_ref.dtype)

            _attn_in_epilogue(acc, qn_ref[...], kn_ref[...], cos_ref[rows, :], sin_ref[rows, :], store)
        else:
            u_ref[0, rows, :] = acc.astype(u_ref.dtype)


def _layer_boundary(y, u, gate_block, x, out_params, in_params, rope, name):
    B, L, _ = x.shape
    tm = OUT_TM
    attn_next = len(in_params) == 4
    n_next = in_params[0].shape[1]
    x_spec = pl.BlockSpec((1, tm, D_MODEL), lambda b_, t: (b_, t, 0))
    zero = lambda b_, t: (0, 0)
    in_specs = [
        x_spec,
        pl.BlockSpec((1, tm, D_MODEL), lambda b_, t: (b_, t, gate_block)),
        x_spec,
        _const_spec((D_MODEL, D_MODEL), zero),
        _const_spec((1, D_MODEL), zero),
        _const_spec((1, D_MODEL), zero),
        _const_spec((1, D_MODEL), zero),
        _const_spec((D_MODEL, n_next), zero),
        _const_spec((1, n_next), zero),
    ]
    args = [y, u, x, *out_params, *in_params]
    if attn_next:
        in_specs += [
            _const_spec((1, HEAD_DIM), zero),
            _const_spec((1, HEAD_DIM), zero),
            pl.BlockSpec((tm, HEAD_DIM), lambda b_, t: (t, 0)),
            pl.BlockSpec((tm, HEAD_DIM), lambda b_, t: (t, 0)),
        ]
        args += list(rope)
    return pl.pallas_call(
        functools.partial(_boundary_kernel, attn_next=attn_next, sub_rows=BOUNDARY_SUB_ROWS),
        grid=(B, L // tm),
        in_specs=in_specs,
        out_specs=[x_spec, pl.BlockSpec((1, tm, n_next), lambda b_, t: (b_, t, 0))],
        out_shape=[jax.ShapeDtypeStruct(x.shape, F32), jax.ShapeDtypeStruct((B, L, n_next), BF16)],
        compiler_params=_cparams(2, 56),
        name=name,
    )(*args)


def _out_kernel(y_ref, g_ref, x_ref, w_ref, b_ref, lg_ref, lb_ref, o_ref):
    o_ref[0] = _gated_out_ln(y_ref[0], g_ref[0], x_ref[0], w_ref, b_ref, lg_ref, lb_ref)


def _out_proj_ln(y, u, gate_block, x, w, b, lg, lb, name):
    B, L, _ = x.shape
    tm = OUT_TM
    x_spec = pl.BlockSpec((1, tm, D_MODEL), lambda b_, t: (b_, t, 0))
    zero = lambda b_, t: (0, 0)
    return pl.pallas_call(
        _out_kernel,
        grid=(B, L // tm),
        in_specs=[
            x_spec,
            pl.BlockSpec((1, tm, D_MODEL), lambda b_, t: (b_, t, gate_block)),
            x_spec,
            _const_spec((D_MODEL, D_MODEL), zero),
            _const_spec((1, D_MODEL), zero),
            _const_spec((1, D_MODEL), zero),
            _const_spec((1, D_MODEL), zero),
        ],
        out_specs=x_spec,
        out_shape=jax.ShapeDtypeStruct(x.shape, F32),
        compiler_params=_cparams(2, 40),
        name=name,
    )(y, u, x, w, b, lg, lb)


def _filter_kernel(feat_ref, w1_ref, b1_ref, fr_ref, w2_ref, b2_ref, w3f_ref, w3b_ref,
                   dcf_ref, dcb_ref, hf_ref, hb_ref, h_s):
    hi = lax.Precision.HIGHEST

    @pl.when((pl.program_id(0) == 0) & (pl.program_id(1) == 0))
    def _():
        fr = fr_ref[...]
        h = jnp.sin(fr * (jnp.dot(feat_ref[...], w1_ref[...], precision=hi, preferred_element_type=F32)
                          + b1_ref[...]))
        h_s[...] = jnp.sin(fr * (jnp.dot(h, w2_ref[...], precision=hi, preferred_element_type=F32)
                                 + b2_ref[...]))

    h = h_s[...]
    t = feat_ref[:, 0:1]
    hf = jnp.dot(h, w3f_ref[...], precision=hi, preferred_element_type=F32) * jnp.exp(-t * jnp.abs(dcf_ref[...]))
    hb = jnp.dot(h, w3b_ref[...], precision=hi, preferred_element_type=F32) * jnp.exp(-t * jnp.abs(dcb_ref[...]))
    ss = jnp.sum(hf * hf + hb * hb, axis=0, keepdims=True)
    sc = lax.rsqrt(ss + FILTER_EPS)
    hf_ref[...] = hf * sc
    row = lax.broadcasted_iota(jnp.int32, hb.shape, 0)
    hb_ref[...] = jnp.where(row == 0, 0.0, hb * sc)


def _hyena_filters(feats, w1p, b1, fr, w2, b2, w3, decay):
    L = feats.shape[0]
    E = HY_WIDTH
    tc = CONV_TC
    nct = E // tc
    zero = lambda o, c: (0, 0)
    col = lambda d: (lambda o, c: (0, (2 * o + d) * nct + c))
    out_spec = pl.BlockSpec((L, tc), lambda o, c: (0, o * nct + c))
    return pl.pallas_call(
        _filter_kernel,
        grid=(HY_ORDER, nct),
        in_specs=[
            _const_spec((L, FILT_PAD_K), zero),
            _const_spec((FILT_PAD_K, FILTER_WIDTH), zero),
            _const_spec((1, FILTER_WIDTH), zero),
            _const_spec((1, FILTER_WIDTH), zero),
            _const_spec((FILTER_WIDTH, FILTER_WIDTH), zero),
            _const_spec((1, FILTER_WIDTH), zero),
            pl.BlockSpec((FILTER_WIDTH, tc), col(0)),
            pl.BlockSpec((FILTER_WIDTH, tc), col(1)),
            pl.BlockSpec((1, tc), col(0)),
            pl.BlockSpec((1, tc), col(1)),
        ],
        out_specs=[out_spec, out_spec],
        out_shape=[jax.ShapeDtypeStruct((L, HY_ORDER * E), F32)] * 2,
        scratch_shapes=[pltpu.VMEM((L, FILTER_WIDTH), F32)],
        compiler_params=_cparams(2, 48),
        name="hyena_filter_taps",
    )(feats, w1p, b1, fr, w2, b2, w3, w3, decay, decay)


def _class_slices(ref, q, F, rows, lanes):
    return (ref[q, rows, lanes], ref[q, pl.ds(rows.start + F, rows.size), lanes])


def _kf_kernel(hf_ref, hb_ref, tf_ref, kf_ref, pa_s, pb_s, *, r, F, tc, inv_len):
    def fwd(src_ref, dst_ref):
        for q in range(r):
            h = src_ref[q * F:(q + 1) * F, :]
            h_hi = h.astype(BF16)
            h_lo = (h - h_hi.astype(F32)).astype(BF16)
            t = tf_ref[q]
            dst_ref[q] = (jnp.dot(t, h_hi, preferred_element_type=F32)
                          + jnp.dot(t, h_lo, preferred_element_type=F32))

    fwd(hf_ref, pa_s)
    fwd(hb_ref, pb_s)

    def body(i, carry):
        rows = pl.ds(pl.multiple_of(i * SPEC_ROWS, SPEC_ROWS), SPEC_ROWS)
        rows_im = pl.ds(rows.start + F, rows.size)
        for lc in range(tc // V7X_LANES):
            lanes = slice(lc * V7X_LANES, (lc + 1) * V7X_LANES)
            xa = _fft([_class_slices(pa_s, q, F, rows, lanes) for q in range(r)], -1.0)
            xb = _fft([_class_slices(pb_s, q, F, rows, lanes) for q in range(r)], -1.0)
            for j in range(r):
                kf_ref[0, j, rows, lanes] = ((xa[j][0] + xb[j][0]) * inv_len).astype(kf_ref.dtype)
                kf_ref[0, j, rows_im, lanes] = ((xa[j][1] - xb[j][1]) * inv_len).astype(kf_ref.dtype)
        return carry

    lax.fori_loop(0, F // SPEC_ROWS, body, 0)


def _filter_spectrum(hf, hb, tf, r):
    L = hf.shape[0]
    F = L // r
    E = HY_WIDTH
    tc = CONV_TC
    nct = E // tc
    kern = functools.partial(_kf_kernel, r=r, F=F, tc=tc, inv_len=1.0 / L)
    return pl.pallas_call(
        kern,
        grid=(HY_ORDER, nct),
        in_specs=[
            pl.BlockSpec((L, tc), lambda o, c: (0, o * nct + c)),
            pl.BlockSpec((L, tc), lambda o, c: (0, o * nct + c)),
            _const_spec((r, 2 * F, F), lambda o, c: (0, 0, 0)),
        ],
        out_specs=pl.BlockSpec((1, r, 2 * F, tc), lambda o, c: (o, 0, 0, c)),
        out_shape=jax.ShapeDtypeStruct((HY_ORDER, r, 2 * F, E), BF16),
        scratch_shapes=[pltpu.VMEM((r, 2 * F, tc), F32), pltpu.VMEM((r, 2 * F, tc), F32)],
        compiler_params=_cparams(2, 56),
        name="hyena_filter_spectrum",
    )(hf, hb, tf)


def _short_conv_classes(u_ref, w, b, r, F):
    def cls(k):
        return u_ref[0, k * F:(k + 1) * F, :].astype(F32)

    first = cls(0)
    row = lax.broadcasted_iota(jnp.int32, first.shape, 0)
    prev = jnp.where(row == 0, 0.0, pltpu.roll(cls(r - 1), 1, axis=0))
    cur = first
    for q in range(r):
        if q < r - 1:
            nxt = cls(q + 1)
        else:
            nxt = jnp.where(row == F - 1, 0.0, pltpu.roll(first, F - 1, axis=0))
        yield q, prev * w[0:1] + cur * w[1:2] + nxt * w[2:3] + b
        prev, cur = cur, nxt


def _conv_kernel(uv_ref, ug0_ref, ug1_ref, kf_ref, tf_ref, ti_ref, cw_ref, cb_ref, fb_ref,
                 out_ref, z_s, p_s, a_s, *, r, F, tc):
    for q, v in _short_conv_classes(uv_ref, cw_ref[0], cb_ref[0:1, :], r, F):
        z_s[q] = v

    gate_refs = (ug0_ref, ug1_ref)
    for o in range(HY_ORDER):
        for q in range(r):
            p_s[q] = jnp.dot(tf_ref[q], z_s[q].astype(BF16), preferred_element_type=F32)

        def body(i, carry, o=o):
            rows = pl.ds(pl.multiple_of(i * SPEC_ROWS, SPEC_ROWS), SPEC_ROWS)
            rows_im = pl.ds(rows.start + F, rows.size)
            for lc in range(tc // V7X_LANES):
                lanes = slice(lc * V7X_LANES, (lc + 1) * V7X_LANES)
                x = _fft([_class_slices(p_s, q, F, rows, lanes) for q in range(r)], -1.0)
                y = []
                for j in range(r):
                    kre = kf_ref[o, j, rows, lanes].astype(F32)
                    kim = kf_ref[o, j, rows_im, lanes].astype(F32)
                    xre, xim = x[j]
                    y.append((xre * kre - xim * kim, xre * kim + xim * kre))
                a = _fft(y, 1.0)
                for q in range(r):
                    a_s[q, rows, lanes] = a[q][0].astype(a_s.dtype)
                    a_s[q, rows_im, lanes] = a[q][1].astype(a_s.dtype)
            return carry

        lax.fori_loop(0, F // SPEC_ROWS, body, 0)

        gates = _short_conv_classes(gate_refs[o], cw_ref[1 + o], cb_ref[1 + o:2 + o, :], r, F)
        for q, g in gates:
            y = jnp.dot(ti_ref[q], a_s[q], preferred_element_type=F32)
            z_new = g * (y + z_s[q] * fb_ref[o:o + 1, :])
            if o == HY_ORDER - 1:
                out_ref[0, q * F:(q + 1) * F, :] = z_new.astype(out_ref.dtype)
            else:
                z_s[q] = z_new


def _hyena_conv(u, kf, tf, ti, cw, cb, fb, r):
    B, L, _ = u.shape
    F = L // r
    E = HY_WIDTH
    tc = CONV_TC
    nct = E // tc
    kern = functools.partial(_conv_kernel, r=r, F=F, tc=tc)
    sec = lambda s: pl.BlockSpec((1, L, tc), lambda c, b: (b, 0, s * nct + c))
    return pl.pallas_call(
        kern,
        grid=(nct, B),
        in_specs=[
            sec(0), sec(1), sec(2),
            _const_spec((HY_ORDER, r, 2 * F, tc), lambda c, b: (0, 0, 0, c)),
            _const_spec((r, 2 * F, F), lambda c, b: (0, 0, 0)),
            _const_spec((r, F, 2 * F), lambda c, b: (0, 0, 0)),
            _const_spec((3, 3, tc), lambda c, b: (0, 0, c)),
            _const_spec((3, tc), lambda c, b: (0, c)),
            _const_spec((HY_ORDER, tc), lambda c, b: (0, c)),
        ],
        out_specs=pl.BlockSpec((1, L, tc), lambda c, b: (b, 0, c)),
        out_shape=jax.ShapeDtypeStruct((B, L, E), BF16),
        scratch_shapes=[pltpu.VMEM((r, F, tc), F32), pltpu.VMEM((r, 2 * F, tc), F32),
                        pltpu.VMEM((r, 2 * F, tc), BF16)],
        compiler_params=_cparams(2, 56),
        name="hyena_long_conv",
    )(u, u, u, kf, tf, ti, cw, cb, fb)


def _attn_kernel(q_ref, k_ref, v_ref, o_ref, vt_s, kmax_s, p_s):
    @pl.when(pl.program_id(2) == 0)
    def _():
        vt_s[...] = v_ref[0].T
        kf = k_ref[0].astype(F32)
        row_sq = jnp.sum(kf * kf, axis=1, keepdims=True)
        kmax_s[...] = jnp.broadcast_to(jnp.max(row_sq, axis=0, keepdims=True), kmax_s.shape)

    k = k_ref[0]
    dims_nt = (((1,), (1,)), ((), ()))

    def q_head(g):
        return q_ref[0, :, g * HEAD_DIM:(g + 1) * HEAD_DIM]

    def finish(g, p_bf16, denom):
        ot = jnp.dot(vt_s[...], p_bf16, preferred_element_type=F32) * (1.0 / denom)
        o_ref[0, :, g * HEAD_DIM:(g + 1) * HEAD_DIM] = ot.T.astype(o_ref.dtype)

    ones = jnp.ones((V7X_SUBLANES, HEAD_DIM), BF16)
    denoms = []
    for g in range(GROUP):
        q = q_head(g)
        qf = q.astype(F32)
        q_sq = lax.dot_general(ones, (qf * qf).astype(BF16), dims_nt, preferred_element_type=F32)
        bound = BOUND_SLACK * jnp.sqrt(q_sq[0:1] * kmax_s[0:1])
        st = lax.dot_general(k, q, dims_nt, preferred_element_type=F32)
        p = jnp.exp2(st - bound)
        denoms.append(jnp.sum(p, axis=0, keepdims=True))
        p_s[g] = p.astype(BF16)
    for g in range(GROUP):
        finish(g, p_s[g], denoms[g])

    smallest = jnp.min(jnp.minimum(jnp.minimum(denoms[0], denoms[1]), jnp.minimum(denoms[2], denoms[3])))

    @pl.when(jnp.logical_not(smallest >= MIN_SAFE_DENOM))
    def _():
        for g in range(GROUP):
            st = lax.dot_general(k, q_head(g), dims_nt, preferred_element_type=F32)
            p = jnp.exp2(st - jnp.max(st, axis=0, keepdims=True))
            finish(g, p.astype(BF16), jnp.sum(p, axis=0, keepdims=True))


def _attention(u):
    B, L, _ = u.shape
    tq = ATTN_TQ
    gw = GROUP * HEAD_DIM
    k_block = 2 * ATTN_WIDTH // HEAD_DIM
    v_block = (2 * ATTN_WIDTH + KV_WIDTH) // HEAD_DIM
    return pl.pallas_call(
        _attn_kernel,
        grid=(B, N_KV_HEADS, L // tq),
        in_specs=[
            pl.BlockSpec((1, tq, gw), lambda b, h, t: (b, t, h)),
            pl.BlockSpec((1, L, HEAD_DIM), lambda b, h, t: (b, 0, k_block + h)),
            pl.BlockSpec((1, L, HEAD_DIM), lambda b, h, t: (b, 0, v_block + h)),
        ],
        out_specs=pl.BlockSpec((1, tq, gw), lambda b, h, t: (b, t, h)),
        out_shape=jax.ShapeDtypeStruct((B, L, ATTN_WIDTH), BF16),
        scratch_shapes=[pltpu.VMEM((HEAD_DIM, L), BF16),
                        pltpu.VMEM((V7X_SUBLANES, tq), F32),
                        pltpu.VMEM((GROUP, L, tq), BF16)],
        compiler_params=_cparams(3, 56),
        name="gqa_attention",
    )(u, u, u)


def _class_dft_tables(L, r):
    F = L // r
    f = jnp.arange(F, dtype=jnp.int32)
    m = jnp.arange(F, dtype=jnp.int32)
    q = jnp.arange(r, dtype=jnp.int32)
    n = ((2 * f + 1)[None, :, None] * (r * m[None, None, :] + q[:, None, None])) % (4 * L)
    theta = n.astype(F32) * (math.pi / (2 * L))
    c = jnp.cos(theta)
    s = jnp.sin(theta)
    tf = jnp.concatenate([c, -s], axis=1).astype(BF16)
    ti = jnp.concatenate([c.transpose(0, 2, 1), -s.transpose(0, 2, 1)], axis=2).astype(BF16)
    return tf, ti


def _filter_features(L):
    t = jnp.linspace(0.0, 1.0, L, dtype=F32)[:, None]
    w = 2.0 * math.pi * jnp.arange(L, dtype=F32)[:, None] / L
    f = jnp.linspace(1e-4, POS_BANDS - 1, POS_BANDS, dtype=F32)[None]
    feats = jnp.concatenate([t, jnp.cos(f * w), -jnp.sin(f * w)], axis=-1)
    return jnp.pad(feats, ((0, 0), (0, FILT_PAD_K - POS_EMB_DIM)))


def _rope_tables(L):
    rows = L // GRID_W
    row = jnp.repeat(jnp.arange(rows, dtype=F32), GRID_W)
    col = jnp.tile(jnp.arange(GRID_W, dtype=F32), rows)
    inv = ROPE_THETA ** (-jnp.arange(0, AXIS_DIM, 2, dtype=F32) / AXIS_DIM)
    ang = jnp.concatenate([row[:, None] * inv, col[:, None] * inv], axis=-1)
    c, s = jnp.cos(ang), jnp.sin(ang)
    return jnp.concatenate([c, c], axis=-1), jnp.concatenate([-s, s], axis=-1)


def _to_class_major(a, r):
    lead, L, C = a.shape[:-2], a.shape[-2], a.shape[-1]
    return a.reshape(lead + (L // r, r, C)).swapaxes(-3, -2).reshape(a.shape)


def _from_class_major(a, r):
    lead, L, C = a.shape[:-2], a.shape[-2], a.shape[-1]
    return a.reshape(lead + (r, L // r, C)).swapaxes(-3, -2).reshape(a.shape)


def _attn_in_params(p, j):
    w = p["at_w_in"][j]
    bi = p["at_b_in"][j]
    kv0 = ATTN_WIDTH
    g0 = ATTN_WIDTH + 2 * KV_WIDTH
    w = jnp.concatenate([w[:, :kv0], w[:, g0:], w[:, kv0:g0]], axis=1).astype(BF16)
    bi = jnp.concatenate([bi[:kv0], bi[g0:], bi[kv0:g0]])[None]
    return (w, bi, p["at_q_norm"][j][None], p["at_k_norm"][j][None])


def _trunk(x, p):
    B, L, _ = x.shape
    r = L // CLASS_ROWS
    tf, ti = _class_dft_tables(L, r)
    feats = _to_class_major(_filter_features(L), r)
    rope = tuple(_to_class_major(t, r) for t in _rope_tables(L))
    x = _to_class_major(x, r)

    def in_params(i):
        j = i // 2
        if i % 2 == 0:
            return (p["hy_w_in"][j].astype(BF16), p["hy_b_in"][j][None])
        return _attn_in_params(p, j)

    def out_params(i):
        j = i // 2
        w, b = (p["hy_w_out"], p["hy_b_out"]) if i % 2 == 0 else (p["at_w_out"], p["at_b_out"])
        return (w[j].astype(BF16), b[j][None], p["ln_g"][i][None], p["ln_b"][i][None])

    u = _hyena_proj(x, *in_params(0))
    for i in range(DEPTH):
        j = i // 2
        if i % 2 == 0:
            w1p = jnp.pad(p["hy_f_w1"][j], ((0, FILT_PAD_K - POS_EMB_DIM), (0, 0)))
            hf, hb = _hyena_filters(feats, w1p, p["hy_f_b1"][j][None], p["hy_f_freq"][j][None],
                                    p["hy_f_w2"][j], p["hy_f_b2"][j][None], p["hy_f_w3"][j],
                                    p["hy_decay"][j][None])
            kf = _filter_spectrum(hf, hb, tf, r)
            cw = p["hy_conv_w"][j].reshape(3, 3, HY_WIDTH).transpose(1, 0, 2)
            cb = p["hy_conv_b"][j].reshape(3, HY_WIDTH)
            y = _hyena_conv(u, kf, tf, ti, cw, cb, p["hy_filt_bias"][j], r)
            gate_block = 3 * HY_WIDTH // D_MODEL
            name = "hyena_out"
        else:
            y = _attention(u)
            gate_block = 1
            name = "attn_out"
        if i + 1 < DEPTH:
            x, u = _layer_boundary(y, u, gate_block, x, out_params(i), in_params(i + 1), rope,
                                   name + "_next_in")
        else:
            x = _out_proj_ln(y, u, gate_block, x, *out_params(i), name + "_ln")
    return _from_class_major(x, r)


def kernel(x_prompt, x_sample, hy_w_in, hy_b_in, hy_conv_w, hy_conv_b, hy_f_w1, hy_f_b1, hy_f_freq, hy_f_w2,
           hy_f_b2, hy_f_w3, hy_decay, hy_filt_bias, hy_w_out, hy_b_out, at_w_in, at_b_in, at_q_norm, at_k_norm,
           at_w_out, at_b_out, ln_g, ln_b):
    p = dict(hy_w_in=hy_w_in, hy_b_in=hy_b_in, hy_conv_w=hy_conv_w, hy_conv_b=hy_conv_b, hy_f_w1=hy_f_w1,
             hy_f_b1=hy_f_b1, hy_f_freq=hy_f_freq, hy_f_w2=hy_f_w2, hy_f_b2=hy_f_b2, hy_f_w3=hy_f_w3,
             hy_decay=hy_decay, hy_filt_bias=hy_filt_bias, hy_w_out=hy_w_out, hy_b_out=hy_b_out,
             at_w_in=at_w_in, at_b_in=at_b_in, at_q_norm=at_q_norm, at_k_norm=at_k_norm, at_w_out=at_w_out,
             at_b_out=at_b_out, ln_g=ln_g, ln_b=ln_b)
    return (_trunk(x_prompt, p), _trunk(x_sample, p))
```

```python
import cmath
import functools
import math

import jax
import jax.numpy as jnp
from jax import lax
from jax.experimental import pallas as pl
from jax.experimental.pallas import tpu as pltpu

F32 = jnp.float32
BF16 = jnp.bfloat16

D_MODEL = 1024
DEPTH = 4
GRID_W = 64
HY_WIDTH = D_MODEL
HY_ORDER = 2
POS_EMB_DIM = 33
POS_BANDS = (POS_EMB_DIM - 1) // 2
FILTER_WIDTH = 64
FILTER_EPS = 1e-6
HEAD_DIM = 128
N_Q_HEADS = D_MODEL // HEAD_DIM
N_KV_HEADS = 2
GROUP = N_Q_HEADS // N_KV_HEADS
ATTN_WIDTH = N_Q_HEADS * HEAD_DIM
KV_WIDTH = N_KV_HEADS * HEAD_DIM
AXIS_DIM = HEAD_DIM // 2
ROPE_THETA = 10000.0
RMS_EPS = 1e-6
LN_EPS = 1e-5
DEEPNORM_ALPHA = (2.0 * DEPTH) ** 0.25

V7X_LANES = 128
V7X_SUBLANES = 8
V7X_MXU_DIM = 256
V7X_VMEM_BYTES = 64 * 1024 * 1024

CLASS_ROWS = V7X_MXU_DIM
CONV_TC = V7X_MXU_DIM
SPEC_ROWS = 2 * V7X_SUBLANES
ATTN_TQ = 256
OUT_TM = 512
BOUNDARY_SUB_ROWS = 256
HYENA_TM = 512
FILT_PAD_K = 64
BOUND_SLACK = 1.02
MIN_SAFE_DENOM = 2.0 ** -90


def _cparams(n_axes, vmem_mb):
    return pltpu.CompilerParams(
        dimension_semantics=("arbitrary",) * n_axes,
        vmem_limit_bytes=vmem_mb * 1024 * 1024,
    )


def _const_spec(block, index_map):
    return pl.BlockSpec(block, index_map, pipeline_mode=pl.Buffered(1))


def _add(a, b):
    if a is None:
        return b
    if b is None:
        return a
    return a + b


def _sub(a, b):
    if b is None:
        return a
    if a is None:
        return -b
    return a - b


def _scale(a, c):
    return None if a is None else a * c


def _twiddle(x, w):
    re, im = x
    c, s = w.real, w.imag
    if abs(s) < 1e-12:
        return (re, im) if c > 0 else (_sub(None, re), _sub(None, im))
    if abs(c) < 1e-12:
        return (_sub(None, im), re) if s > 0 else (im, _sub(None, re))
    if abs(abs(c) - abs(s)) < 1e-12:
        a = abs(c)
        total = _add(re, im)
        if c > 0 and s > 0:
            return (_scale(_sub(re, im), a), _scale(total, a))
        if c > 0:
            return (_scale(total, a), _scale(_sub(im, re), a))
        if s > 0:
            return (_scale(total, -a), _scale(_sub(re, im), a))
        return (_scale(_sub(im, re), a), _scale(total, -a))
    nre = _sub(_scale(re, c), _scale(im, s))
    nim = _add(_scale(re, s), _scale(im, c))
    return (nre, nim)


def _fft(xs, sign, first_half_only=False):
    n = len(xs)
    if n == 1:
        return list(xs)
    ev = _fft(xs[0::2], sign)
    od = _fft(xs[1::2], sign)
    out = [None] * n
    for k in range(n // 2):
        t = _twiddle(od[k], cmath.exp(sign * 2j * math.pi * k / n))
        out[k] = (_add(ev[k][0], t[0]), _add(ev[k][1], t[1]))
        if not first_half_only:
            out[k + n // 2] = (_sub(ev[k][0], t[0]), _sub(ev[k][1], t[1]))
    return out[: n // 2] if first_half_only else out


def _proj_kernel(x_ref, w_ref, b_ref, o_ref):
    x = x_ref[0].astype(BF16)
    acc = jnp.dot(x, w_ref[...], preferred_element_type=F32) + b_ref[...]
    o_ref[0] = acc.astype(o_ref.dtype)


def _hyena_proj(x, w, b):
    B, L, _ = x.shape
    n_out = w.shape[1]
    tm = HYENA_TM
    return pl.pallas_call(
        _proj_kernel,
        grid=(B, L // tm),
        in_specs=[
            pl.BlockSpec((1, tm, D_MODEL), lambda b, t: (b, t, 0)),
            _const_spec((D_MODEL, n_out), lambda b, t: (0, 0)),
            _const_spec((1, n_out), lambda b, t: (0, 0)),
        ],
        out_specs=pl.BlockSpec((1, tm, n_out), lambda b, t: (b, t, 0)),
        out_shape=jax.ShapeDtypeStruct((B, L, n_out), BF16),
        compiler_params=_cparams(2, 48),
        name="hyena_in_proj",
    )(x, w, b)


def _attn_in_epilogue(acc, q_gain, k_gain, cos2, sin2, store):
    def head(col, gain, scale):
        xh = acc[:, col:col + HEAD_DIM]
        ms = jnp.mean(xh * xh, axis=-1, keepdims=True)
        xn = xh * lax.rsqrt(ms + RMS_EPS) * gain
        rot = xn * cos2 + pltpu.roll(xn, AXIS_DIM, axis=1) * sin2
        store(col, rot * scale if scale != 1.0 else rot)

    for h in range(N_Q_HEADS):
        head(h * HEAD_DIM, q_gain, HEAD_DIM ** -0.5 * math.log2(math.e))
    store(ATTN_WIDTH, acc[:, ATTN_WIDTH:2 * ATTN_WIDTH])
    for h in range(N_KV_HEADS):
        head(2 * ATTN_WIDTH + h * HEAD_DIM, k_gain, 1.0)
    v0 = 2 * ATTN_WIDTH + KV_WIDTH
    store(v0, acc[:, v0:v0 + KV_WIDTH])


def _gated_out_ln(y, g, x, w_ref, b_ref, lg_ref, lb_ref):
    y = y.astype(F32)
    g = g.astype(F32)
    yg = (y * (g * jax.nn.sigmoid(g))).astype(BF16)
    s = jnp.dot(yg, w_ref[...], preferred_element_type=F32) + b_ref[...]
    v = DEEPNORM_ALPHA * x + s
    mu = jnp.mean(v, axis=-1, keepdims=True)
    vc = v - mu
    var = jnp.mean(vc * vc, axis=-1, keepdims=True)
    return vc * lax.rsqrt(var + LN_EPS) * lg_ref[...] + lb_ref[...]


def _boundary_kernel(*refs, attn_next, sub_rows):
    y_ref, g_ref, x_ref, wo_ref, bo_ref, lg_ref, lb_ref, wi_ref, bi_ref = refs[:9]
    if attn_next:
        qn_ref, kn_ref, cos_ref, sin_ref, xo_ref, u_ref = refs[9:]
    else:
        xo_ref, u_ref = refs[9:]
    sub_tiles = [slice(r0, r0 + sub_rows) for r0 in range(0, x_ref.shape[1], sub_rows)]
    normed = []
    for rows in sub_tiles:
        xn = _gated_out_ln(y_ref[0, rows, :], g_ref[0, rows, :], x_ref[0, rows, :],
                           wo_ref, bo_ref, lg_ref, lb_ref)
        xo_ref[0, rows, :] = xn
        normed.append(xn.astype(BF16))
    for rows, xb in zip(sub_tiles, normed):
        acc = jnp.dot(xb, wi_ref[...], preferred_element_type=F32) + bi_ref[...]
        if attn_next:
            def store(col, val, rows=rows):
                u_ref[0, rows, col:col + val.shape[1]] = val.astype(u_ref.dtype)

            _attn_in_epilogue(acc, qn_ref[...], kn_ref[...], cos_ref[rows, :], sin_ref[rows, :], store)
        else:
            u_ref[0, rows, :] = acc.astype(u_ref.dtype)


def _layer_boundary(y, u, gate_block, x, out_params, in_params, rope, name):
    B, L, _ = x.shape
    tm = OUT_TM
    attn_next = len(in_params) == 4
    n_next = in_params[0].shape[1]
    x_spec = pl.BlockSpec((1, tm, D_MODEL), lambda b_, t: (b_, t, 0))
    zero = lambda b_, t: (0, 0)
    in_specs = [
        x_spec,
        pl.BlockSpec((1, tm, D_MODEL), lambda b_, t: (b_, t, gate_block)),
        x_spec,
        _const_spec((D_MODEL, D_MODEL), zero),
        _const_spec((1, D_MODEL), zero),
        _const_spec((1, D_MODEL), zero),
        _const_spec((1, D_MODEL), zero),
        _const_spec((D_MODEL, n_next), zero),
        _const_spec((1, n_next), zero),
    ]
    args = [y, u, x, *out_params, *in_params]
    if attn_next:
        in_specs += [
            _const_spec((1, HEAD_DIM), zero),
            _const_spec((1, HEAD_DIM), zero),
            pl.BlockSpec((tm, HEAD_DIM), lambda b_, t: (t, 0)),
            pl.BlockSpec((tm, HEAD_DIM), lambda b_, t: (t, 0)),
        ]
        args += list(rope)
    return pl.pallas_call(
        functools.partial(_boundary_kernel, attn_next=attn_next, sub_rows=BOUNDARY_SUB_ROWS),
        grid=(B, L // tm),
        in_specs=in_specs,
        out_specs=[x_spec, pl.BlockSpec((1, tm, n_next), lambda b_, t: (b_, t, 0))],
        out_shape=[jax.ShapeDtypeStruct(x.shape, F32), jax.ShapeDtypeStruct((B, L, n_next), BF16)],
        compiler_params=_cparams(2, 56),
        name=name,
    )(*args)


def _out_kernel(y_ref, g_ref, x_ref, w_ref, b_ref, lg_ref, lb_ref, o_ref):
    o_ref[0] = _gated_out_ln(y_ref[0], g_ref[0], x_ref[0], w_ref, b_ref, lg_ref, lb_ref)


def _out_proj_ln(y, u, gate_block, x, w, b, lg, lb, name):
    B, L, _ = x.shape
    tm = OUT_TM
    x_spec = pl.BlockSpec((1, tm, D_MODEL), lambda b_, t: (b_, t, 0))
    zero = lambda b_, t: (0, 0)
    return pl.pallas_call(
        _out_kernel,
        grid=(B, L // tm),
        in_specs=[
            x_spec,
            pl.BlockSpec((1, tm, D_MODEL), lambda b_, t: (b_, t, gate_block)),
            x_spec,
            _const_spec((D_MODEL, D_MODEL), zero),
            _const_spec((1, D_MODEL), zero),
            _const_spec((1, D_MODEL), zero),
            _const_spec((1, D_MODEL), zero),
        ],
        out_specs=x_spec,
        out_shape=jax.ShapeDtypeStruct(x.shape, F32),
        compiler_params=_cparams(2, 40),
        name=name,
    )(y, u, x, w, b, lg, lb)


def _filter_kernel(feat_ref, w1_ref, b1_ref, fr_ref, w2_ref, b2_ref, w3f_ref, w3b_ref,
                   dcf_ref, dcb_ref, hf_ref, hb_ref, h_s):
    hi = lax.Precision.HIGHEST

    @pl.when((pl.program_id(0) == 0) & (pl.program_id(1) == 0))
    def _():
        fr = fr_ref[...]
        h = jnp.sin(fr * (jnp.dot(feat_ref[...], w1_ref[...], precision=hi, preferred_element_type=F32)
                          + b1_ref[...]))
        h = jnp.sin(fr * (jnp.dot(h, w2_ref[...], precision=hi, preferred_element_type=F32)
                          + b2_ref[...]))
        h_hi = h.astype(BF16)
        h_lo = (h - h_hi.astype(F32)).astype(BF16)
        w = FILTER_WIDTH
        h_s[:, 0:w] = h_hi
        h_s[:, w:2 * w] = h_hi
        h_s[:, 2 * w:3 * w] = h_lo
        h_s[:, 3 * w:4 * w] = jnp.zeros_like(h_lo)

    def last_layer(w_ref):
        w = w_ref[...]
        w_hi = w.astype(BF16)
        w_lo = (w - w_hi.astype(F32)).astype(BF16)
        w_cat = jnp.concatenate([w_hi, w_lo, w_hi, jnp.zeros_like(w_hi)], axis=0)
        return jnp.dot(h_s[...], w_cat, preferred_element_type=F32)

    t = feat_ref[:, 0:1]
    hf = last_layer(w3f_ref) * jnp.exp(-t * jnp.abs(dcf_ref[...]))
    hb = last_layer(w3b_ref) * jnp.exp(-t * jnp.abs(dcb_ref[...]))
    ss = jnp.sum(hf * hf + hb * hb, axis=0, keepdims=True)
    sc = lax.rsqrt(ss + FILTER_EPS)
    hf_ref[...] = hf * sc
    row = lax.broadcasted_iota(jnp.int32, hb.shape, 0)
    hb_ref[...] = jnp.where(row == 0, 0.0, hb * sc)


def _hyena_filters(feats, w1p, b1, fr, w2, b2, w3, decay):
    L = feats.shape[0]
    E = HY_WIDTH
    tc = CONV_TC
    nct = E // tc
    zero = lambda o, c: (0, 0)
    col = lambda d: (lambda o, c: (0, (2 * o + d) * nct + c))
    out_spec = pl.BlockSpec((L, tc), lambda o, c: (0, o * nct + c))
    return pl.pallas_call(
        _filter_kernel,
        grid=(HY_ORDER, nct),
        in_specs=[
            _const_spec((L, FILT_PAD_K), zero),
            _const_spec((FILT_PAD_K, FILTER_WIDTH), zero),
            _const_spec((1, FILTER_WIDTH), zero),
            _const_spec((1, FILTER_WIDTH), zero),
            _const_spec((FILTER_WIDTH, FILTER_WIDTH), zero),
            _const_spec((1, FILTER_WIDTH), zero),
            pl.BlockSpec((FILTER_WIDTH, tc), col(0)),
            pl.BlockSpec((FILTER_WIDTH, tc), col(1)),
            pl.BlockSpec((1, tc), col(0)),
            pl.BlockSpec((1, tc), col(1)),
        ],
        out_specs=[out_spec, out_spec],
        out_shape=[jax.ShapeDtypeStruct((L, HY_ORDER * E), F32)] * 2,
        scratch_shapes=[pltpu.VMEM((L, 4 * FILTER_WIDTH), BF16)],
        compiler_params=_cparams(2, 48),
        name="hyena_filter_taps",
    )(feats, w1p, b1, fr, w2, b2, w3, w3, decay, decay)


def _class_slices(ref, q, F, rows, lanes):
    return (ref[q, rows, lanes], ref[q, pl.ds(rows.start + F, rows.size), lanes])


def _kf_kernel(hf_ref, hb_ref, tf_ref, kf_ref, pa_s, pb_s, *, r, F, tc, inv_len):
    def fwd(src_ref, dst_ref):
        for q in range(r):
            h = src_ref[q * F:(q + 1) * F, :]
            h_hi = h.astype(BF16)
            h_lo = (h - h_hi.astype(F32)).astype(BF16)
            t = tf_ref[q]
            dst_ref[q] = (jnp.dot(t, h_hi, preferred_element_type=F32)
                          + jnp.dot(t, h_lo, preferred_element_type=F32))

    fwd(hf_ref, pa_s)
    fwd(hb_ref, pb_s)

    def body(i, carry):
        rows = pl.ds(pl.multiple_of(i * SPEC_ROWS, SPEC_ROWS), SPEC_ROWS)
        rows_im = pl.ds(rows.start + F, rows.size)
        for lc in range(tc // V7X_LANES):
            lanes = slice(lc * V7X_LANES, (lc + 1) * V7X_LANES)
            xa = _fft([_class_slices(pa_s, q, F, rows, lanes) for q in range(r)], -1.0)
            xb = _fft([_class_slices(pb_s, q, F, rows, lanes) for q in range(r)], -1.0)
            for j in range(r):
                kf_ref[0, j, rows, lanes] = ((xa[j][0] + xb[j][0]) * inv_len).astype(kf_ref.dtype)
                kf_ref[0, j, rows_im, lanes] = ((xa[j][1] - xb[j][1]) * inv_len).astype(kf_ref.dtype)
        return carry

    lax.fori_loop(0, F // SPEC_ROWS, body, 0)


def _filter_spectrum(hf, hb, tf, r):
    L = hf.shape[0]
    F = L // r
    E = HY_WIDTH
    tc = CONV_TC
    nct = E // tc
    kern = functools.partial(_kf_kernel, r=r, F=F, tc=tc, inv_len=1.0 / L)
    return pl.pallas_call(
        kern,
        grid=(HY_ORDER, nct),
        in_specs=[
            pl.BlockSpec((L, tc), lambda o, c: (0, o * nct + c)),
            pl.BlockSpec((L, tc), lambda o, c: (0, o * nct + c)),
            _const_spec((r, 2 * F, F), lambda o, c: (0, 0, 0)),
        ],
        out_specs=pl.BlockSpec((1, r, 2 * F, tc), lambda o, c: (o, 0, 0, c)),
        out_shape=jax.ShapeDtypeStruct((HY_ORDER, r, 2 * F, E), BF16),
        scratch_shapes=[pltpu.VMEM((r, 2 * F, tc), F32), pltpu.VMEM((r, 2 * F, tc), F32)],
        compiler_params=_cparams(2, 56),
        name="hyena_filter_spectrum",
    )(hf, hb, tf)


def _short_conv_classes(u_ref, w, b, r, F):
    def cls(k):
        return u_ref[0, k * F:(k + 1) * F, :].astype(F32)

    first = cls(0)
    row = lax.broadcasted_iota(jnp.int32, first.shape, 0)
    prev = jnp.where(row == 0, 0.0, pltpu.roll(cls(r - 1), 1, axis=0))
    cur = first
    for q in range(r):
        if q < r - 1:
            nxt = cls(q + 1)
        else:
            nxt = jnp.where(row == F - 1, 0.0, pltpu.roll(first, F - 1, axis=0))
        yield q, prev * w[0:1] + cur * w[1:2] + nxt * w[2:3] + b
        prev, cur = cur, nxt


def _conv_kernel(uv_ref, ug0_ref, ug1_ref, kf_ref, tf_ref, ti_ref, cw_ref, cb_ref, fb_ref,
                 out_ref, z_s, p_s, a_s, *, r, F, tc):
    for q, v in _short_conv_classes(uv_ref, cw_ref[0], cb_ref[0:1, :], r, F):
        z_s[q] = v

    gate_refs = (ug0_ref, ug1_ref)
    for o in range(HY_ORDER):
        for q in range(r):
            p_s[q] = jnp.dot(tf_ref[q], z_s[q].astype(BF16), preferred_element_type=F32).astype(p_s.dtype)

        def body(i, carry, o=o):
            rows = pl.ds(pl.multiple_of(i * SPEC_ROWS, SPEC_ROWS), SPEC_ROWS)
            rows_im = pl.ds(rows.start + F, rows.size)
            for lc in range(tc // V7X_LANES):
                lanes = slice(lc * V7X_LANES, (lc + 1) * V7X_LANES)
                x = _fft([_class_slices(p_s, q, F, rows, lanes) for q in range(r)], -1.0)
                y = []
                for j in range(r):
                    kre = kf_ref[o, j, rows, lanes]
                    kim = kf_ref[o, j, rows_im, lanes]
                    xre, xim = x[j]
                    y.append((xre * kre - xim * kim, xre * kim + xim * kre))
                a = _fft(y, 1.0)
                for q in range(r):
                    a_s[q, rows, lanes] = a[q][0]
                    a_s[q, rows_im, lanes] = a[q][1]
            return carry

        lax.fori_loop(0, F // SPEC_ROWS, body, 0)

        gates = _short_conv_classes(gate_refs[o], cw_ref[1 + o], cb_ref[1 + o:2 + o, :], r, F)
        for q, g in gates:
            y = jnp.dot(ti_ref[q], a_s[q], preferred_element_type=F32)
            z_new = g * (y + z_s[q] * fb_ref[o:o + 1, :])
            if o == HY_ORDER - 1:
                out_ref[0, q * F:(q + 1) * F, :] = z_new.astype(out_ref.dtype)
            else:
                z_s[q] = z_new


def _hyena_conv(u, kf, tf, ti, cw, cb, fb, r):
    B, L, _ = u.shape
    F = L // r
    E = HY_WIDTH
    tc = CONV_TC
    nct = E // tc
    kern = functools.partial(_conv_kernel, r=r, F=F, tc=tc)
    sec = lambda s: pl.BlockSpec((1, L, tc), lambda c, b: (b, 0, s * nct + c))
    return pl.pallas_call(
        kern,
        grid=(nct, B),
        in_specs=[
            sec(0), sec(1), sec(2),
            _const_spec((HY_ORDER, r, 2 * F, tc), lambda c, b: (0, 0, 0, c)),
            _const_spec((r, 2 * F, F), lambda c, b: (0, 0, 0)),
            _const_spec((r, F, 2 * F), lambda c, b: (0, 0, 0)),
            _const_spec((3, 3, tc), lambda c, b: (0, 0, c)),
            _const_spec((3, tc), lambda c, b: (0, c)),
            _const_spec((HY_ORDER, tc), lambda c, b: (0, c)),
        ],
        out_specs=pl.BlockSpec((1, L, tc), lambda c, b: (b, 0, c)),
        out_shape=jax.ShapeDtypeStruct((B, L, E), BF16),
        scratch_shapes=[pltpu.VMEM((r, F, tc), F32), pltpu.VMEM((r, 2 * F, tc), BF16),
                        pltpu.VMEM((r, 2 * F, tc), BF16)],
        compiler_params=_cparams(2, 56),
        name="hyena_long_conv",
    )(u, u, u, kf, tf, ti, cw, cb, fb)


def _attn_kernel(q_ref, k_ref, v_ref, o_ref, vt_s, kmax_s, p_s):
    @pl.when(pl.program_id(2) == 0)
    def _():
        vt_s[...] = v_ref[0].T
        kf = k_ref[0].astype(F32)
        row_sq = jnp.sum(kf * kf, axis=1, keepdims=True)
        kmax_s[...] = jnp.broadcast_to(jnp.max(row_sq, axis=0, keepdims=True), kmax_s.shape)

    k = k_ref[0]
    dims_nt = (((1,), (1,)), ((), ()))

    def q_head(g):
        return q_ref[0, :, g * HEAD_DIM:(g + 1) * HEAD_DIM]

    def finish(g, p_bf16, denom):
        ot = jnp.dot(vt_s[...], p_bf16, preferred_element_type=F32) * (1.0 / denom)
        o_ref[0, :, g * HEAD_DIM:(g + 1) * HEAD_DIM] = ot.T.astype(o_ref.dtype)

    ones = jnp.ones((V7X_SUBLANES, HEAD_DIM), BF16)
    denoms = []
    for g in range(GROUP):
        q = q_head(g)
        qf = q.astype(F32)
        q_sq = lax.dot_general(ones, (qf * qf).astype(BF16), dims_nt, preferred_element_type=F32)
        bound = BOUND_SLACK * jnp.sqrt(q_sq[0:1] * kmax_s[0:1])
        st = lax.dot_general(k, q, dims_nt, preferred_element_type=F32)
        p = jnp.exp2(st - bound)
        denoms.append(jnp.sum(p, axis=0, keepdims=True))
        p_s[g] = p.astype(BF16)
    for g in range(GROUP):
        finish(g, p_s[g], denoms[g])

    smallest = jnp.min(jnp.minimum(jnp.minimum(denoms[0], denoms[1]), jnp.minimum(denoms[2], denoms[3])))

    @pl.when(jnp.logical_not(smallest >= MIN_SAFE_DENOM))
    def _():
        for g in range(GROUP):
            st = lax.dot_general(k, q_head(g), dims_nt, preferred_element_type=F32)
            p = jnp.exp2(st - jnp.max(st, axis=0, keepdims=True))
            finish(g, p.astype(BF16), jnp.sum(p, axis=0, keepdims=True))


def _attention(u):
    B, L, _ = u.shape
    tq = ATTN_TQ
    gw = GROUP * HEAD_DIM
    k_block = 2 * ATTN_WIDTH // HEAD_DIM
    v_block = (2 * ATTN_WIDTH + KV_WIDTH) // HEAD_DIM
    return pl.pallas_call(
        _attn_kernel,
        grid=(B, N_KV_HEADS, L // tq),
        in_specs=[
            pl.BlockSpec((1, tq, gw), lambda b, h, t: (b, t, h)),
            pl.BlockSpec((1, L, HEAD_DIM), lambda b, h, t: (b, 0, k_block + h)),
            pl.BlockSpec((1, L, HEAD_DIM), lambda b, h, t: (b, 0, v_block + h)),
        ],
        out_specs=pl.BlockSpec((1, tq, gw), lambda b, h, t: (b, t, h)),
        out_shape=jax.ShapeDtypeStruct((B, L, ATTN_WIDTH), BF16),
        scratch_shapes=[pltpu.VMEM((HEAD_DIM, L), BF16),
                        pltpu.VMEM((V7X_SUBLANES, tq), F32),
                        pltpu.VMEM((GROUP, L, tq), BF16)],
        compiler_params=_cparams(3, 56),
        name="gqa_attention",
    )(u, u, u)


def _class_dft_tables(L, r):
    F = L // r
    f = jnp.arange(F, dtype=jnp.int32)
    m = jnp.arange(F, dtype=jnp.int32)
    q = jnp.arange(r, dtype=jnp.int32)
    n = ((2 * f + 1)[None, :, None] * (r * m[None, None, :] + q[:, None, None])) % (4 * L)
    theta = n.astype(F32) * (math.pi / (2 * L))
    c = jnp.cos(theta)
    s = jnp.sin(theta)
    tf = jnp.concatenate([c, -s], axis=1).astype(BF16)
    ti = jnp.concatenate([c.transpose(0, 2, 1), -s.transpose(0, 2, 1)], axis=2).astype(BF16)
    return tf, ti


def _filter_features(L):
    t = jnp.linspace(0.0, 1.0, L, dtype=F32)[:, None]
    w = 2.0 * math.pi * jnp.arange(L, dtype=F32)[:, None] / L
    f = jnp.linspace(1e-4, POS_BANDS - 1, POS_BANDS, dtype=F32)[None]
    feats = jnp.concatenate([t, jnp.cos(f * w), -jnp.sin(f * w)], axis=-1)
    return jnp.pad(feats, ((0, 0), (0, FILT_PAD_K - POS_EMB_DIM)))


def _rope_tables(L):
    rows = L // GRID_W
    row = jnp.repeat(jnp.arange(rows, dtype=F32), GRID_W)
    col = jnp.tile(jnp.arange(GRID_W, dtype=F32), rows)
    inv = ROPE_THETA ** (-jnp.arange(0, AXIS_DIM, 2, dtype=F32) / AXIS_DIM)
    ang = jnp.concatenate([row[:, None] * inv, col[:, None] * inv], axis=-1)
    c, s = jnp.cos(ang), jnp.sin(ang)
    return jnp.concatenate([c, c], axis=-1), jnp.concatenate([-s, s], axis=-1)


def _to_class_major(a, r):
    lead, L, C = a.shape[:-2], a.shape[-2], a.shape[-1]
    return a.reshape(lead + (L // r, r, C)).swapaxes(-3, -2).reshape(a.shape)


def _from_class_major(a, r):
    lead, L, C = a.shape[:-2], a.shape[-2], a.shape[-1]
    return a.reshape(lead + (r, L // r, C)).swapaxes(-3, -2).reshape(a.shape)


def _attn_in_params(p, j):
    w = p["at_w_in"][j]
    bi = p["at_b_in"][j]
    kv0 = ATTN_WIDTH
    g0 = ATTN_WIDTH + 2 * KV_WIDTH
    w = jnp.concatenate([w[:, :kv0], w[:, g0:], w[:, kv0:g0]], axis=1).astype(BF16)
    bi = jnp.concatenate([bi[:kv0], bi[g0:], bi[kv0:g0]])[None]
    return (w, bi, p["at_q_norm"][j][None], p["at_k_norm"][j][None])


def _trunk(x, p):
    B, L, _ = x.shape
    r = L // CLASS_ROWS
    tf, ti = _class_dft_tables(L, r)
    feats = _to_class_major(_filter_features(L), r)
    rope = tuple(_to_class_major(t, r) for t in _rope_tables(L))
    x = _to_class_major(x, r)

    def in_params(i):
        j = i // 2
        if i % 2 == 0:
            return (p["hy_w_in"][j].astype(BF16), p["hy_b_in"][j][None])
        return _attn_in_params(p, j)

    def out_params(i):
        j = i // 2
        w, b = (p["hy_w_out"], p["hy_b_out"]) if i % 2 == 0 else (p["at_w_out"], p["at_b_out"])
        return (w[j].astype(BF16), b[j][None], p["ln_g"][i][None], p["ln_b"][i][None])

    u = _hyena_proj(x, *in_params(0))
    for i in range(DEPTH):
        j = i // 2
        if i % 2 == 0:
            w1p = jnp.pad(p["hy_f_w1"][j], ((0, FILT_PAD_K - POS_EMB_DIM), (0, 0)))
            hf, hb = _hyena_filters(feats, w1p, p["hy_f_b1"][j][None], p["hy_f_freq"][j][None],
                                    p["hy_f_w2"][j], p["hy_f_b2"][j][None], p["hy_f_w3"][j],
                                    p["hy_decay"][j][None])
            kf = _filter_spectrum(hf, hb, tf, r)
            cw = p["hy_conv_w"][j].reshape(3, 3, HY_WIDTH).transpose(1, 0, 2)
            cb = p["hy_conv_b"][j].reshape(3, HY_WIDTH)
            y = _hyena_conv(u, kf, tf, ti, cw, cb, p["hy_filt_bias"][j], r)
            gate_block = 3 * HY_WIDTH // D_MODEL
            name = "hyena_out"
        else:
            y = _attention(u)
            gate_block = 1
            name = "attn_out"
        if i + 1 < DEPTH:
            x, u = _layer_boundary(y, u, gate_block, x, out_params(i), in_params(i + 1), rope,
                                   name + "_next_in")
        else:
            x = _out_proj_ln(y, u, gate_block, x, *out_params(i), name + "_ln")
    return _from_class_major(x, r)


def kernel(x_prompt, x_sample, hy_w_in, hy_b_in, hy_conv_w, hy_conv_b, hy_f_w1, hy_f_b1, hy_f_freq, hy_f_w2,
           hy_f_b2, hy_f_w3, hy_decay, hy_filt_bias, hy_w_out, hy_b_out, at_w_in, at_b_in, at_q_norm, at_k_norm,
           at_w_out, at_b_out, ln_g, ln_b):
    p = dict(hy_w_in=hy_w_in, hy_b_in=hy_b_in, hy_conv_w=hy_conv_w, hy_conv_b=hy_conv_b, hy_f_w1=hy_f_w1,
             hy_f_b1=hy_f_b1, hy_f_freq=hy_f_freq, hy_f_w2=hy_f_w2, hy_f_b2=hy_f_b2, hy_f_w3=hy_f_w3,
             hy_decay=hy_decay, hy_filt_bias=hy_filt_bias, hy_w_out=hy_w_out, hy_b_out=hy_b_out,
             at_w_in=at_w_in, at_b_in=at_b_in, at_q_norm=at_q_norm, at_k_norm=at_k_norm, at_w_out=at_w_out,
             at_b_out=at_b_out, ln_g=ln_g, ln_b=ln_b)
    return (_trunk(x_prompt, p), _trunk(x_sample, p))
```

```python
import cmath
import functools
import math

import jax
import jax.numpy as jnp
from jax import lax
from jax.experimental import pallas as pl
from jax.experimental.pallas import tpu as pltpu

F32 = jnp.float32
BF16 = jnp.bfloat16

D_MODEL = 1024
DEPTH = 4
GRID_W = 64
HY_WIDTH = D_MODEL
HY_ORDER = 2
POS_EMB_DIM = 33
POS_BANDS = (POS_EMB_DIM - 1) // 2
FILTER_WIDTH = 64
FILTER_EPS = 1e-6
HEAD_DIM = 128
N_Q_HEADS = D_MODEL // HEAD_DIM
N_KV_HEADS = 2
GROUP = N_Q_HEADS // N_KV_HEADS
ATTN_WIDTH = N_Q_HEADS * HEAD_DIM
KV_WIDTH = N_KV_HEADS * HEAD_DIM
AXIS_DIM = HEAD_DIM // 2
ROPE_THETA = 10000.0
RMS_EPS = 1e-6
LN_EPS = 1e-5
DEEPNORM_ALPHA = (2.0 * DEPTH) ** 0.25

V7X_LANES = 128
V7X_SUBLANES = 8
V7X_MXU_DIM = 256
V7X_VMEM_BYTES = 64 * 1024 * 1024

CLASS_ROWS = V7X_MXU_DIM
CONV_TC = V7X_MXU_DIM
SPEC_ROWS = 2 * V7X_SUBLANES
ATTN_TQ = 256
OUT_TM = 512
BOUNDARY_SUB_ROWS = 256
HYENA_TM = 512
FILT_PAD_K = 64
BOUND_SLACK = 1.02
MIN_SAFE_DENOM = 2.0 ** -90


def _cparams(n_axes, vmem_mb):
    return pltpu.CompilerParams(
        dimension_semantics=("arbitrary",) * n_axes,
        vmem_limit_bytes=vmem_mb * 1024 * 1024,
    )


def _const_spec(block, index_map):
    return pl.BlockSpec(block, index_map, pipeline_mode=pl.Buffered(1))


def _add(a, b):
    if a is None:
        return b
    if b is None:
        return a
    return a + b


def _sub(a, b):
    if b is None:
        return a
    if a is None:
        return -b
    return a - b


def _scale(a, c):
    return None if a is None else a * c


def _twiddle(x, w):
    re, im = x
    c, s = w.real, w.imag
    if abs(s) < 1e-12:
        return (re, im) if c > 0 else (_sub(None, re), _sub(None, im))
    if abs(c) < 1e-12:
        return (_sub(None, im), re) if s > 0 else (im, _sub(None, re))
    if abs(abs(c) - abs(s)) < 1e-12:
        a = abs(c)
        total = _add(re, im)
        if c > 0 and s > 0:
            return (_scale(_sub(re, im), a), _scale(total, a))
        if c > 0:
            return (_scale(total, a), _scale(_sub(im, re), a))
        if s > 0:
            return (_scale(total, -a), _scale(_sub(re, im), a))
        return (_scale(_sub(im, re), a), _scale(total, -a))
    nre = _sub(_scale(re, c), _scale(im, s))
    nim = _add(_scale(re, s), _scale(im, c))
    return (nre, nim)


def _fft(xs, sign, first_half_only=False):
    n = len(xs)
    if n == 1:
        return list(xs)
    ev = _fft(xs[0::2], sign)
    od = _fft(xs[1::2], sign)
    out = [None] * n
    for k in range(n // 2):
        t = _twiddle(od[k], cmath.exp(sign * 2j * math.pi * k / n))
        out[k] = (_add(ev[k][0], t[0]), _add(ev[k][1], t[1]))
        if not first_half_only:
            out[k + n // 2] = (_sub(ev[k][0], t[0]), _sub(ev[k][1], t[1]))
    return out[: n // 2] if first_half_only else out


def _store_col_tiles(o_ref, rows, acc):
    for c in range(acc.shape[1] // CONV_TC):
        o_ref[0, c, rows, :] = acc[:, c * CONV_TC:(c + 1) * CONV_TC].astype(o_ref.dtype)


def _load_col_tiles(ref, rows):
    return jnp.concatenate([ref[0, c, rows, :] for c in range(ref.shape[1])], axis=1)


def _tile_major_shape(B, L, n):
    return (B, n // CONV_TC, L, CONV_TC)


def _proj_kernel(x_ref, w_ref, b_ref, o_ref):
    x = x_ref[0].astype(BF16)
    acc = jnp.dot(x, w_ref[...], preferred_element_type=F32) + b_ref[...]
    _store_col_tiles(o_ref, slice(None), acc)


def _hyena_proj(x, w, b):
    B, L, _ = x.shape
    n_out = w.shape[1]
    tm = HYENA_TM
    return pl.pallas_call(
        _proj_kernel,
        grid=(B, L // tm),
        in_specs=[
            pl.BlockSpec((1, tm, D_MODEL), lambda b, t: (b, t, 0)),
            _const_spec((D_MODEL, n_out), lambda b, t: (0, 0)),
            _const_spec((1, n_out), lambda b, t: (0, 0)),
        ],
        out_specs=pl.BlockSpec((1, n_out // CONV_TC, tm, CONV_TC), lambda b, t: (b, 0, t, 0)),
        out_shape=jax.ShapeDtypeStruct(_tile_major_shape(B, L, n_out), BF16),
        compiler_params=_cparams(2, 48),
        name="hyena_in_proj",
    )(x, w, b)


def _attn_in_epilogue(acc, q_gain, k_gain, cos2, sin2, store):
    def head(col, gain, scale):
        xh = acc[:, col:col + HEAD_DIM]
        ms = jnp.mean(xh * xh, axis=-1, keepdims=True)
        xn = xh * lax.rsqrt(ms + RMS_EPS) * gain
        rot = xn * cos2 + pltpu.roll(xn, AXIS_DIM, axis=1) * sin2
        store(col, rot * scale if scale != 1.0 else rot)

    for h in range(N_Q_HEADS):
        head(h * HEAD_DIM, q_gain, HEAD_DIM ** -0.5 * math.log2(math.e))
    store(ATTN_WIDTH, acc[:, ATTN_WIDTH:2 * ATTN_WIDTH])
    for h in range(N_KV_HEADS):
        head(2 * ATTN_WIDTH + h * HEAD_DIM, k_gain, 1.0)
    v0 = 2 * ATTN_WIDTH + KV_WIDTH
    store(v0, acc[:, v0:v0 + KV_WIDTH])


def _gated_out_ln(y, g, x, w_ref, b_ref, lg_ref, lb_ref):
    y = y.astype(F32)
    g = g.astype(F32)
    yg = (y * (g * jax.nn.sigmoid(g))).astype(BF16)
    s = jnp.dot(yg, w_ref[...], preferred_element_type=F32) + b_ref[...]
    v = DEEPNORM_ALPHA * x + s
    mu = jnp.mean(v, axis=-1, keepdims=True)
    vc = v - mu
    var = jnp.mean(vc * vc, axis=-1, keepdims=True)
    return vc * lax.rsqrt(var + LN_EPS) * lg_ref[...] + lb_ref[...]


def _boundary_kernel(*refs, attn_next, sub_rows):
    y_ref, g_ref, x_ref, wo_ref, bo_ref, lg_ref, lb_ref, wi_ref, bi_ref = refs[:9]
    if attn_next:
        qn_ref, kn_ref, cos_ref, sin_ref, xo_ref, u_ref = refs[9:]
    else:
        xo_ref, u_ref = refs[9:]
    sub_tiles = [slice(r0, r0 + sub_rows) for r0 in range(0, x_ref.shape[1], sub_rows)]
    normed = []
    for rows in sub_tiles:
        if attn_next:
            y, g = _load_col_tiles(y_ref, rows), _load_col_tiles(g_ref, rows)
        else:
            y, g = y_ref[0, rows, :], g_ref[0, rows, :]
        xn = _gated_out_ln(y, g, x_ref[0, rows, :], wo_ref, bo_ref, lg_ref, lb_ref)
        xo_ref[0, rows, :] = xn
        normed.append(xn.astype(BF16))
    for rows, xb in zip(sub_tiles, normed):
        acc = jnp.dot(xb, wi_ref[...], preferred_element_type=F32) + bi_ref[...]
        if attn_next:
            def store(col, val, rows=rows):
                u_ref[0, rows, col:col + val.shape[1]] = val.astype(u_ref.dtype)

            _attn_in_epilogue(acc, qn_ref[...], kn_ref[...], cos_ref[rows, :], sin_ref[rows, :], store)
        else:
            _store_col_tiles(u_ref, rows, acc)


def _layer_boundary(y, u, gate_block, x, out_params, in_params, rope, name):
    B, L, _ = x.shape
    tm = OUT_TM
    attn_next = len(in_params) == 4
    n_next = in_params[0].shape[1]
    x_spec = pl.BlockSpec((1, tm, D_MODEL), lambda b_, t: (b_, t, 0))
    zero = lambda b_, t: (0, 0)
    if attn_next:
        tiles = D_MODEL // CONV_TC
        y_spec = pl.BlockSpec((1, tiles, tm, CONV_TC), lambda b_, t: (b_, 0, t, 0))
        g_spec = pl.BlockSpec((1, tiles, tm, CONV_TC), lambda b_, t: (b_, gate_block, t, 0))
        u_spec = pl.BlockSpec((1, tm, n_next), lambda b_, t: (b_, t, 0))
        u_shape = (B, L, n_next)
    else:
        y_spec = x_spec
        g_spec = pl.BlockSpec((1, tm, D_MODEL), lambda b_, t: (b_, t, gate_block))
        u_spec = pl.BlockSpec((1, n_next // CONV_TC, tm, CONV_TC), lambda b_, t: (b_, 0, t, 0))
        u_shape = _tile_major_shape(B, L, n_next)
    in_specs = [
        y_spec,
        g_spec,
        x_spec,
        _const_spec((D_MODEL, D_MODEL), zero),
        _const_spec((1, D_MODEL), zero),
        _const_spec((1, D_MODEL), zero),
        _const_spec((1, D_MODEL), zero),
        _const_spec((D_MODEL, n_next), zero),
        _const_spec((1, n_next), zero),
    ]
    args = [y, u, x, *out_params, *in_params]
    if attn_next:
        in_specs += [
            _const_spec((1, HEAD_DIM), zero),
            _const_spec((1, HEAD_DIM), zero),
            pl.BlockSpec((tm, HEAD_DIM), lambda b_, t: (t, 0)),
            pl.BlockSpec((tm, HEAD_DIM), lambda b_, t: (t, 0)),
        ]
        args += list(rope)
    return pl.pallas_call(
        functools.partial(_boundary_kernel, attn_next=attn_next, sub_rows=BOUNDARY_SUB_ROWS),
        grid=(B, L // tm),
        in_specs=in_specs,
        out_specs=[x_spec, u_spec],
        out_shape=[jax.ShapeDtypeStruct(x.shape, F32), jax.ShapeDtypeStruct(u_shape, BF16)],
        compiler_params=_cparams(2, 56),
        name=name,
    )(*args)


def _out_kernel(y_ref, g_ref, x_ref, w_ref, b_ref, lg_ref, lb_ref, o_ref):
    o_ref[0] = _gated_out_ln(y_ref[0], g_ref[0], x_ref[0], w_ref, b_ref, lg_ref, lb_ref)


def _out_proj_ln(y, u, gate_block, x, w, b, lg, lb, name):
    B, L, _ = x.shape
    tm = OUT_TM
    x_spec = pl.BlockSpec((1, tm, D_MODEL), lambda b_, t: (b_, t, 0))
    zero = lambda b_, t: (0, 0)
    return pl.pallas_call(
        _out_kernel,
        grid=(B, L // tm),
        in_specs=[
            x_spec,
            pl.BlockSpec((1, tm, D_MODEL), lambda b_, t: (b_, t, gate_block)),
            x_spec,
            _const_spec((D_MODEL, D_MODEL), zero),
            _const_spec((1, D_MODEL), zero),
            _const_spec((1, D_MODEL), zero),
            _const_spec((1, D_MODEL), zero),
        ],
        out_specs=x_spec,
        out_shape=jax.ShapeDtypeStruct(x.shape, F32),
        compiler_params=_cparams(2, 40),
        name=name,
    )(y, u, x, w, b, lg, lb)


def _filter_kernel(feat_ref, w1_ref, b1_ref, fr_ref, w2_ref, b2_ref, w3f_ref, w3b_ref,
                   dcf_ref, dcb_ref, hf_ref, hb_ref, h_s):
    hi = lax.Precision.HIGHEST

    @pl.when((pl.program_id(0) == 0) & (pl.program_id(1) == 0))
    def _():
        fr = fr_ref[...]
        h = jnp.sin(fr * (jnp.dot(feat_ref[...], w1_ref[...], precision=hi, preferred_element_type=F32)
                          + b1_ref[...]))
        h = jnp.sin(fr * (jnp.dot(h, w2_ref[...], precision=hi, preferred_element_type=F32)
                          + b2_ref[...]))
        h_hi = h.astype(BF16)
        h_lo = (h - h_hi.astype(F32)).astype(BF16)
        w = FILTER_WIDTH
        h_s[:, 0:w] = h_hi
        h_s[:, w:2 * w] = h_hi
        h_s[:, 2 * w:3 * w] = h_lo
        h_s[:, 3 * w:4 * w] = jnp.zeros_like(h_lo)

    def last_layer(w_ref):
        w = w_ref[...]
        w_hi = w.astype(BF16)
        w_lo = (w - w_hi.astype(F32)).astype(BF16)
        w_cat = jnp.concatenate([w_hi, w_lo, w_hi, jnp.zeros_like(w_hi)], axis=0)
        return jnp.dot(h_s[...], w_cat, preferred_element_type=F32)

    t = feat_ref[:, 0:1]
    hf = last_layer(w3f_ref) * jnp.exp(-t * jnp.abs(dcf_ref[...]))
    hb = last_layer(w3b_ref) * jnp.exp(-t * jnp.abs(dcb_ref[...]))
    ss = jnp.sum(hf * hf + hb * hb, axis=0, keepdims=True)
    sc = lax.rsqrt(ss + FILTER_EPS)
    hf_ref[...] = hf * sc
    row = lax.broadcasted_iota(jnp.int32, hb.shape, 0)
    hb_ref[...] = jnp.where(row == 0, 0.0, hb * sc)


def _hyena_filters(feats, w1p, b1, fr, w2, b2, w3, decay):
    L = feats.shape[0]
    E = HY_WIDTH
    tc = CONV_TC
    nct = E // tc
    zero = lambda o, c: (0, 0)
    col = lambda d: (lambda o, c: (0, (2 * o + d) * nct + c))
    out_spec = pl.BlockSpec((L, tc), lambda o, c: (0, o * nct + c))
    return pl.pallas_call(
        _filter_kernel,
        grid=(HY_ORDER, nct),
        in_specs=[
            _const_spec((L, FILT_PAD_K), zero),
            _const_spec((FILT_PAD_K, FILTER_WIDTH), zero),
            _const_spec((1, FILTER_WIDTH), zero),
            _const_spec((1, FILTER_WIDTH), zero),
            _const_spec((FILTER_WIDTH, FILTER_WIDTH), zero),
            _const_spec((1, FILTER_WIDTH), zero),
            pl.BlockSpec((FILTER_WIDTH, tc), col(0)),
            pl.BlockSpec((FILTER_WIDTH, tc), col(1)),
            pl.BlockSpec((1, tc), col(0)),
            pl.BlockSpec((1, tc), col(1)),
        ],
        out_specs=[out_spec, out_spec],
        out_shape=[jax.ShapeDtypeStruct((L, HY_ORDER * E), F32)] * 2,
        scratch_shapes=[pltpu.VMEM((L, 4 * FILTER_WIDTH), BF16)],
        compiler_params=_cparams(2, 48),
        name="hyena_filter_taps",
    )(feats, w1p, b1, fr, w2, b2, w3, w3, decay, decay)


def _class_slices(ref, q, F, rows, lanes):
    return (ref[q, rows, lanes], ref[q, pl.ds(rows.start + F, rows.size), lanes])


def _kf_kernel(hf_ref, hb_ref, tf_ref, kf_ref, pa_s, pb_s, *, r, F, tc, inv_len):
    def fwd(src_ref, dst_ref):
        for q in range(r):
            h = src_ref[q * F:(q + 1) * F, :]
            h_hi = h.astype(BF16)
            h_lo = (h - h_hi.astype(F32)).astype(BF16)
            t = tf_ref[q]
            dst_ref[q] = (jnp.dot(t, h_hi, preferred_element_type=F32)
                          + jnp.dot(t, h_lo, preferred_element_type=F32))

    fwd(hf_ref, pa_s)
    fwd(hb_ref, pb_s)

    def body(i, carry):
        rows = pl.ds(pl.multiple_of(i * SPEC_ROWS, SPEC_ROWS), SPEC_ROWS)
        rows_im = pl.ds(rows.start + F, rows.size)
        for lc in range(tc // V7X_LANES):
            lanes = slice(lc * V7X_LANES, (lc + 1) * V7X_LANES)
            xa = _fft([_class_slices(pa_s, q, F, rows, lanes) for q in range(r)], -1.0)
            xb = _fft([_class_slices(pb_s, q, F, rows, lanes) for q in range(r)], -1.0)
            for j in range(r):
                kf_ref[0, j, rows, lanes] = ((xa[j][0] + xb[j][0]) * inv_len).astype(kf_ref.dtype)
                kf_ref[0, j, rows_im, lanes] = ((xa[j][1] - xb[j][1]) * inv_len).astype(kf_ref.dtype)
        return carry

    lax.fori_loop(0, F // SPEC_ROWS, body, 0)


def _filter_spectrum(hf, hb, tf, r):
    L = hf.shape[0]
    F = L // r
    E = HY_WIDTH
    tc = CONV_TC
    nct = E // tc
    kern = functools.partial(_kf_kernel, r=r, F=F, tc=tc, inv_len=1.0 / L)
    return pl.pallas_call(
        kern,
        grid=(HY_ORDER, nct),
        in_specs=[
            pl.BlockSpec((L, tc), lambda o, c: (0, o * nct + c)),
            pl.BlockSpec((L, tc), lambda o, c: (0, o * nct + c)),
            _const_spec((r, 2 * F, F), lambda o, c: (0, 0, 0)),
        ],
        out_specs=pl.BlockSpec((1, r, 2 * F, tc), lambda o, c: (o, 0, 0, c)),
        out_shape=jax.ShapeDtypeStruct((HY_ORDER, r, 2 * F, E), BF16),
        scratch_shapes=[pltpu.VMEM((r, 2 * F, tc), F32), pltpu.VMEM((r, 2 * F, tc), F32)],
        compiler_params=_cparams(2, 56),
        name="hyena_filter_spectrum",
    )(hf, hb, tf)


def _short_conv_classes(u_ref, w, b, r, F):
    def cls(k):
        return u_ref[k * F:(k + 1) * F, :].astype(F32)

    first = cls(0)
    row = lax.broadcasted_iota(jnp.int32, first.shape, 0)
    prev = jnp.where(row == 0, 0.0, pltpu.roll(cls(r - 1), 1, axis=0))
    cur = first
    for q in range(r):
        if q < r - 1:
            nxt = cls(q + 1)
        else:
            nxt = jnp.where(row == F - 1, 0.0, pltpu.roll(first, F - 1, axis=0))
        yield q, prev * w[0:1] + cur * w[1:2] + nxt * w[2:3] + b
        prev, cur = cur, nxt


def _conv_kernel(uv_ref, ug0_ref, ug1_ref, kf_ref, tf_ref, ti_ref, cw_ref, cb_ref, fb_ref,
                 out_ref, z_s, p_s, a_s, *, r, F, tc):
    for q, v in _short_conv_classes(uv_ref, cw_ref[0], cb_ref[0:1, :], r, F):
        z_s[q] = v

    gate_refs = (ug0_ref, ug1_ref)
    for o in range(HY_ORDER):
        for q in range(r):
            p_s[q] = jnp.dot(tf_ref[q], z_s[q].astype(BF16), preferred_element_type=F32).astype(p_s.dtype)

        def body(i, carry, o=o):
            rows = pl.ds(pl.multiple_of(i * SPEC_ROWS, SPEC_ROWS), SPEC_ROWS)
            rows_im = pl.ds(rows.start + F, rows.size)
            for lc in range(tc // V7X_LANES):
                lanes = slice(lc * V7X_LANES, (lc + 1) * V7X_LANES)
                x = _fft([_class_slices(p_s, q, F, rows, lanes) for q in range(r)], -1.0)
                y = []
                for j in range(r):
                    kre = kf_ref[o, j, rows, lanes]
                    kim = kf_ref[o, j, rows_im, lanes]
                    xre, xim = x[j]
                    y.append((xre * kre - xim * kim, xre * kim + xim * kre))
                a = _fft(y, 1.0)
                for q in range(r):
                    a_s[q, rows, lanes] = a[q][0]
                    a_s[q, rows_im, lanes] = a[q][1]
            return carry

        lax.fori_loop(0, F // SPEC_ROWS, body, 0)

        gates = _short_conv_classes(gate_refs[o], cw_ref[1 + o], cb_ref[1 + o:2 + o, :], r, F)
        for q, g in gates:
            y = jnp.dot(ti_ref[q], a_s[q], preferred_element_type=F32)
            z_new = g * (y + z_s[q] * fb_ref[o:o + 1, :])
            if o == HY_ORDER - 1:
                out_ref[q * F:(q + 1) * F, :] = z_new.astype(out_ref.dtype)
            else:
                z_s[q] = z_new


def _hyena_conv(u, kf, tf, ti, cw, cb, fb, r):
    B, _, L, _ = u.shape
    F = L // r
    E = HY_WIDTH
    tc = CONV_TC
    nct = E // tc
    kern = functools.partial(_conv_kernel, r=r, F=F, tc=tc)
    sec = lambda s: pl.BlockSpec((None, None, L, tc), lambda c, b: (b, s * nct + c, 0, 0))
    return pl.pallas_call(
        kern,
        grid=(nct, B),
        in_specs=[
            sec(0), sec(1), sec(2),
            _const_spec((HY_ORDER, r, 2 * F, tc), lambda c, b: (0, 0, 0, c)),
            _const_spec((r, 2 * F, F), lambda c, b: (0, 0, 0)),
            _const_spec((r, F, 2 * F), lambda c, b: (0, 0, 0)),
            _const_spec((3, 3, tc), lambda c, b: (0, 0, c)),
            _const_spec((3, tc), lambda c, b: (0, c)),
            _const_spec((HY_ORDER, tc), lambda c, b: (0, c)),
        ],
        out_specs=pl.BlockSpec((None, None, L, tc), lambda c, b: (b, c, 0, 0)),
        out_shape=jax.ShapeDtypeStruct(_tile_major_shape(B, L, E), BF16),
        scratch_shapes=[pltpu.VMEM((r, F, tc), F32), pltpu.VMEM((r, 2 * F, tc), BF16),
                        pltpu.VMEM((r, 2 * F, tc), BF16)],
        compiler_params=_cparams(2, 56),
        name="hyena_long_conv",
    )(u, u, u, kf, tf, ti, cw, cb, fb)


def _attn_kernel(q_ref, k_ref, v_ref, o_ref, vt_s, kmax_s, p_s):
    @pl.when(pl.program_id(2) == 0)
    def _():
        vt_s[...] = v_ref[0].T
        kf = k_ref[0].astype(F32)
        row_sq = jnp.sum(kf * kf, axis=1, keepdims=True)
        kmax_s[...] = jnp.broadcast_to(jnp.max(row_sq, axis=0, keepdims=True), kmax_s.shape)

    k = k_ref[0]
    dims_nt = (((1,), (1,)), ((), ()))

    def q_head(g):
        return q_ref[0, :, g * HEAD_DIM:(g + 1) * HEAD_DIM]

    def finish(g, p_bf16, denom):
        ot = jnp.dot(vt_s[...], p_bf16, preferred_element_type=F32) * (1.0 / denom)
        o_ref[0, :, g * HEAD_DIM:(g + 1) * HEAD_DIM] = ot.T.astype(o_ref.dtype)

    ones = jnp.ones((V7X_SUBLANES, HEAD_DIM), BF16)
    denoms = []
    for g in range(GROUP):
        q = q_head(g)
        qf = q.astype(F32)
        q_sq = lax.dot_general(ones, (qf * qf).astype(BF16), dims_nt, preferred_element_type=F32)
        bound = BOUND_SLACK * jnp.sqrt(q_sq[0:1] * kmax_s[0:1])
        st = lax.dot_general(k, q, dims_nt, preferred_element_type=F32)
        p = jnp.exp2(st - bound)
        denoms.append(jnp.sum(p, axis=0, keepdims=True))
        p_s[g] = p.astype(BF16)
    for g in range(GROUP):
        finish(g, p_s[g], denoms[g])

    smallest = jnp.min(jnp.minimum(jnp.minimum(denoms[0], denoms[1]), jnp.minimum(denoms[2], denoms[3])))

    @pl.when(jnp.logical_not(smallest >= MIN_SAFE_DENOM))
    def _():
        for g in range(GROUP):
            st = lax.dot_general(k, q_head(g), dims_nt, preferred_element_type=F32)
            p = jnp.exp2(st - jnp.max(st, axis=0, keepdims=True))
            finish(g, p.astype(BF16), jnp.sum(p, axis=0, keepdims=True))


def _attention(u):
    B, L, _ = u.shape
    tq = ATTN_TQ
    gw = GROUP * HEAD_DIM
    k_block = 2 * ATTN_WIDTH // HEAD_DIM
    v_block = (2 * ATTN_WIDTH + KV_WIDTH) // HEAD_DIM
    return pl.pallas_call(
        _attn_kernel,
        grid=(B, N_KV_HEADS, L // tq),
        in_specs=[
            pl.BlockSpec((1, tq, gw), lambda b, h, t: (b, t, h)),
            pl.BlockSpec((1, L, HEAD_DIM), lambda b, h, t: (b, 0, k_block + h)),
            pl.BlockSpec((1, L, HEAD_DIM), lambda b, h, t: (b, 0, v_block + h)),
        ],
        out_specs=pl.BlockSpec((1, tq, gw), lambda b, h, t: (b, t, h)),
        out_shape=jax.ShapeDtypeStruct((B, L, ATTN_WIDTH), BF16),
        scratch_shapes=[pltpu.VMEM((HEAD_DIM, L), BF16),
                        pltpu.VMEM((V7X_SUBLANES, tq), F32),
                        pltpu.VMEM((GROUP, L, tq), BF16)],
        compiler_params=_cparams(3, 56),
        name="gqa_attention",
    )(u, u, u)


def _class_dft_tables(L, r):
    F = L // r
    f = jnp.arange(F, dtype=jnp.int32)
    m = jnp.arange(F, dtype=jnp.int32)
    q = jnp.arange(r, dtype=jnp.int32)
    n = ((2 * f + 1)[None, :, None] * (r * m[None, None, :] + q[:, None, None])) % (4 * L)
    theta = n.astype(F32) * (math.pi / (2 * L))
    c = jnp.cos(theta)
    s = jnp.sin(theta)
    tf = jnp.concatenate([c, -s], axis=1).astype(BF16)
    ti = jnp.concatenate([c.transpose(0, 2, 1), -s.transpose(0, 2, 1)], axis=2).astype(BF16)
    return tf, ti


def _filter_features(L):
    t = jnp.linspace(0.0, 1.0, L, dtype=F32)[:, None]
    w = 2.0 * math.pi * jnp.arange(L, dtype=F32)[:, None] / L
    f = jnp.linspace(1e-4, POS_BANDS - 1, POS_BANDS, dtype=F32)[None]
    feats = jnp.concatenate([t, jnp.cos(f * w), -jnp.sin(f * w)], axis=-1)
    return jnp.pad(feats, ((0, 0), (0, FILT_PAD_K - POS_EMB_DIM)))


def _rope_tables(L):
    rows = L // GRID_W
    row = jnp.repeat(jnp.arange(rows, dtype=F32), GRID_W)
    col = jnp.tile(jnp.arange(GRID_W, dtype=F32), rows)
    inv = ROPE_THETA ** (-jnp.arange(0, AXIS_DIM, 2, dtype=F32) / AXIS_DIM)
    ang = jnp.concatenate([row[:, None] * inv, col[:, None] * inv], axis=-1)
    c, s = jnp.cos(ang), jnp.sin(ang)
    return jnp.concatenate([c, c], axis=-1), jnp.concatenate([-s, s], axis=-1)


def _to_class_major(a, r):
    lead, L, C = a.shape[:-2], a.shape[-2], a.shape[-1]
    return a.reshape(lead + (L // r, r, C)).swapaxes(-3, -2).reshape(a.shape)


def _from_class_major(a, r):
    lead, L, C = a.shape[:-2], a.shape[-2], a.shape[-1]
    return a.reshape(lead + (r, L // r, C)).swapaxes(-3, -2).reshape(a.shape)


def _attn_in_params(p, j):
    w = p["at_w_in"][j]
    bi = p["at_b_in"][j]
    kv0 = ATTN_WIDTH
    g0 = ATTN_WIDTH + 2 * KV_WIDTH
    w = jnp.concatenate([w[:, :kv0], w[:, g0:], w[:, kv0:g0]], axis=1).astype(BF16)
    bi = jnp.concatenate([bi[:kv0], bi[g0:], bi[kv0:g0]])[None]
    return (w, bi, p["at_q_norm"][j][None], p["at_k_norm"][j][None])


def _trunk(x, p):
    B, L, _ = x.shape
    r = L // CLASS_ROWS
    tf, ti = _class_dft_tables(L, r)
    feats = _to_class_major(_filter_features(L), r)
    rope = tuple(_to_class_major(t, r) for t in _rope_tables(L))
    x = _to_class_major(x, r)

    def in_params(i):
        j = i // 2
        if i % 2 == 0:
            return (p["hy_w_in"][j].astype(BF16), p["hy_b_in"][j][None])
        return _attn_in_params(p, j)

    def out_params(i):
        j = i // 2
        w, b = (p["hy_w_out"], p["hy_b_out"]) if i % 2 == 0 else (p["at_w_out"], p["at_b_out"])
        return (w[j].astype(BF16), b[j][None], p["ln_g"][i][None], p["ln_b"][i][None])

    u = _hyena_proj(x, *in_params(0))
    for i in range(DEPTH):
        j = i // 2
        if i % 2 == 0:
            w1p = jnp.pad(p["hy_f_w1"][j], ((0, FILT_PAD_K - POS_EMB_DIM), (0, 0)))
            hf, hb = _hyena_filters(feats, w1p, p["hy_f_b1"][j][None], p["hy_f_freq"][j][None],
                                    p["hy_f_w2"][j], p["hy_f_b2"][j][None], p["hy_f_w3"][j],
                                    p["hy_decay"][j][None])
            kf = _filter_spectrum(hf, hb, tf, r)
            cw = p["hy_conv_w"][j].reshape(3, 3, HY_WIDTH).transpose(1, 0, 2)
            cb = p["hy_conv_b"][j].reshape(3, HY_WIDTH)
            y = _hyena_conv(u, kf, tf, ti, cw, cb, p["hy_filt_bias"][j], r)
            gate_block = 3 * HY_WIDTH // D_MODEL
            name = "hyena_out"
        else:
            y = _attention(u)
            gate_block = 1
            name = "attn_out"
        if i + 1 < DEPTH:
            x, u = _layer_boundary(y, u, gate_block, x, out_params(i), in_params(i + 1), rope,
                                   name + "_next_in")
        else:
            x = _out_proj_ln(y, u, gate_block, x, *out_params(i), name + "_ln")
    return _from_class_major(x, r)


def kernel(x_prompt, x_sample, hy_w_in, hy_b_in, hy_conv_w, hy_conv_b, hy_f_w1, hy_f_b1, hy_f_freq, hy_f_w2,
           hy_f_b2, hy_f_w3, hy_decay, hy_filt_bias, hy_w_out, hy_b_out, at_w_in, at_b_in, at_q_norm, at_k_norm,
           at_w_out, at_b_out, ln_g, ln_b):
    p = dict(hy_w_in=hy_w_in, hy_b_in=hy_b_in, hy_conv_w=hy_conv_w, hy_conv_b=hy_conv_b, hy_f_w1=hy_f_w1,
             hy_f_b1=hy_f_b1, hy_f_freq=hy_f_freq, hy_f_w2=hy_f_w2, hy_f_b2=hy_f_b2, hy_f_w3=hy_f_w3,
             hy_decay=hy_decay, hy_filt_bias=hy_filt_bias, hy_w_out=hy_w_out, hy_b_out=hy_b_out,
             at_w_in=at_w_in, at_b_in=at_b_in, at_q_norm=at_q_norm, at_k_norm=at_k_norm, at_w_out=at_w_out,
             at_b_out=at_b_out, ln_g=ln_g, ln_b=ln_b)
    return (_trunk(x_prompt, p), _trunk(x_sample, p))
```

```python
import cmath
import functools
import math

import jax
import jax.numpy as jnp
from jax import lax
from jax.experimental import pallas as pl
from jax.experimental.pallas import tpu as pltpu

F32 = jnp.float32
BF16 = jnp.bfloat16

D_MODEL = 1024
DEPTH = 4
GRID_W = 64
HY_WIDTH = D_MODEL
HY_ORDER = 2
POS_EMB_DIM = 33
POS_BANDS = (POS_EMB_DIM - 1) // 2
FILTER_WIDTH = 64
FILTER_EPS = 1e-6
HEAD_DIM = 128
N_Q_HEADS = D_MODEL // HEAD_DIM
N_KV_HEADS = 2
GROUP = N_Q_HEADS // N_KV_HEADS
ATTN_WIDTH = N_Q_HEADS * HEAD_DIM
KV_WIDTH = N_KV_HEADS * HEAD_DIM
AXIS_DIM = HEAD_DIM // 2
ROPE_THETA = 10000.0
RMS_EPS = 1e-6
LN_EPS = 1e-5
DEEPNORM_ALPHA = (2.0 * DEPTH) ** 0.25

V7X_LANES = 128
V7X_SUBLANES = 8
V7X_MXU_DIM = 256
V7X_VMEM_BYTES = 64 * 1024 * 1024

CLASS_ROWS = V7X_MXU_DIM
CONV_TC = V7X_MXU_DIM
SPEC_ROWS = 2 * V7X_SUBLANES
ATTN_TQ = 512
OUT_TM = 512
BOUNDARY_SUB_ROWS = 256
HYENA_TM = 512
FILT_PAD_K = 64
BOUND_SLACK = 1.02
MIN_SAFE_DENOM = 2.0 ** -90


def _cparams(n_axes, vmem_mb):
    return pltpu.CompilerParams(
        dimension_semantics=("arbitrary",) * n_axes,
        vmem_limit_bytes=vmem_mb * 1024 * 1024,
    )


def _const_spec(block, index_map):
    return pl.BlockSpec(block, index_map, pipeline_mode=pl.Buffered(1))


def _add(a, b):
    if a is None:
        return b
    if b is None:
        return a
    return a + b


def _sub(a, b):
    if b is None:
        return a
    if a is None:
        return -b
    return a - b


def _scale(a, c):
    return None if a is None else a * c


def _twiddle(x, w):
    re, im = x
    c, s = w.real, w.imag
    if abs(s) < 1e-12:
        return (re, im) if c > 0 else (_sub(None, re), _sub(None, im))
    if abs(c) < 1e-12:
        return (_sub(None, im), re) if s > 0 else (im, _sub(None, re))
    if abs(abs(c) - abs(s)) < 1e-12:
        a = abs(c)
        total = _add(re, im)
        if c > 0 and s > 0:
            return (_scale(_sub(re, im), a), _scale(total, a))
        if c > 0:
            return (_scale(total, a), _scale(_sub(im, re), a))
        if s > 0:
            return (_scale(total, -a), _scale(_sub(re, im), a))
        return (_scale(_sub(im, re), a), _scale(total, -a))
    nre = _sub(_scale(re, c), _scale(im, s))
    nim = _add(_scale(re, s), _scale(im, c))
    return (nre, nim)


def _fft(xs, sign, first_half_only=False):
    n = len(xs)
    if n == 1:
        return list(xs)
    ev = _fft(xs[0::2], sign)
    od = _fft(xs[1::2], sign)
    out = [None] * n
    for k in range(n // 2):
        t = _twiddle(od[k], cmath.exp(sign * 2j * math.pi * k / n))
        out[k] = (_add(ev[k][0], t[0]), _add(ev[k][1], t[1]))
        if not first_half_only:
            out[k + n // 2] = (_sub(ev[k][0], t[0]), _sub(ev[k][1], t[1]))
    return out[: n // 2] if first_half_only else out


def _store_col_tiles(o_ref, rows, acc):
    for c in range(acc.shape[1] // CONV_TC):
        o_ref[0, c, rows, :] = acc[:, c * CONV_TC:(c + 1) * CONV_TC].astype(o_ref.dtype)


def _load_col_tiles(ref, rows):
    return jnp.concatenate([ref[0, c, rows, :] for c in range(ref.shape[1])], axis=1)


def _tile_major_shape(B, L, n):
    return (B, n // CONV_TC, L, CONV_TC)


def _proj_kernel(x_ref, w_ref, b_ref, o_ref):
    x = x_ref[0].astype(BF16)
    acc = jnp.dot(x, w_ref[...], preferred_element_type=F32) + b_ref[...]
    _store_col_tiles(o_ref, slice(None), acc)


def _hyena_proj(x, w, b):
    B, L, _ = x.shape
    n_out = w.shape[1]
    tm = HYENA_TM
    return pl.pallas_call(
        _proj_kernel,
        grid=(B, L // tm),
        in_specs=[
            pl.BlockSpec((1, tm, D_MODEL), lambda b, t: (b, t, 0)),
            _const_spec((D_MODEL, n_out), lambda b, t: (0, 0)),
            _const_spec((1, n_out), lambda b, t: (0, 0)),
        ],
        out_specs=pl.BlockSpec((1, n_out // CONV_TC, tm, CONV_TC), lambda b, t: (b, 0, t, 0)),
        out_shape=jax.ShapeDtypeStruct(_tile_major_shape(B, L, n_out), BF16),
        compiler_params=_cparams(2, 48),
        name="hyena_in_proj",
    )(x, w, b)


def _attn_in_epilogue(acc, q_gain, k_gain, cos2, sin2, store):
    def head(col, gain, scale):
        xh = acc[:, col:col + HEAD_DIM]
        ms = jnp.mean(xh * xh, axis=-1, keepdims=True)
        xn = xh * lax.rsqrt(ms + RMS_EPS) * gain
        rot = xn * cos2 + pltpu.roll(xn, AXIS_DIM, axis=1) * sin2
        store(col, rot * scale if scale != 1.0 else rot)

    for h in range(N_Q_HEADS):
        head(h * HEAD_DIM, q_gain, HEAD_DIM ** -0.5 * math.log2(math.e))
    store(ATTN_WIDTH, acc[:, ATTN_WIDTH:2 * ATTN_WIDTH])
    for h in range(N_KV_HEADS):
        head(2 * ATTN_WIDTH + h * HEAD_DIM, k_gain, 1.0)
    v0 = 2 * ATTN_WIDTH + KV_WIDTH
    store(v0, acc[:, v0:v0 + KV_WIDTH])


def _gated_out_ln(y, g, x, w_ref, b_ref, lg_ref, lb_ref):
    y = y.astype(F32)
    g = g.astype(F32)
    yg = (y * (g * jax.nn.sigmoid(g))).astype(BF16)
    s = jnp.dot(yg, w_ref[...], preferred_element_type=F32) + b_ref[...]
    v = DEEPNORM_ALPHA * x + s
    mu = jnp.mean(v, axis=-1, keepdims=True)
    vc = v - mu
    var = jnp.mean(vc * vc, axis=-1, keepdims=True)
    return vc * lax.rsqrt(var + LN_EPS) * lg_ref[...] + lb_ref[...]


def _boundary_kernel(*refs, attn_next, sub_rows):
    y_ref, g_ref, x_ref, wo_ref, bo_ref, lg_ref, lb_ref, wi_ref, bi_ref = refs[:9]
    if attn_next:
        qn_ref, kn_ref, cos_ref, sin_ref, xo_ref, u_ref = refs[9:]
    else:
        xo_ref, u_ref = refs[9:]
    sub_tiles = [slice(r0, r0 + sub_rows) for r0 in range(0, x_ref.shape[1], sub_rows)]
    normed = []
    for rows in sub_tiles:
        if attn_next:
            y, g = _load_col_tiles(y_ref, rows), _load_col_tiles(g_ref, rows)
        else:
            y, g = y_ref[0, rows, :], g_ref[0, rows, :]
        xn = _gated_out_ln(y, g, x_ref[0, rows, :], wo_ref, bo_ref, lg_ref, lb_ref)
        xo_ref[0, rows, :] = xn
        normed.append(xn.astype(BF16))
    for rows, xb in zip(sub_tiles, normed):
        acc = jnp.dot(xb, wi_ref[...], preferred_element_type=F32) + bi_ref[...]
        if attn_next:
            def store(col, val, rows=rows):
                u_ref[0, rows, col:col + val.shape[1]] = val.astype(u_ref.dtype)

            _attn_in_epilogue(acc, qn_ref[...], kn_ref[...], cos_ref[rows, :], sin_ref[rows, :], store)
        else:
            _store_col_tiles(u_ref, rows, acc)


def _layer_boundary(y, u, gate_block, x, out_params, in_params, rope, name):
    B, L, _ = x.shape
    attn_next = len(in_params) == 4
    n_next = in_params[0].shape[1]
    tm = 2 * OUT_TM if attn_next else OUT_TM
    x_spec = pl.BlockSpec((1, tm, D_MODEL), lambda b_, t: (b_, t, 0))
    zero = lambda b_, t: (0, 0)
    if attn_next:
        tiles = D_MODEL // CONV_TC
        y_spec = pl.BlockSpec((1, tiles, tm, CONV_TC), lambda b_, t: (b_, 0, t, 0))
        g_spec = pl.BlockSpec((1, tiles, tm, CONV_TC), lambda b_, t: (b_, gate_block, t, 0))
        u_spec = pl.BlockSpec((1, tm, n_next), lambda b_, t: (b_, t, 0))
        u_shape = (B, L, n_next)
    else:
        y_spec = x_spec
        g_spec = pl.BlockSpec((1, tm, D_MODEL), lambda b_, t: (b_, t, gate_block))
        u_spec = pl.BlockSpec((1, n_next // CONV_TC, tm, CONV_TC), lambda b_, t: (b_, 0, t, 0))
        u_shape = _tile_major_shape(B, L, n_next)
    in_specs = [
        y_spec,
        g_spec,
        x_spec,
        _const_spec((D_MODEL, D_MODEL), zero),
        _const_spec((1, D_MODEL), zero),
        _const_spec((1, D_MODEL), zero),
        _const_spec((1, D_MODEL), zero),
        _const_spec((D_MODEL, n_next), zero),
        _const_spec((1, n_next), zero),
    ]
    args = [y, u, x, *out_params, *in_params]
    if attn_next:
        in_specs += [
            _const_spec((1, HEAD_DIM), zero),
            _const_spec((1, HEAD_DIM), zero),
            pl.BlockSpec((tm, HEAD_DIM), lambda b_, t: (t, 0)),
            pl.BlockSpec((tm, HEAD_DIM), lambda b_, t: (t, 0)),
        ]
        args += list(rope)
    return pl.pallas_call(
        functools.partial(_boundary_kernel, attn_next=attn_next, sub_rows=BOUNDARY_SUB_ROWS),
        grid=(B, L // tm),
        in_specs=in_specs,
        out_specs=[x_spec, u_spec],
        out_shape=[jax.ShapeDtypeStruct(x.shape, F32), jax.ShapeDtypeStruct(u_shape, BF16)],
        compiler_params=_cparams(2, 56),
        name=name,
    )(*args)


def _out_kernel(y_ref, g_ref, x_ref, w_ref, b_ref, lg_ref, lb_ref, o_ref):
    o_ref[0] = _gated_out_ln(y_ref[0], g_ref[0], x_ref[0], w_ref, b_ref, lg_ref, lb_ref)


def _out_proj_ln(y, u, gate_block, x, w, b, lg, lb, name):
    B, L, _ = x.shape
    tm = OUT_TM
    x_spec = pl.BlockSpec((1, tm, D_MODEL), lambda b_, t: (b_, t, 0))
    zero = lambda b_, t: (0, 0)
    return pl.pallas_call(
        _out_kernel,
        grid=(B, L // tm),
        in_specs=[
            x_spec,
            pl.BlockSpec((1, tm, D_MODEL), lambda b_, t: (b_, t, gate_block)),
            x_spec,
            _const_spec((D_MODEL, D_MODEL), zero),
            _const_spec((1, D_MODEL), zero),
            _const_spec((1, D_MODEL), zero),
            _const_spec((1, D_MODEL), zero),
        ],
        out_specs=x_spec,
        out_shape=jax.ShapeDtypeStruct(x.shape, F32),
        compiler_params=_cparams(2, 40),
        name=name,
    )(y, u, x, w, b, lg, lb)


def _filter_kernel(feat_ref, w1_ref, b1_ref, fr_ref, w2_ref, b2_ref, w3f_ref, w3b_ref,
                   dcf_ref, dcb_ref, hf_ref, hb_ref, h_s):
    hi = lax.Precision.HIGHEST

    @pl.when((pl.program_id(0) == 0) & (pl.program_id(1) == 0))
    def _():
        fr = fr_ref[...]
        h = jnp.sin(fr * (jnp.dot(feat_ref[...], w1_ref[...], precision=hi, preferred_element_type=F32)
                          + b1_ref[...]))
        h = jnp.sin(fr * (jnp.dot(h, w2_ref[...], precision=hi, preferred_element_type=F32)
                          + b2_ref[...]))
        h_hi = h.astype(BF16)
        h_lo = (h - h_hi.astype(F32)).astype(BF16)
        w = FILTER_WIDTH
        h_s[:, 0:w] = h_hi
        h_s[:, w:2 * w] = h_hi
        h_s[:, 2 * w:3 * w] = h_lo
        h_s[:, 3 * w:4 * w] = jnp.zeros_like(h_lo)

    def last_layer(w_ref):
        w = w_ref[...]
        w_hi = w.astype(BF16)
        w_lo = (w - w_hi.astype(F32)).astype(BF16)
        w_cat = jnp.concatenate([w_hi, w_lo, w_hi, jnp.zeros_like(w_hi)], axis=0)
        return jnp.dot(h_s[...], w_cat, preferred_element_type=F32)

    t = feat_ref[:, 0:1]
    hf = last_layer(w3f_ref) * jnp.exp(-t * jnp.abs(dcf_ref[...]))
    hb = last_layer(w3b_ref) * jnp.exp(-t * jnp.abs(dcb_ref[...]))
    ss = jnp.sum(hf * hf + hb * hb, axis=0, keepdims=True)
    sc = lax.rsqrt(ss + FILTER_EPS)
    hf_ref[...] = hf * sc
    row = lax.broadcasted_iota(jnp.int32, hb.shape, 0)
    hb_ref[...] = jnp.where(row == 0, 0.0, hb * sc)


def _hyena_filters(feats, w1p, b1, fr, w2, b2, w3, decay):
    L = feats.shape[0]
    E = HY_WIDTH
    tc = CONV_TC
    nct = E // tc
    zero = lambda o, c: (0, 0)
    col = lambda d: (lambda o, c: (0, (2 * o + d) * nct + c))
    out_spec = pl.BlockSpec((L, tc), lambda o, c: (0, o * nct + c))
    return pl.pallas_call(
        _filter_kernel,
        grid=(HY_ORDER, nct),
        in_specs=[
            _const_spec((L, FILT_PAD_K), zero),
            _const_spec((FILT_PAD_K, FILTER_WIDTH), zero),
            _const_spec((1, FILTER_WIDTH), zero),
            _const_spec((1, FILTER_WIDTH), zero),
            _const_spec((FILTER_WIDTH, FILTER_WIDTH), zero),
            _const_spec((1, FILTER_WIDTH), zero),
            pl.BlockSpec((FILTER_WIDTH, tc), col(0)),
            pl.BlockSpec((FILTER_WIDTH, tc), col(1)),
            pl.BlockSpec((1, tc), col(0)),
            pl.BlockSpec((1, tc), col(1)),
        ],
        out_specs=[out_spec, out_spec],
        out_shape=[jax.ShapeDtypeStruct((L, HY_ORDER * E), F32)] * 2,
        scratch_shapes=[pltpu.VMEM((L, 4 * FILTER_WIDTH), BF16)],
        compiler_params=_cparams(2, 48),
        name="hyena_filter_taps",
    )(feats, w1p, b1, fr, w2, b2, w3, w3, decay, decay)


def _class_slices(ref, q, F, rows, lanes):
    return (ref[q, rows, lanes], ref[q, pl.ds(rows.start + F, rows.size), lanes])


def _kf_kernel(hf_ref, hb_ref, tf_ref, kf_ref, pa_s, pb_s, *, r, F, tc, inv_len):
    def fwd(src_ref, dst_ref):
        for q in range(r):
            h = src_ref[q * F:(q + 1) * F, :].astype(BF16)
            dst_ref[q] = jnp.dot(tf_ref[q], h, preferred_element_type=F32)

    fwd(hf_ref, pa_s)
    fwd(hb_ref, pb_s)

    def body(i, carry):
        rows = pl.ds(pl.multiple_of(i * SPEC_ROWS, SPEC_ROWS), SPEC_ROWS)
        rows_im = pl.ds(rows.start + F, rows.size)
        for lc in range(tc // V7X_LANES):
            lanes = slice(lc * V7X_LANES, (lc + 1) * V7X_LANES)
            xa = _fft([_class_slices(pa_s, q, F, rows, lanes) for q in range(r)], -1.0)
            xb = _fft([_class_slices(pb_s, q, F, rows, lanes) for q in range(r)], -1.0)
            for j in range(r):
                kf_ref[0, j, rows, lanes] = ((xa[j][0] + xb[j][0]) * inv_len).astype(kf_ref.dtype)
                kf_ref[0, j, rows_im, lanes] = ((xa[j][1] - xb[j][1]) * inv_len).astype(kf_ref.dtype)
        return carry

    lax.fori_loop(0, F // SPEC_ROWS, body, 0)


def _filter_spectrum(hf, hb, tf, r):
    L = hf.shape[0]
    F = L // r
    E = HY_WIDTH
    tc = CONV_TC
    nct = E // tc
    kern = functools.partial(_kf_kernel, r=r, F=F, tc=tc, inv_len=1.0 / L)
    return pl.pallas_call(
        kern,
        grid=(HY_ORDER, nct),
        in_specs=[
            pl.BlockSpec((L, tc), lambda o, c: (0, o * nct + c)),
            pl.BlockSpec((L, tc), lambda o, c: (0, o * nct + c)),
            _const_spec((r, 2 * F, F), lambda o, c: (0, 0, 0)),
        ],
        out_specs=pl.BlockSpec((1, r, 2 * F, tc), lambda o, c: (o, 0, 0, c)),
        out_shape=jax.ShapeDtypeStruct((HY_ORDER, r, 2 * F, E), BF16),
        scratch_shapes=[pltpu.VMEM((r, 2 * F, tc), F32), pltpu.VMEM((r, 2 * F, tc), F32)],
        compiler_params=_cparams(2, 56),
        name="hyena_filter_spectrum",
    )(hf, hb, tf)


def _short_conv_classes(u_ref, w, b, r, F):
    def cls(k):
        return u_ref[k * F:(k + 1) * F, :].astype(F32)

    first = cls(0)
    row = lax.broadcasted_iota(jnp.int32, first.shape, 0)
    prev = jnp.where(row == 0, 0.0, pltpu.roll(cls(r - 1), 1, axis=0))
    cur = first
    for q in range(r):
        if q < r - 1:
            nxt = cls(q + 1)
        else:
            nxt = jnp.where(row == F - 1, 0.0, pltpu.roll(first, F - 1, axis=0))
        yield q, prev * w[0:1] + cur * w[1:2] + nxt * w[2:3] + b
        prev, cur = cur, nxt


def _conv_kernel(uv_ref, ug0_ref, ug1_ref, kf_ref, tf_ref, ti_ref, cw_ref, cb_ref, fb_ref,
                 out_ref, z_s, p_s, a_s, *, r, F, tc):
    for q, v in _short_conv_classes(uv_ref, cw_ref[0], cb_ref[0:1, :], r, F):
        z_s[q] = v

    gate_refs = (ug0_ref, ug1_ref)
    for o in range(HY_ORDER):
        for q in range(r):
            p_s[q] = jnp.dot(tf_ref[q], z_s[q].astype(BF16), preferred_element_type=F32).astype(p_s.dtype)

        def body(i, carry, o=o):
            rows = pl.ds(pl.multiple_of(i * SPEC_ROWS, SPEC_ROWS), SPEC_ROWS)
            rows_im = pl.ds(rows.start + F, rows.size)
            for lc in range(tc // V7X_LANES):
                lanes = slice(lc * V7X_LANES, (lc + 1) * V7X_LANES)
                x = _fft([_class_slices(p_s, q, F, rows, lanes) for q in range(r)], -1.0)
                y = []
                for j in range(r):
                    kre = kf_ref[o, j, rows, lanes]
                    kim = kf_ref[o, j, rows_im, lanes]
                    xre, xim = x[j]
                    y.append((xre * kre - xim * kim, xre * kim + xim * kre))
                a = _fft(y, 1.0)
                for q in range(r):
                    a_s[q, rows, lanes] = a[q][0]
                    a_s[q, rows_im, lanes] = a[q][1]
            return carry

        lax.fori_loop(0, F // SPEC_ROWS, body, 0)

        gates = _short_conv_classes(gate_refs[o], cw_ref[1 + o], cb_ref[1 + o:2 + o, :], r, F)
        for q, g in gates:
            y = jnp.dot(ti_ref[q], a_s[q], preferred_element_type=F32)
            z_new = g * (y + z_s[q] * fb_ref[o:o + 1, :])
            if o == HY_ORDER - 1:
                out_ref[q * F:(q + 1) * F, :] = z_new.astype(out_ref.dtype)
            else:
                z_s[q] = z_new


def _hyena_conv(u, kf, tf, ti, cw, cb, fb, r):
    B, _, L, _ = u.shape
    F = L // r
    E = HY_WIDTH
    tc = CONV_TC
    nct = E // tc
    kern = functools.partial(_conv_kernel, r=r, F=F, tc=tc)
    sec = lambda s: pl.BlockSpec((None, None, L, tc), lambda c, b: (b, s * nct + c, 0, 0))
    return pl.pallas_call(
        kern,
        grid=(nct, B),
        in_specs=[
            sec(0), sec(1), sec(2),
            _const_spec((HY_ORDER, r, 2 * F, tc), lambda c, b: (0, 0, 0, c)),
            _const_spec((r, 2 * F, F), lambda c, b: (0, 0, 0)),
            _const_spec((r, F, 2 * F), lambda c, b: (0, 0, 0)),
            _const_spec((3, 3, tc), lambda c, b: (0, 0, c)),
            _const_spec((3, tc), lambda c, b: (0, c)),
            _const_spec((HY_ORDER, tc), lambda c, b: (0, c)),
        ],
        out_specs=pl.BlockSpec((None, None, L, tc), lambda c, b: (b, c, 0, 0)),
        out_shape=jax.ShapeDtypeStruct(_tile_major_shape(B, L, E), BF16),
        scratch_shapes=[pltpu.VMEM((r, F, tc), F32), pltpu.VMEM((r, 2 * F, tc), BF16),
                        pltpu.VMEM((r, 2 * F, tc), BF16)],
        compiler_params=_cparams(2, 56),
        name="hyena_long_conv",
    )(u, u, u, kf, tf, ti, cw, cb, fb)


def _attn_kernel(q_ref, k_ref, v_ref, o_ref, vt_s, kmax_s, p_s):
    @pl.when(pl.program_id(2) == 0)
    def _():
        vt_s[...] = v_ref[0].T
        kf = k_ref[0].astype(F32)
        row_sq = jnp.sum(kf * kf, axis=1, keepdims=True)
        kmax_s[...] = jnp.broadcast_to(jnp.max(row_sq, axis=0, keepdims=True), kmax_s.shape)

    k = k_ref[0]
    dims_nt = (((1,), (1,)), ((), ()))

    def q_head(g):
        return q_ref[0, :, g * HEAD_DIM:(g + 1) * HEAD_DIM]

    def finish(g, p_bf16, denom):
        ot = jnp.dot(vt_s[...], p_bf16, preferred_element_type=F32) * (1.0 / denom)
        o_ref[0, :, g * HEAD_DIM:(g + 1) * HEAD_DIM] = ot.T.astype(o_ref.dtype)

    ones = jnp.ones((V7X_SUBLANES, HEAD_DIM), BF16)
    denoms = []
    for g in range(GROUP):
        q = q_head(g)
        qf = q.astype(F32)
        q_sq = lax.dot_general(ones, (qf * qf).astype(BF16), dims_nt, preferred_element_type=F32)
        bound = BOUND_SLACK * jnp.sqrt(q_sq[0:1] * kmax_s[0:1])
        st = lax.dot_general(k, q, dims_nt, preferred_element_type=F32)
        p = jnp.exp2(st - bound)
        denoms.append(jnp.sum(p, axis=0, keepdims=True))
        p_s[g] = p.astype(BF16)
    for g in range(GROUP):
        finish(g, p_s[g], denoms[g])

    smallest = jnp.min(jnp.minimum(jnp.minimum(denoms[0], denoms[1]), jnp.minimum(denoms[2], denoms[3])))

    @pl.when(jnp.logical_not(smallest >= MIN_SAFE_DENOM))
    def _():
        for g in range(GROUP):
            st = lax.dot_general(k, q_head(g), dims_nt, preferred_element_type=F32)
            p = jnp.exp2(st - jnp.max(st, axis=0, keepdims=True))
            finish(g, p.astype(BF16), jnp.sum(p, axis=0, keepdims=True))


def _attention(u):
    B, L, _ = u.shape
    tq = ATTN_TQ
    gw = GROUP * HEAD_DIM
    k_block = 2 * ATTN_WIDTH // HEAD_DIM
    v_block = (2 * ATTN_WIDTH + KV_WIDTH) // HEAD_DIM
    return pl.pallas_call(
        _attn_kernel,
        grid=(B, N_KV_HEADS, L // tq),
        in_specs=[
            pl.BlockSpec((1, tq, gw), lambda b, h, t: (b, t, h)),
            pl.BlockSpec((1, L, HEAD_DIM), lambda b, h, t: (b, 0, k_block + h)),
            pl.BlockSpec((1, L, HEAD_DIM), lambda b, h, t: (b, 0, v_block + h)),
        ],
        out_specs=pl.BlockSpec((1, tq, gw), lambda b, h, t: (b, t, h)),
        out_shape=jax.ShapeDtypeStruct((B, L, ATTN_WIDTH), BF16),
        scratch_shapes=[pltpu.VMEM((HEAD_DIM, L), BF16),
                        pltpu.VMEM((V7X_SUBLANES, tq), F32),
                        pltpu.VMEM((GROUP, L, tq), BF16)],
        compiler_params=_cparams(3, 56),
        name="gqa_attention",
    )(u, u, u)


def _class_dft_tables(L, r):
    F = L // r
    f = jnp.arange(F, dtype=jnp.int32)
    m = jnp.arange(F, dtype=jnp.int32)
    q = jnp.arange(r, dtype=jnp.int32)
    n = ((2 * f + 1)[None, :, None] * (r * m[None, None, :] + q[:, None, None])) % (4 * L)
    theta = n.astype(F32) * (math.pi / (2 * L))
    c = jnp.cos(theta)
    s = jnp.sin(theta)
    tf = jnp.concatenate([c, -s], axis=1).astype(BF16)
    ti = jnp.concatenate([c.transpose(0, 2, 1), -s.transpose(0, 2, 1)], axis=2).astype(BF16)
    return tf, ti


def _filter_features(L):
    t = jnp.linspace(0.0, 1.0, L, dtype=F32)[:, None]
    w = 2.0 * math.pi * jnp.arange(L, dtype=F32)[:, None] / L
    f = jnp.linspace(1e-4, POS_BANDS - 1, POS_BANDS, dtype=F32)[None]
    feats = jnp.concatenate([t, jnp.cos(f * w), -jnp.sin(f * w)], axis=-1)
    return jnp.pad(feats, ((0, 0), (0, FILT_PAD_K - POS_EMB_DIM)))


def _rope_tables(L):
    rows = L // GRID_W
    row = jnp.repeat(jnp.arange(rows, dtype=F32), GRID_W)
    col = jnp.tile(jnp.arange(GRID_W, dtype=F32), rows)
    inv = ROPE_THETA ** (-jnp.arange(0, AXIS_DIM, 2, dtype=F32) / AXIS_DIM)
    ang = jnp.concatenate([row[:, None] * inv, col[:, None] * inv], axis=-1)
    c, s = jnp.cos(ang), jnp.sin(ang)
    return jnp.concatenate([c, c], axis=-1), jnp.concatenate([-s, s], axis=-1)


def _to_class_major(a, r):
    lead, L, C = a.shape[:-2], a.shape[-2], a.shape[-1]
    return a.reshape(lead + (L // r, r, C)).swapaxes(-3, -2).reshape(a.shape)


def _from_class_major(a, r):
    lead, L, C = a.shape[:-2], a.shape[-2], a.shape[-1]
    return a.reshape(lead + (r, L // r, C)).swapaxes(-3, -2).reshape(a.shape)


def _attn_in_params(p, j):
    w = p["at_w_in"][j]
    bi = p["at_b_in"][j]
    kv0 = ATTN_WIDTH
    g0 = ATTN_WIDTH + 2 * KV_WIDTH
    w = jnp.concatenate([w[:, :kv0], w[:, g0:], w[:, kv0:g0]], axis=1).astype(BF16)
    bi = jnp.concatenate([bi[:kv0], bi[g0:], bi[kv0:g0]])[None]
    return (w, bi, p["at_q_norm"][j][None], p["at_k_norm"][j][None])


def _trunk(x, p):
    B, L, _ = x.shape
    r = L // CLASS_ROWS
    tf, ti = _class_dft_tables(L, r)
    feats = _to_class_major(_filter_features(L), r)
    rope = tuple(_to_class_major(t, r) for t in _rope_tables(L))
    x = _to_class_major(x, r)

    def in_params(i):
        j = i // 2
        if i % 2 == 0:
            return (p["hy_w_in"][j].astype(BF16), p["hy_b_in"][j][None])
        return _attn_in_params(p, j)

    def out_params(i):
        j = i // 2
        w, b = (p["hy_w_out"], p["hy_b_out"]) if i % 2 == 0 else (p["at_w_out"], p["at_b_out"])
        return (w[j].astype(BF16), b[j][None], p["ln_g"][i][None], p["ln_b"][i][None])

    u = _hyena_proj(x, *in_params(0))
    for i in range(DEPTH):
        j = i // 2
        if i % 2 == 0:
            w1p = jnp.pad(p["hy_f_w1"][j], ((0, FILT_PAD_K - POS_EMB_DIM), (0, 0)))
            hf, hb = _hyena_filters(feats, w1p, p["hy_f_b1"][j][None], p["hy_f_freq"][j][None],
                                    p["hy_f_w2"][j], p["hy_f_b2"][j][None], p["hy_f_w3"][j],
                                    p["hy_decay"][j][None])
            kf = _filter_spectrum(hf, hb, tf, r)
            cw = p["hy_conv_w"][j].reshape(3, 3, HY_WIDTH).transpose(1, 0, 2)
            cb = p["hy_conv_b"][j].reshape(3, HY_WIDTH)
            y = _hyena_conv(u, kf, tf, ti, cw, cb, p["hy_filt_bias"][j], r)
            gate_block = 3 * HY_WIDTH // D_MODEL
            name = "hyena_out"
        else:
            y = _attention(u)
            gate_block = 1
            name = "attn_out"
        if i + 1 < DEPTH:
            x, u = _layer_boundary(y, u, gate_block, x, out_params(i), in_params(i + 1), rope,
                                   name + "_next_in")
        else:
            x = _out_proj_ln(y, u, gate_block, x, *out_params(i), name + "_ln")
    return _from_class_major(x, r)


def kernel(x_prompt, x_sample, hy_w_in, hy_b_in, hy_conv_w, hy_conv_b, hy_f_w1, hy_f_b1, hy_f_freq, hy_f_w2,
           hy_f_b2, hy_f_w3, hy_decay, hy_filt_bias, hy_w_out, hy_b_out, at_w_in, at_b_in, at_q_norm, at_k_norm,
           at_w_out, at_b_out, ln_g, ln_b):
    p = dict(hy_w_in=hy_w_in, hy_b_in=hy_b_in, hy_conv_w=hy_conv_w, hy_conv_b=hy_conv_b, hy_f_w1=hy_f_w1,
             hy_f_b1=hy_f_b1, hy_f_freq=hy_f_freq, hy_f_w2=hy_f_w2, hy_f_b2=hy_f_b2, hy_f_w3=hy_f_w3,
             hy_decay=hy_decay, hy_filt_bias=hy_filt_bias, hy_w_out=hy_w_out, hy_b_out=hy_b_out,
             at_w_in=at_w_in, at_b_in=at_b_in, at_q_norm=at_q_norm, at_k_norm=at_k_norm, at_w_out=at_w_out,
             at_b_out=at_b_out, ln_g=ln_g, ln_b=ln_b)
    return (_trunk(x_prompt, p), _trunk(x_sample, p))
```

```python
import cmath
import functools
import math

import jax
import jax.numpy as jnp
from jax import lax
from jax.experimental import pallas as pl
from jax.experimental.pallas import tpu as pltpu

F32 = jnp.float32
BF16 = jnp.bfloat16

D_MODEL = 1024
DEPTH = 4
GRID_W = 64
HY_WIDTH = D_MODEL
HY_ORDER = 2
POS_EMB_DIM = 33
POS_BANDS = (POS_EMB_DIM - 1) // 2
FILTER_WIDTH = 64
FILTER_EPS = 1e-6
HEAD_DIM = 128
N_Q_HEADS = D_MODEL // HEAD_DIM
N_KV_HEADS = 2
GROUP = N_Q_HEADS // N_KV_HEADS
ATTN_WIDTH = N_Q_HEADS * HEAD_DIM
KV_WIDTH = N_KV_HEADS * HEAD_DIM
AXIS_DIM = HEAD_DIM // 2
ROPE_THETA = 10000.0
RMS_EPS = 1e-6
LN_EPS = 1e-5
DEEPNORM_ALPHA = (2.0 * DEPTH) ** 0.25

V7X_LANES = 128
V7X_SUBLANES = 8
V7X_MXU_DIM = 256
V7X_VMEM_BYTES = 64 * 1024 * 1024

CLASS_ROWS = V7X_MXU_DIM
CONV_TC = V7X_MXU_DIM
SPEC_ROWS = 2 * V7X_SUBLANES
SPEC_CHUNK = 64
SPEC_UNROLL_MAX_CLASSES = 8
ATTN_TQ = 512
OUT_TM = 512
BOUNDARY_SUB_ROWS = 256
HYENA_TM = 512
FILT_PAD_K = 64
BOUND_SLACK = 1.02
MIN_SAFE_DENOM = 2.0 ** -90


def _cparams(n_axes, vmem_mb):
    return pltpu.CompilerParams(
        dimension_semantics=("arbitrary",) * n_axes,
        vmem_limit_bytes=vmem_mb * 1024 * 1024,
    )


def _const_spec(block, index_map):
    return pl.BlockSpec(block, index_map, pipeline_mode=pl.Buffered(1))


def _add(a, b):
    if a is None:
        return b
    if b is None:
        return a
    return a + b


def _sub(a, b):
    if b is None:
        return a
    if a is None:
        return -b
    return a - b


def _scale(a, c):
    return None if a is None else a * c


def _twiddle(x, w):
    re, im = x
    c, s = w.real, w.imag
    if abs(s) < 1e-12:
        return (re, im) if c > 0 else (_sub(None, re), _sub(None, im))
    if abs(c) < 1e-12:
        return (_sub(None, im), re) if s > 0 else (im, _sub(None, re))
    if abs(abs(c) - abs(s)) < 1e-12:
        a = abs(c)
        total = _add(re, im)
        if c > 0 and s > 0:
            return (_scale(_sub(re, im), a), _scale(total, a))
        if c > 0:
            return (_scale(total, a), _scale(_sub(im, re), a))
        if s > 0:
            return (_scale(total, -a), _scale(_sub(re, im), a))
        return (_scale(_sub(im, re), a), _scale(total, -a))
    nre = _sub(_scale(re, c), _scale(im, s))
    nim = _add(_scale(re, s), _scale(im, c))
    return (nre, nim)


def _fft(xs, sign, first_half_only=False):
    n = len(xs)
    if n == 1:
        return list(xs)
    ev = _fft(xs[0::2], sign)
    od = _fft(xs[1::2], sign)
    out = [None] * n
    for k in range(n // 2):
        t = _twiddle(od[k], cmath.exp(sign * 2j * math.pi * k / n))
        out[k] = (_add(ev[k][0], t[0]), _add(ev[k][1], t[1]))
        if not first_half_only:
            out[k + n // 2] = (_sub(ev[k][0], t[0]), _sub(ev[k][1], t[1]))
    return out[: n // 2] if first_half_only else out


def _store_col_tiles(o_ref, rows, acc):
    for c in range(acc.shape[1] // CONV_TC):
        o_ref[0, c, rows, :] = acc[:, c * CONV_TC:(c + 1) * CONV_TC].astype(o_ref.dtype)


def _load_col_tiles(ref, rows):
    return jnp.concatenate([ref[0, c, rows, :] for c in range(ref.shape[1])], axis=1)


def _tile_major_shape(B, L, n):
    return (B, n // CONV_TC, L, CONV_TC)


def _proj_kernel(x_ref, w_ref, b_ref, o_ref):
    x = x_ref[0].astype(BF16)
    acc = jnp.dot(x, w_ref[...], preferred_element_type=F32) + b_ref[...]
    _store_col_tiles(o_ref, slice(None), acc)


def _hyena_proj(x, w, b):
    B, L, _ = x.shape
    n_out = w.shape[1]
    tm = HYENA_TM
    return pl.pallas_call(
        _proj_kernel,
        grid=(B, L // tm),
        in_specs=[
            pl.BlockSpec((1, tm, D_MODEL), lambda b, t: (b, t, 0)),
            _const_spec((D_MODEL, n_out), lambda b, t: (0, 0)),
            _const_spec((1, n_out), lambda b, t: (0, 0)),
        ],
        out_specs=pl.BlockSpec((1, n_out // CONV_TC, tm, CONV_TC), lambda b, t: (b, 0, t, 0)),
        out_shape=jax.ShapeDtypeStruct(_tile_major_shape(B, L, n_out), BF16),
        compiler_params=_cparams(2, 48),
        name="hyena_in_proj",
    )(x, w, b)


def _attn_in_epilogue(acc, q_gain, k_gain, cos2, sin2, store):
    def head(col, gain, scale):
        xh = acc[:, col:col + HEAD_DIM]
        ms = jnp.mean(xh * xh, axis=-1, keepdims=True)
        xn = xh * lax.rsqrt(ms + RMS_EPS) * gain
        rot = xn * cos2 + pltpu.roll(xn, AXIS_DIM, axis=1) * sin2
        store(col, rot * scale if scale != 1.0 else rot)

    for h in range(N_Q_HEADS):
        head(h * HEAD_DIM, q_gain, HEAD_DIM ** -0.5 * math.log2(math.e))
    store(ATTN_WIDTH, acc[:, ATTN_WIDTH:2 * ATTN_WIDTH])
    for h in range(N_KV_HEADS):
        head(2 * ATTN_WIDTH + h * HEAD_DIM, k_gain, 1.0)
    v0 = 2 * ATTN_WIDTH + KV_WIDTH
    store(v0, acc[:, v0:v0 + KV_WIDTH])


def _gated_out_ln(y, g, x, w_ref, b_ref, lg_ref, lb_ref):
    y = y.astype(F32)
    g = g.astype(F32)
    yg = (y * (g * jax.nn.sigmoid(g))).astype(BF16)
    s = jnp.dot(yg, w_ref[...], preferred_element_type=F32) + b_ref[...]
    v = DEEPNORM_ALPHA * x + s
    mu = jnp.mean(v, axis=-1, keepdims=True)
    vc = v - mu
    var = jnp.mean(vc * vc, axis=-1, keepdims=True)
    return vc * lax.rsqrt(var + LN_EPS) * lg_ref[...] + lb_ref[...]


def _boundary_kernel(*refs, attn_next, sub_rows):
    y_ref, g_ref, x_ref, wo_ref, bo_ref, lg_ref, lb_ref, wi_ref, bi_ref = refs[:9]
    if attn_next:
        qn_ref, kn_ref, cos_ref, sin_ref, xo_ref, u_ref = refs[9:]
    else:
        xo_ref, u_ref = refs[9:]
    sub_tiles = [slice(r0, r0 + sub_rows) for r0 in range(0, x_ref.shape[1], sub_rows)]
    normed = []
    for rows in sub_tiles:
        if attn_next:
            y, g = _load_col_tiles(y_ref, rows), _load_col_tiles(g_ref, rows)
        else:
            y, g = y_ref[0, rows, :], g_ref[0, rows, :]
        xn = _gated_out_ln(y, g, x_ref[0, rows, :], wo_ref, bo_ref, lg_ref, lb_ref)
        xo_ref[0, rows, :] = xn
        normed.append(xn.astype(BF16))
    for rows, xb in zip(sub_tiles, normed):
        acc = jnp.dot(xb, wi_ref[...], preferred_element_type=F32) + bi_ref[...]
        if attn_next:
            def store(col, val, rows=rows):
                u_ref[0, rows, col:col + val.shape[1]] = val.astype(u_ref.dtype)

            _attn_in_epilogue(acc, qn_ref[...], kn_ref[...], cos_ref[rows, :], sin_ref[rows, :], store)
        else:
            _store_col_tiles(u_ref, rows, acc)


def _layer_boundary(y, u, gate_block, x, out_params, in_params, rope, name):
    B, L, _ = x.shape
    attn_next = len(in_params) == 4
    n_next = in_params[0].shape[1]
    tm = 2 * OUT_TM if attn_next else OUT_TM
    x_spec = pl.BlockSpec((1, tm, D_MODEL), lambda b_, t: (b_, t, 0))
    zero = lambda b_, t: (0, 0)
    if attn_next:
        tiles = D_MODEL // CONV_TC
        y_spec = pl.BlockSpec((1, tiles, tm, CONV_TC), lambda b_, t: (b_, 0, t, 0))
        g_spec = pl.BlockSpec((1, tiles, tm, CONV_TC), lambda b_, t: (b_, gate_block, t, 0))
        u_spec = pl.BlockSpec((1, tm, n_next), lambda b_, t: (b_, t, 0))
        u_shape = (B, L, n_next)
    else:
        y_spec = x_spec
        g_spec = pl.BlockSpec((1, tm, D_MODEL), lambda b_, t: (b_, t, gate_block))
        u_spec = pl.BlockSpec((1, n_next // CONV_TC, tm, CONV_TC), lambda b_, t: (b_, 0, t, 0))
        u_shape = _tile_major_shape(B, L, n_next)
    in_specs = [
        y_spec,
        g_spec,
        x_spec,
        _const_spec((D_MODEL, D_MODEL), zero),
        _const_spec((1, D_MODEL), zero),
        _const_spec((1, D_MODEL), zero),
        _const_spec((1, D_MODEL), zero),
        _const_spec((D_MODEL, n_next), zero),
        _const_spec((1, n_next), zero),
    ]
    args = [y, u, x, *out_params, *in_params]
    if attn_next:
        in_specs += [
            _const_spec((1, HEAD_DIM), zero),
            _const_spec((1, HEAD_DIM), zero),
            pl.BlockSpec((tm, HEAD_DIM), lambda b_, t: (t, 0)),
            pl.BlockSpec((tm, HEAD_DIM), lambda b_, t: (t, 0)),
        ]
        args += list(rope)
    return pl.pallas_call(
        functools.partial(_boundary_kernel, attn_next=attn_next, sub_rows=BOUNDARY_SUB_ROWS),
        grid=(B, L // tm),
        in_specs=in_specs,
        out_specs=[x_spec, u_spec],
        out_shape=[jax.ShapeDtypeStruct(x.shape, F32), jax.ShapeDtypeStruct(u_shape, BF16)],
        compiler_params=_cparams(2, 56),
        name=name,
    )(*args)


def _out_kernel(y_ref, g_ref, x_ref, w_ref, b_ref, lg_ref, lb_ref, o_ref):
    o_ref[0] = _gated_out_ln(y_ref[0], g_ref[0], x_ref[0], w_ref, b_ref, lg_ref, lb_ref)


def _out_proj_ln(y, u, gate_block, x, w, b, lg, lb, name):
    B, L, _ = x.shape
    tm = OUT_TM
    x_spec = pl.BlockSpec((1, tm, D_MODEL), lambda b_, t: (b_, t, 0))
    zero = lambda b_, t: (0, 0)
    return pl.pallas_call(
        _out_kernel,
        grid=(B, L // tm),
        in_specs=[
            x_spec,
            pl.BlockSpec((1, tm, D_MODEL), lambda b_, t: (b_, t, gate_block)),
            x_spec,
            _const_spec((D_MODEL, D_MODEL), zero),
            _const_spec((1, D_MODEL), zero),
            _const_spec((1, D_MODEL), zero),
            _const_spec((1, D_MODEL), zero),
        ],
        out_specs=x_spec,
        out_shape=jax.ShapeDtypeStruct(x.shape, F32),
        compiler_params=_cparams(2, 40),
        name=name,
    )(y, u, x, w, b, lg, lb)


def _filter_kernel(feat_ref, w1_ref, b1_ref, fr_ref, w2_ref, b2_ref, w3f_ref, w3b_ref,
                   dcf_ref, dcb_ref, hf_ref, hb_ref, h_s):
    hi = lax.Precision.HIGHEST

    @pl.when((pl.program_id(0) == 0) & (pl.program_id(1) == 0))
    def _():
        fr = fr_ref[...]
        h = jnp.sin(fr * (jnp.dot(feat_ref[...], w1_ref[...], precision=hi, preferred_element_type=F32)
                          + b1_ref[...]))
        h = jnp.sin(fr * (jnp.dot(h, w2_ref[...], precision=hi, preferred_element_type=F32)
                          + b2_ref[...]))
        h_hi = h.astype(BF16)
        h_lo = (h - h_hi.astype(F32)).astype(BF16)
        w = FILTER_WIDTH
        h_s[:, 0:w] = h_hi
        h_s[:, w:2 * w] = h_hi
        h_s[:, 2 * w:3 * w] = h_lo
        h_s[:, 3 * w:4 * w] = jnp.zeros_like(h_lo)

    def last_layer(w_ref):
        w = w_ref[...]
        w_hi = w.astype(BF16)
        w_lo = (w - w_hi.astype(F32)).astype(BF16)
        w_cat = jnp.concatenate([w_hi, w_lo, w_hi, jnp.zeros_like(w_hi)], axis=0)
        return jnp.dot(h_s[...], w_cat, preferred_element_type=F32)

    t = feat_ref[:, 0:1]
    hf = last_layer(w3f_ref) * jnp.exp(-t * jnp.abs(dcf_ref[...]))
    hb = last_layer(w3b_ref) * jnp.exp(-t * jnp.abs(dcb_ref[...]))
    ss = jnp.sum(hf * hf + hb * hb, axis=0, keepdims=True)
    sc = lax.rsqrt(ss + FILTER_EPS)
    hf_ref[...] = hf * sc
    row = lax.broadcasted_iota(jnp.int32, hb.shape, 0)
    hb_ref[...] = jnp.where(row == 0, 0.0, hb * sc)


def _hyena_filters(feats, w1p, b1, fr, w2, b2, w3, decay):
    L = feats.shape[0]
    E = HY_WIDTH
    tc = CONV_TC
    nct = E // tc
    zero = lambda o, c: (0, 0)
    col = lambda d: (lambda o, c: (0, (2 * o + d) * nct + c))
    out_spec = pl.BlockSpec((L, tc), lambda o, c: (0, o * nct + c))
    return pl.pallas_call(
        _filter_kernel,
        grid=(HY_ORDER, nct),
        in_specs=[
            _const_spec((L, FILT_PAD_K), zero),
            _const_spec((FILT_PAD_K, FILTER_WIDTH), zero),
            _const_spec((1, FILTER_WIDTH), zero),
            _const_spec((1, FILTER_WIDTH), zero),
            _const_spec((FILTER_WIDTH, FILTER_WIDTH), zero),
            _const_spec((1, FILTER_WIDTH), zero),
            pl.BlockSpec((FILTER_WIDTH, tc), col(0)),
            pl.BlockSpec((FILTER_WIDTH, tc), col(1)),
            pl.BlockSpec((1, tc), col(0)),
            pl.BlockSpec((1, tc), col(1)),
        ],
        out_specs=[out_spec, out_spec],
        out_shape=[jax.ShapeDtypeStruct((L, HY_ORDER * E), F32)] * 2,
        scratch_shapes=[pltpu.VMEM((L, 4 * FILTER_WIDTH), BF16)],
        compiler_params=_cparams(2, 48),
        name="hyena_filter_taps",
    )(feats, w1p, b1, fr, w2, b2, w3, w3, decay, decay)


def _class_slices(ref, q, F, rows, lanes):
    return (ref[q, rows, lanes], ref[q, pl.ds(rows.start + F, rows.size), lanes])


def _kf_kernel(hf_ref, hb_ref, tf_ref, kf_ref, pa_s, pb_s, *, r, F, tc, inv_len):
    def fwd(src_ref, dst_ref):
        for q in range(r):
            h = src_ref[q * F:(q + 1) * F, :].astype(BF16)
            dst_ref[q] = jnp.dot(tf_ref[q], h, preferred_element_type=F32)

    fwd(hf_ref, pa_s)
    fwd(hb_ref, pb_s)

    def body(i, carry):
        rows = pl.ds(pl.multiple_of(i * SPEC_ROWS, SPEC_ROWS), SPEC_ROWS)
        rows_im = pl.ds(rows.start + F, rows.size)
        for lc in range(tc // V7X_LANES):
            lanes = slice(lc * V7X_LANES, (lc + 1) * V7X_LANES)
            xa = _fft([_class_slices(pa_s, q, F, rows, lanes) for q in range(r)], -1.0)
            xb = _fft([_class_slices(pb_s, q, F, rows, lanes) for q in range(r)], -1.0)
            for j in range(r):
                kf_ref[0, j, rows, lanes] = ((xa[j][0] + xb[j][0]) * inv_len).astype(kf_ref.dtype)
                kf_ref[0, j, rows_im, lanes] = ((xa[j][1] - xb[j][1]) * inv_len).astype(kf_ref.dtype)
        return carry

    lax.fori_loop(0, F // SPEC_ROWS, body, 0)


def _filter_spectrum(hf, hb, tf, r):
    L = hf.shape[0]
    F = L // r
    E = HY_WIDTH
    tc = CONV_TC
    nct = E // tc
    kern = functools.partial(_kf_kernel, r=r, F=F, tc=tc, inv_len=1.0 / L)
    return pl.pallas_call(
        kern,
        grid=(HY_ORDER, nct),
        in_specs=[
            pl.BlockSpec((L, tc), lambda o, c: (0, o * nct + c)),
            pl.BlockSpec((L, tc), lambda o, c: (0, o * nct + c)),
            _const_spec((r, 2 * F, F), lambda o, c: (0, 0, 0)),
        ],
        out_specs=pl.BlockSpec((1, r, 2 * F, tc), lambda o, c: (o, 0, 0, c)),
        out_shape=jax.ShapeDtypeStruct((HY_ORDER, r, 2 * F, E), BF16),
        scratch_shapes=[pltpu.VMEM((r, 2 * F, tc), F32), pltpu.VMEM((r, 2 * F, tc), F32)],
        compiler_params=_cparams(2, 56),
        name="hyena_filter_spectrum",
    )(hf, hb, tf)


def _short_conv_classes(u_ref, w, b, r, F):
    def cls(k):
        return u_ref[k * F:(k + 1) * F, :].astype(F32)

    first = cls(0)
    row = lax.broadcasted_iota(jnp.int32, first.shape, 0)
    prev = jnp.where(row == 0, 0.0, pltpu.roll(cls(r - 1), 1, axis=0))
    cur = first
    for q in range(r):
        if q < r - 1:
            nxt = cls(q + 1)
        else:
            nxt = jnp.where(row == F - 1, 0.0, pltpu.roll(first, F - 1, axis=0))
        yield q, prev * w[0:1] + cur * w[1:2] + nxt * w[2:3] + b
        prev, cur = cur, nxt


def _conv_kernel(uv_ref, ug0_ref, ug1_ref, kf_ref, tf_ref, ti_ref, cw_ref, cb_ref, fb_ref,
                 out_ref, z_s, p_s, a_s, *, r, F, tc, straight_line):
    for q, v in _short_conv_classes(uv_ref, cw_ref[0], cb_ref[0:1, :], r, F):
        z_s[q] = v

    gate_refs = (ug0_ref, ug1_ref)
    for o in range(HY_ORDER):
        def spectral(rows, rows_im, o=o):
            for lc in range(tc // V7X_LANES):
                lanes = slice(lc * V7X_LANES, (lc + 1) * V7X_LANES)
                x = _fft([(p_s[q, rows, lanes], p_s[q, rows_im, lanes]) for q in range(r)], -1.0)
                y = []
                for j in range(r):
                    kre = kf_ref[o, j, rows, lanes]
                    kim = kf_ref[o, j, rows_im, lanes]
                    xre, xim = x[j]
                    y.append((xre * kre - xim * kim, xre * kim + xim * kre))
                a = _fft(y, 1.0)
                for q in range(r):
                    a_s[q, rows, lanes] = a[q][0]
                    a_s[q, rows_im, lanes] = a[q][1]

        if straight_line:
            z_bf = [z_s[q].astype(BF16) for q in range(r)]
            for c0 in range(0, F, SPEC_CHUNK):
                re_rows = slice(c0, c0 + SPEC_CHUNK)
                im_rows = slice(F + c0, F + c0 + SPEC_CHUNK)
                for q in range(r):
                    table = jnp.concatenate([tf_ref[q, re_rows, :], tf_ref[q, im_rows, :]], axis=0)
                    d = jnp.dot(table, z_bf[q], preferred_element_type=F32).astype(p_s.dtype)
                    p_s[q, re_rows, :] = d[:SPEC_CHUNK]
                    p_s[q, im_rows, :] = d[SPEC_CHUNK:]
                for g0 in range(c0, c0 + SPEC_CHUNK, SPEC_ROWS):
                    spectral(slice(g0, g0 + SPEC_ROWS), slice(F + g0, F + g0 + SPEC_ROWS))
        else:
            for q in range(r):
                p_s[q] = jnp.dot(tf_ref[q], z_s[q].astype(BF16),
                                 preferred_element_type=F32).astype(p_s.dtype)

            def body(i, carry):
                rows = pl.ds(pl.multiple_of(i * SPEC_ROWS, SPEC_ROWS), SPEC_ROWS)
                spectral(rows, pl.ds(rows.start + F, rows.size))
                return carry

            lax.fori_loop(0, F // SPEC_ROWS, body, 0)

        gates = _short_conv_classes(gate_refs[o], cw_ref[1 + o], cb_ref[1 + o:2 + o, :], r, F)
        for q, g in gates:
            y = jnp.dot(ti_ref[q], a_s[q], preferred_element_type=F32)
            z_new = g * (y + z_s[q] * fb_ref[o:o + 1, :])
            if o == HY_ORDER - 1:
                out_ref[q * F:(q + 1) * F, :] = z_new.astype(out_ref.dtype)
            else:
                z_s[q] = z_new


def _hyena_conv(u, kf, tf, ti, cw, cb, fb, r):
    B, _, L, _ = u.shape
    F = L // r
    E = HY_WIDTH
    tc = CONV_TC
    nct = E // tc
    kern = functools.partial(_conv_kernel, r=r, F=F, tc=tc, straight_line=r <= SPEC_UNROLL_MAX_CLASSES)
    sec = lambda s: pl.BlockSpec((None, None, L, tc), lambda c, b: (b, s * nct + c, 0, 0))
    return pl.pallas_call(
        kern,
        grid=(nct, B),
        in_specs=[
            sec(0), sec(1), sec(2),
            _const_spec((HY_ORDER, r, 2 * F, tc), lambda c, b: (0, 0, 0, c)),
            _const_spec((r, 2 * F, F), lambda c, b: (0, 0, 0)),
            _const_spec((r, F, 2 * F), lambda c, b: (0, 0, 0)),
            _const_spec((3, 3, tc), lambda c, b: (0, 0, c)),
            _const_spec((3, tc), lambda c, b: (0, c)),
            _const_spec((HY_ORDER, tc), lambda c, b: (0, c)),
        ],
        out_specs=pl.BlockSpec((None, None, L, tc), lambda c, b: (b, c, 0, 0)),
        out_shape=jax.ShapeDtypeStruct(_tile_major_shape(B, L, E), BF16),
        scratch_shapes=[pltpu.VMEM((r, F, tc), F32), pltpu.VMEM((r, 2 * F, tc), BF16),
                        pltpu.VMEM((r, 2 * F, tc), BF16)],
        compiler_params=_cparams(2, 56),
        name="hyena_long_conv",
    )(u, u, u, kf, tf, ti, cw, cb, fb)


def _attn_kernel(q_ref, k_ref, v_ref, o_ref, vt_s, kmax_s, p_s):
    @pl.when(pl.program_id(2) == 0)
    def _():
        vt_s[...] = v_ref[0].T
        kf = k_ref[0].astype(F32)
        row_sq = jnp.sum(kf * kf, axis=1, keepdims=True)
        kmax_s[...] = jnp.broadcast_to(jnp.max(row_sq, axis=0, keepdims=True), kmax_s.shape)

    k = k_ref[0]
    dims_nt = (((1,), (1,)), ((), ()))

    def q_head(g):
        return q_ref[0, :, g * HEAD_DIM:(g + 1) * HEAD_DIM]

    def finish(g, p_bf16, denom):
        ot = jnp.dot(vt_s[...], p_bf16, preferred_element_type=F32) * (1.0 / denom)
        o_ref[0, :, g * HEAD_DIM:(g + 1) * HEAD_DIM] = ot.T.astype(o_ref.dtype)

    ones = jnp.ones((V7X_SUBLANES, HEAD_DIM), BF16)
    denoms = []
    for g in range(GROUP):
        q = q_head(g)
        qf = q.astype(F32)
        q_sq = lax.dot_general(ones, (qf * qf).astype(BF16), dims_nt, preferred_element_type=F32)
        bound = BOUND_SLACK * jnp.sqrt(q_sq[0:1] * kmax_s[0:1])
        st = lax.dot_general(k, q, dims_nt, preferred_element_type=F32)
        p = jnp.exp2(st - bound)
        denoms.append(jnp.sum(p, axis=0, keepdims=True))
        p_s[g] = p.astype(BF16)
    for g in range(GROUP):
        finish(g, p_s[g], denoms[g])

    smallest = jnp.min(jnp.minimum(jnp.minimum(denoms[0], denoms[1]), jnp.minimum(denoms[2], denoms[3])))

    @pl.when(jnp.logical_not(smallest >= MIN_SAFE_DENOM))
    def _():
        for g in range(GROUP):
            st = lax.dot_general(k, q_head(g), dims_nt, preferred_element_type=F32)
            p = jnp.exp2(st - jnp.max(st, axis=0, keepdims=True))
            finish(g, p.astype(BF16), jnp.sum(p, axis=0, keepdims=True))


def _attention(u):
    B, L, _ = u.shape
    tq = ATTN_TQ
    gw = GROUP * HEAD_DIM
    k_block = 2 * ATTN_WIDTH // HEAD_DIM
    v_block = (2 * ATTN_WIDTH + KV_WIDTH) // HEAD_DIM
    return pl.pallas_call(
        _attn_kernel,
        grid=(B, N_KV_HEADS, L // tq),
        in_specs=[
            pl.BlockSpec((1, tq, gw), lambda b, h, t: (b, t, h)),
            pl.BlockSpec((1, L, HEAD_DIM), lambda b, h, t: (b, 0, k_block + h)),
            pl.BlockSpec((1, L, HEAD_DIM), lambda b, h, t: (b, 0, v_block + h)),
        ],
        out_specs=pl.BlockSpec((1, tq, gw), lambda b, h, t: (b, t, h)),
        out_shape=jax.ShapeDtypeStruct((B, L, ATTN_WIDTH), BF16),
        scratch_shapes=[pltpu.VMEM((HEAD_DIM, L), BF16),
                        pltpu.VMEM((V7X_SUBLANES, tq), F32),
                        pltpu.VMEM((GROUP, L, tq), BF16)],
        compiler_params=_cparams(3, 56),
        name="gqa_attention",
    )(u, u, u)


def _class_dft_tables(L, r):
    F = L // r
    f = jnp.arange(F, dtype=jnp.int32)
    m = jnp.arange(F, dtype=jnp.int32)
    q = jnp.arange(r, dtype=jnp.int32)
    n = ((2 * f + 1)[None, :, None] * (r * m[None, None, :] + q[:, None, None])) % (4 * L)
    theta = n.astype(F32) * (math.pi / (2 * L))
    c = jnp.cos(theta)
    s = jnp.sin(theta)
    tf = jnp.concatenate([c, -s], axis=1).astype(BF16)
    ti = jnp.concatenate([c.transpose(0, 2, 1), -s.transpose(0, 2, 1)], axis=2).astype(BF16)
    return tf, ti


def _filter_features(L):
    t = jnp.linspace(0.0, 1.0, L, dtype=F32)[:, None]
    w = 2.0 * math.pi * jnp.arange(L, dtype=F32)[:, None] / L
    f = jnp.linspace(1e-4, POS_BANDS - 1, POS_BANDS, dtype=F32)[None]
    feats = jnp.concatenate([t, jnp.cos(f * w), -jnp.sin(f * w)], axis=-1)
    return jnp.pad(feats, ((0, 0), (0, FILT_PAD_K - POS_EMB_DIM)))


def _rope_tables(L):
    rows = L // GRID_W
    row = jnp.repeat(jnp.arange(rows, dtype=F32), GRID_W)
    col = jnp.tile(jnp.arange(GRID_W, dtype=F32), rows)
    inv = ROPE_THETA ** (-jnp.arange(0, AXIS_DIM, 2, dtype=F32) / AXIS_DIM)
    ang = jnp.concatenate([row[:, None] * inv, col[:, None] * inv], axis=-1)
    c, s = jnp.cos(ang), jnp.sin(ang)
    return jnp.concatenate([c, c], axis=-1), jnp.concatenate([-s, s], axis=-1)


def _to_class_major(a, r):
    lead, L, C = a.shape[:-2], a.shape[-2], a.shape[-1]
    return a.reshape(lead + (L // r, r, C)).swapaxes(-3, -2).reshape(a.shape)


def _from_class_major(a, r):
    lead, L, C = a.shape[:-2], a.shape[-2], a.shape[-1]
    return a.reshape(lead + (r, L // r, C)).swapaxes(-3, -2).reshape(a.shape)


def _attn_in_params(p, j):
    w = p["at_w_in"][j]
    bi = p["at_b_in"][j]
    kv0 = ATTN_WIDTH
    g0 = ATTN_WIDTH + 2 * KV_WIDTH
    w = jnp.concatenate([w[:, :kv0], w[:, g0:], w[:, kv0:g0]], axis=1).astype(BF16)
    bi = jnp.concatenate([bi[:kv0], bi[g0:], bi[kv0:g0]])[None]
    return (w, bi, p["at_q_norm"][j][None], p["at_k_norm"][j][None])


def _trunk(x, p):
    B, L, _ = x.shape
    r = L // CLASS_ROWS
    tf, ti = _class_dft_tables(L, r)
    feats = _to_class_major(_filter_features(L), r)
    rope = tuple(_to_class_major(t, r) for t in _rope_tables(L))
    x = _to_class_major(x, r)

    def in_params(i):
        j = i // 2
        if i % 2 == 0:
            return (p["hy_w_in"][j].astype(BF16), p["hy_b_in"][j][None])
        return _attn_in_params(p, j)

    def out_params(i):
        j = i // 2
        w, b = (p["hy_w_out"], p["hy_b_out"]) if i % 2 == 0 else (p["at_w_out"], p["at_b_out"])
        return (w[j].astype(BF16), b[j][None], p["ln_g"][i][None], p["ln_b"][i][None])

    u = _hyena_proj(x, *in_params(0))
    for i in range(DEPTH):
        j = i // 2
        if i % 2 == 0:
            w1p = jnp.pad(p["hy_f_w1"][j], ((0, FILT_PAD_K - POS_EMB_DIM), (0, 0)))
            hf, hb = _hyena_filters(feats, w1p, p["hy_f_b1"][j][None], p["hy_f_freq"][j][None],
                                    p["hy_f_w2"][j], p["hy_f_b2"][j][None], p["hy_f_w3"][j],
                                    p["hy_decay"][j][None])
            kf = _filter_spectrum(hf, hb, tf, r)
            cw = p["hy_conv_w"][j].reshape(3, 3, HY_WIDTH).transpose(1, 0, 2)
            cb = p["hy_conv_b"][j].reshape(3, HY_WIDTH)
            y = _hyena_conv(u, kf, tf, ti, cw, cb, p["hy_filt_bias"][j], r)
            gate_block = 3 * HY_WIDTH // D_MODEL
            name = "hyena_out"
        else:
            y = _attention(u)
            gate_block = 1
            name = "attn_out"
        if i + 1 < DEPTH:
            x, u = _layer_boundary(y, u, gate_block, x, out_params(i), in_params(i + 1), rope,
                                   name + "_next_in")
        else:
            x = _out_proj_ln(y, u, gate_block, x, *out_params(i), name + "_ln")
    return _from_class_major(x, r)


def kernel(x_prompt, x_sample, hy_w_in, hy_b_in, hy_conv_w, hy_conv_b, hy_f_w1, hy_f_b1, hy_f_freq, hy_f_w2,
           hy_f_b2, hy_f_w3, hy_decay, hy_filt_bias, hy_w_out, hy_b_out, at_w_in, at_b_in, at_q_norm, at_k_norm,
           at_w_out, at_b_out, ln_g, ln_b):
    p = dict(hy_w_in=hy_w_in, hy_b_in=hy_b_in, hy_conv_w=hy_conv_w, hy_conv_b=hy_conv_b, hy_f_w1=hy_f_w1,
             hy_f_b1=hy_f_b1, hy_f_freq=hy_f_freq, hy_f_w2=hy_f_w2, hy_f_b2=hy_f_b2, hy_f_w3=hy_f_w3,
             hy_decay=hy_decay, hy_filt_bias=hy_filt_bias, hy_w_out=hy_w_out, hy_b_out=hy_b_out,
             at_w_in=at_w_in, at_b_in=at_b_in, at_q_norm=at_q_norm, at_k_norm=at_k_norm, at_w_out=at_w_out,
             at_b_out=at_b_out, ln_g=ln_g, ln_b=ln_b)
    return (_trunk(x_prompt, p), _trunk(x_sample, p))
```

```python
import cmath
import functools
import math

import jax
import jax.numpy as jnp
from jax import lax
from jax.experimental import pallas as pl
from jax.experimental.pallas import tpu as pltpu

F32 = jnp.float32
BF16 = jnp.bfloat16

D_MODEL = 1024
DEPTH = 4
GRID_W = 64
HY_WIDTH = D_MODEL
HY_ORDER = 2
POS_EMB_DIM = 33
POS_BANDS = (POS_EMB_DIM - 1) // 2
FILTER_WIDTH = 64
FILTER_EPS = 1e-6
HEAD_DIM = 128
N_Q_HEADS = D_MODEL // HEAD_DIM
N_KV_HEADS = 2
GROUP = N_Q_HEADS // N_KV_HEADS
ATTN_WIDTH = N_Q_HEADS * HEAD_DIM
KV_WIDTH = N_KV_HEADS * HEAD_DIM
AXIS_DIM = HEAD_DIM // 2
ROPE_THETA = 10000.0
RMS_EPS = 1e-6
LN_EPS = 1e-5
DEEPNORM_ALPHA = (2.0 * DEPTH) ** 0.25

V7X_LANES = 128
V7X_SUBLANES = 8
V7X_MXU_DIM = 256
V7X_VMEM_BYTES = 64 * 1024 * 1024

CLASS_ROWS = V7X_MXU_DIM
CONV_TC = V7X_MXU_DIM
SPEC_ROWS = 2 * V7X_SUBLANES
ATTN_TILE_ELEMS = 512 * 4096
OUT_TM = 512
BOUNDARY_SUB_ROWS = 256
HYENA_TM = 1024
FILT_PAD_K = 64
BOUND_SLACK = 1.02
MIN_SAFE_DENOM = 2.0 ** -90


def _cparams(n_axes, vmem_mb):
    return pltpu.CompilerParams(
        dimension_semantics=("arbitrary",) * n_axes,
        vmem_limit_bytes=vmem_mb * 1024 * 1024,
    )


def _const_spec(block, index_map):
    return pl.BlockSpec(block, index_map, pipeline_mode=pl.Buffered(1))


def _add(a, b):
    if a is None:
        return b
    if b is None:
        return a
    return a + b


def _sub(a, b):
    if b is None:
        return a
    if a is None:
        return -b
    return a - b


def _scale(a, c):
    return None if a is None else a * c


def _twiddle(x, w):
    re, im = x
    c, s = w.real, w.imag
    if abs(s) < 1e-12:
        return (re, im) if c > 0 else (_sub(None, re), _sub(None, im))
    if abs(c) < 1e-12:
        return (_sub(None, im), re) if s > 0 else (im, _sub(None, re))
    if abs(abs(c) - abs(s)) < 1e-12:
        a = abs(c)
        total = _add(re, im)
        if c > 0 and s > 0:
            return (_scale(_sub(re, im), a), _scale(total, a))
        if c > 0:
            return (_scale(total, a), _scale(_sub(im, re), a))
        if s > 0:
            return (_scale(total, -a), _scale(_sub(re, im), a))
        return (_scale(_sub(im, re), a), _scale(total, -a))
    nre = _sub(_scale(re, c), _scale(im, s))
    nim = _add(_scale(re, s), _scale(im, c))
    return (nre, nim)


def _fft(xs, sign, first_half_only=False):
    n = len(xs)
    if n == 1:
        return list(xs)
    ev = _fft(xs[0::2], sign)
    od = _fft(xs[1::2], sign)
    out = [None] * n
    for k in range(n // 2):
        t = _twiddle(od[k], cmath.exp(sign * 2j * math.pi * k / n))
        out[k] = (_add(ev[k][0], t[0]), _add(ev[k][1], t[1]))
        if not first_half_only:
            out[k + n // 2] = (_sub(ev[k][0], t[0]), _sub(ev[k][1], t[1]))
    return out[: n // 2] if first_half_only else out


def _store_col_tiles(o_ref, rows, acc):
    for c in range(acc.shape[1] // CONV_TC):
        o_ref[0, c, rows, :] = acc[:, c * CONV_TC:(c + 1) * CONV_TC].astype(o_ref.dtype)


def _load_col_tiles(ref, rows):
    return jnp.concatenate([ref[0, c, rows, :] for c in range(ref.shape[1])], axis=1)


def _tile_major_shape(B, L, n):
    return (B, n // CONV_TC, L, CONV_TC)


def _proj_kernel(x_ref, w_ref, b_ref, o_ref):
    x = x_ref[0].astype(BF16)
    acc = jnp.dot(x, w_ref[...], preferred_element_type=F32) + b_ref[...]
    _store_col_tiles(o_ref, slice(None), acc)


def _hyena_proj(x, w, b):
    B, L, _ = x.shape
    n_out = w.shape[1]
    tm = HYENA_TM
    return pl.pallas_call(
        _proj_kernel,
        grid=(B, L // tm),
        in_specs=[
            pl.BlockSpec((1, tm, D_MODEL), lambda b, t: (b, t, 0)),
            _const_spec((D_MODEL, n_out), lambda b, t: (0, 0)),
            _const_spec((1, n_out), lambda b, t: (0, 0)),
        ],
        out_specs=pl.BlockSpec((1, n_out // CONV_TC, tm, CONV_TC), lambda b, t: (b, 0, t, 0)),
        out_shape=jax.ShapeDtypeStruct(_tile_major_shape(B, L, n_out), BF16),
        compiler_params=_cparams(2, 48),
        name="hyena_in_proj",
    )(x, w, b)


def _attn_in_epilogue(acc, q_gain, k_gain, cos2, sin2, store):
    def head(col, gain, scale):
        xh = acc[:, col:col + HEAD_DIM]
        ms = jnp.mean(xh * xh, axis=-1, keepdims=True)
        xn = xh * lax.rsqrt(ms + RMS_EPS) * gain
        rot = xn * cos2 + pltpu.roll(xn, AXIS_DIM, axis=1) * sin2
        store(col, rot * scale if scale != 1.0 else rot)

    for h in range(N_Q_HEADS):
        head(h * HEAD_DIM, q_gain, HEAD_DIM ** -0.5 * math.log2(math.e))
    store(ATTN_WIDTH, acc[:, ATTN_WIDTH:2 * ATTN_WIDTH])
    for h in range(N_KV_HEADS):
        head(2 * ATTN_WIDTH + h * HEAD_DIM, k_gain, 1.0)
    v0 = 2 * ATTN_WIDTH + KV_WIDTH
    store(v0, acc[:, v0:v0 + KV_WIDTH])


def _gated_out_ln(y, g, x, w_ref, b_ref, lg_ref, lb_ref):
    y = y.astype(F32)
    g = g.astype(F32)
    yg = (y * (g * jax.nn.sigmoid(g))).astype(BF16)
    s = jnp.dot(yg, w_ref[...], preferred_element_type=F32) + b_ref[...]
    v = DEEPNORM_ALPHA * x + s
    mu = jnp.mean(v, axis=-1, keepdims=True)
    vc = v - mu
    var = jnp.mean(vc * vc, axis=-1, keepdims=True)
    return vc * lax.rsqrt(var + LN_EPS) * lg_ref[...] + lb_ref[...]


def _boundary_kernel(*refs, attn_next, sub_rows):
    y_ref, g_ref, x_ref, wo_ref, bo_ref, lg_ref, lb_ref, wi_ref, bi_ref = refs[:9]
    if attn_next:
        qn_ref, kn_ref, cos_ref, sin_ref, xo_ref, u_ref = refs[9:]
    else:
        xo_ref, u_ref = refs[9:]
    sub_tiles = [slice(r0, r0 + sub_rows) for r0 in range(0, x_ref.shape[1], sub_rows)]
    normed = []
    for rows in sub_tiles:
        if attn_next:
            y, g = _load_col_tiles(y_ref, rows), _load_col_tiles(g_ref, rows)
        else:
            y, g = y_ref[0, rows, :], g_ref[0, rows, :]
        xn = _gated_out_ln(y, g, x_ref[0, rows, :], wo_ref, bo_ref, lg_ref, lb_ref)
        xo_ref[0, rows, :] = xn
        normed.append(xn.astype(BF16))
    for rows, xb in zip(sub_tiles, normed):
        acc = jnp.dot(xb, wi_ref[...], preferred_element_type=F32) + bi_ref[...]
        if attn_next:
            def store(col, val, rows=rows):
                u_ref[0, rows, col:col + val.shape[1]] = val.astype(u_ref.dtype)

            _attn_in_epilogue(acc, qn_ref[...], kn_ref[...], cos_ref[rows, :], sin_ref[rows, :], store)
        else:
            _store_col_tiles(u_ref, rows, acc)


def _layer_boundary(y, u, gate_block, x, out_params, in_params, rope, name):
    B, L, _ = x.shape
    attn_next = len(in_params) == 4
    n_next = in_params[0].shape[1]
    tm = 2 * OUT_TM if attn_next else OUT_TM
    x_spec = pl.BlockSpec((1, tm, D_MODEL), lambda b_, t: (b_, t, 0))
    zero = lambda b_, t: (0, 0)
    if attn_next:
        tiles = D_MODEL // CONV_TC
        y_spec = pl.BlockSpec((1, tiles, tm, CONV_TC), lambda b_, t: (b_, 0, t, 0))
        g_spec = pl.BlockSpec((1, tiles, tm, CONV_TC), lambda b_, t: (b_, gate_block, t, 0))
        u_spec = pl.BlockSpec((1, tm, n_next), lambda b_, t: (b_, t, 0))
        u_shape = (B, L, n_next)
    else:
        y_spec = x_spec
        g_spec = pl.BlockSpec((1, tm, D_MODEL), lambda b_, t: (b_, t, gate_block))
        u_spec = pl.BlockSpec((1, n_next // CONV_TC, tm, CONV_TC), lambda b_, t: (b_, 0, t, 0))
        u_shape = _tile_major_shape(B, L, n_next)
    in_specs = [
        y_spec,
        g_spec,
        x_spec,
        _const_spec((D_MODEL, D_MODEL), zero),
        _const_spec((1, D_MODEL), zero),
        _const_spec((1, D_MODEL), zero),
        _const_spec((1, D_MODEL), zero),
        _const_spec((D_MODEL, n_next), zero),
        _const_spec((1, n_next), zero),
    ]
    args = [y, u, x, *out_params, *in_params]
    if attn_next:
        in_specs += [
            _const_spec((1, HEAD_DIM), zero),
            _const_spec((1, HEAD_DIM), zero),
            pl.BlockSpec((tm, HEAD_DIM), lambda b_, t: (t, 0)),
            pl.BlockSpec((tm, HEAD_DIM), lambda b_, t: (t, 0)),
        ]
        args += list(rope)
    return pl.pallas_call(
        functools.partial(_boundary_kernel, attn_next=attn_next, sub_rows=BOUNDARY_SUB_ROWS),
        grid=(B, L // tm),
        in_specs=in_specs,
        out_specs=[x_spec, u_spec],
        out_shape=[jax.ShapeDtypeStruct(x.shape, F32), jax.ShapeDtypeStruct(u_shape, BF16)],
        compiler_params=_cparams(2, 56),
        name=name,
    )(*args)


def _out_kernel(y_ref, g_ref, x_ref, w_ref, b_ref, lg_ref, lb_ref, o_ref):
    o_ref[0] = _gated_out_ln(y_ref[0], g_ref[0], x_ref[0], w_ref, b_ref, lg_ref, lb_ref)


def _out_proj_ln(y, u, gate_block, x, w, b, lg, lb, name):
    B, L, _ = x.shape
    tm = 2 * OUT_TM
    x_spec = pl.BlockSpec((1, tm, D_MODEL), lambda b_, t: (b_, t, 0))
    zero = lambda b_, t: (0, 0)
    return pl.pallas_call(
        _out_kernel,
        grid=(B, L // tm),
        in_specs=[
            x_spec,
            pl.BlockSpec((1, tm, D_MODEL), lambda b_, t: (b_, t, gate_block)),
            x_spec,
            _const_spec((D_MODEL, D_MODEL), zero),
            _const_spec((1, D_MODEL), zero),
            _const_spec((1, D_MODEL), zero),
            _const_spec((1, D_MODEL), zero),
        ],
        out_specs=x_spec,
        out_shape=jax.ShapeDtypeStruct(x.shape, F32),
        compiler_params=_cparams(2, 40),
        name=name,
    )(y, u, x, w, b, lg, lb)


def _class_slices(ref, q, F, rows, lanes):
    return (ref[q, rows, lanes], ref[q, pl.ds(rows.start + F, rows.size), lanes])


def _filter_kernel(feat_ref, w1_ref, b1_ref, fr_ref, w2_ref, b2_ref, w3f_ref, w3b_ref,
                   dcf_ref, dcb_ref, tf_ref, kf_ref, h_s, pa_s, pb_s, *, r, F, tc, inv_len):
    hi = lax.Precision.HIGHEST

    @pl.when((pl.program_id(0) == 0) & (pl.program_id(1) == 0))
    def _():
        fr = fr_ref[...]
        h = jnp.sin(fr * (jnp.dot(feat_ref[...], w1_ref[...], precision=hi, preferred_element_type=F32)
                          + b1_ref[...]))
        h = jnp.sin(fr * (jnp.dot(h, w2_ref[...], precision=hi, preferred_element_type=F32)
                          + b2_ref[...]))
        h_hi = h.astype(BF16)
        h_lo = (h - h_hi.astype(F32)).astype(BF16)
        w = FILTER_WIDTH
        h_s[:, 0:w] = h_hi
        h_s[:, w:2 * w] = h_hi
        h_s[:, 2 * w:3 * w] = h_lo
        h_s[:, 3 * w:4 * w] = jnp.zeros_like(h_lo)

    def last_layer(w_ref):
        w = w_ref[...]
        w_hi = w.astype(BF16)
        w_lo = (w - w_hi.astype(F32)).astype(BF16)
        w_cat = jnp.concatenate([w_hi, w_lo, w_hi, jnp.zeros_like(w_hi)], axis=0)
        return jnp.dot(h_s[...], w_cat, preferred_element_type=F32)

    t = feat_ref[:, 0:1]
    hf = last_layer(w3f_ref) * jnp.exp(-t * jnp.abs(dcf_ref[...]))
    hb = last_layer(w3b_ref) * jnp.exp(-t * jnp.abs(dcb_ref[...]))
    ss = jnp.sum(hf * hf + hb * hb, axis=0, keepdims=True)
    sc = lax.rsqrt(ss + FILTER_EPS)
    row = lax.broadcasted_iota(jnp.int32, hb.shape, 0)
    taps = (hf * sc, jnp.where(row == 0, 0.0, hb * sc))

    for h, dst in zip(taps, (pa_s, pb_s)):
        for q in range(r):
            dst[q] = jnp.dot(tf_ref[q], h[q * F:(q + 1) * F, :].astype(BF16), preferred_element_type=F32)

    def body(i, carry):
        rows = pl.ds(pl.multiple_of(i * SPEC_ROWS, SPEC_ROWS), SPEC_ROWS)
        rows_im = pl.ds(rows.start + F, rows.size)
        for lc in range(tc // V7X_LANES):
            lanes = slice(lc * V7X_LANES, (lc + 1) * V7X_LANES)
            xa = _fft([_class_slices(pa_s, q, F, rows, lanes) for q in range(r)], -1.0)
            xb = _fft([_class_slices(pb_s, q, F, rows, lanes) for q in range(r)], -1.0)
            for j in range(r):
                kf_ref[0, j, rows, lanes] = ((xa[j][0] + xb[j][0]) * inv_len).astype(kf_ref.dtype)
                kf_ref[0, j, rows_im, lanes] = ((xa[j][1] - xb[j][1]) * inv_len).astype(kf_ref.dtype)
        return carry

    lax.fori_loop(0, F // SPEC_ROWS, body, 0)


def _hyena_filter_spectrum(feats, w1p, b1, fr, w2, b2, w3, decay, tf, r):
    L = feats.shape[0]
    F = L // r
    E = HY_WIDTH
    tc = CONV_TC
    nct = E // tc
    zero = lambda o, c: (0, 0)
    col = lambda d: (lambda o, c: (0, (2 * o + d) * nct + c))
    kern = functools.partial(_filter_kernel, r=r, F=F, tc=tc, inv_len=1.0 / L)
    return pl.pallas_call(
        kern,
        grid=(HY_ORDER, nct),
        in_specs=[
            _const_spec((L, FILT_PAD_K), zero),
            _const_spec((FILT_PAD_K, FILTER_WIDTH), zero),
            _const_spec((1, FILTER_WIDTH), zero),
            _const_spec((1, FILTER_WIDTH), zero),
            _const_spec((FILTER_WIDTH, FILTER_WIDTH), zero),
            _const_spec((1, FILTER_WIDTH), zero),
            pl.BlockSpec((FILTER_WIDTH, tc), col(0)),
            pl.BlockSpec((FILTER_WIDTH, tc), col(1)),
            pl.BlockSpec((1, tc), col(0)),
            pl.BlockSpec((1, tc), col(1)),
            _const_spec((r, 2 * F, F), lambda o, c: (0, 0, 0)),
        ],
        out_specs=pl.BlockSpec((1, r, 2 * F, tc), lambda o, c: (o, 0, 0, c)),
        out_shape=jax.ShapeDtypeStruct((HY_ORDER, r, 2 * F, E), BF16),
        scratch_shapes=[pltpu.VMEM((L, 4 * FILTER_WIDTH), BF16),
                        pltpu.VMEM((r, 2 * F, tc), F32), pltpu.VMEM((r, 2 * F, tc), F32)],
        compiler_params=_cparams(2, 56),
        name="hyena_filter_spectrum",
    )(feats, w1p, b1, fr, w2, b2, w3, w3, decay, decay, tf)


def _short_conv_classes(u_ref, w, b, r, F):
    def cls(k):
        return u_ref[k * F:(k + 1) * F, :].astype(F32)

    first = cls(0)
    row = lax.broadcasted_iota(jnp.int32, first.shape, 0)
    prev = jnp.where(row == 0, 0.0, pltpu.roll(cls(r - 1), 1, axis=0))
    cur = first
    for q in range(r):
        if q < r - 1:
            nxt = cls(q + 1)
        else:
            nxt = jnp.where(row == F - 1, 0.0, pltpu.roll(first, F - 1, axis=0))
        yield q, prev * w[0:1] + cur * w[1:2] + nxt * w[2:3] + b
        prev, cur = cur, nxt


def _conv_kernel(uv_ref, ug0_ref, ug1_ref, kf_ref, tf_ref, ti_ref, cw_ref, cb_ref, fb_ref,
                 out_ref, z_s, p_s, a_s, *, r, F, tc):
    for q, v in _short_conv_classes(uv_ref, cw_ref[0], cb_ref[0:1, :], r, F):
        z_s[q] = v

    gate_refs = (ug0_ref, ug1_ref)
    for o in range(HY_ORDER):
        def spectral(rows, rows_im, o=o):
            for lc in range(tc // V7X_LANES):
                lanes = slice(lc * V7X_LANES, (lc + 1) * V7X_LANES)
                x = _fft([(p_s[q, rows, lanes], p_s[q, rows_im, lanes]) for q in range(r)], -1.0)
                y = []
                for j in range(r):
                    kre = kf_ref[o, j, rows, lanes]
                    kim = kf_ref[o, j, rows_im, lanes]
                    xre, xim = x[j]
                    y.append((xre * kre - xim * kim, xre * kim + xim * kre))
                a = _fft(y, 1.0)
                for q in range(r):
                    a_s[q, rows, lanes] = a[q][0]
                    a_s[q, rows_im, lanes] = a[q][1]

        for q in range(r):
            p_s[q] = jnp.dot(tf_ref[q], z_s[q].astype(BF16), preferred_element_type=F32).astype(p_s.dtype)

        def body(i, carry):
            rows = pl.ds(pl.multiple_of(i * SPEC_ROWS, SPEC_ROWS), SPEC_ROWS)
            spectral(rows, pl.ds(rows.start + F, rows.size))
            return carry

        lax.fori_loop(0, F // SPEC_ROWS, body, 0)

        gates = _short_conv_classes(gate_refs[o], cw_ref[1 + o], cb_ref[1 + o:2 + o, :], r, F)
        for q, g in gates:
            y = jnp.dot(ti_ref[q], a_s[q], preferred_element_type=F32)
            z_new = g * (y + z_s[q] * fb_ref[o:o + 1, :])
            if o == HY_ORDER - 1:
                out_ref[q * F:(q + 1) * F, :] = z_new.astype(out_ref.dtype)
            else:
                z_s[q] = z_new


def _hyena_conv(u, kf, tf, ti, cw, cb, fb, r):
    B, _, L, _ = u.shape
    F = L // r
    E = HY_WIDTH
    tc = CONV_TC
    nct = E // tc
    kern = functools.partial(_conv_kernel, r=r, F=F, tc=tc)
    sec = lambda s: pl.BlockSpec((None, None, L, tc), lambda c, b: (b, s * nct + c, 0, 0))
    return pl.pallas_call(
        kern,
        grid=(nct, B),
        in_specs=[
            sec(0), sec(1), sec(2),
            _const_spec((HY_ORDER, r, 2 * F, tc), lambda c, b: (0, 0, 0, c)),
            _const_spec((r, 2 * F, F), lambda c, b: (0, 0, 0)),
            _const_spec((r, F, 2 * F), lambda c, b: (0, 0, 0)),
            _const_spec((3, 3, tc), lambda c, b: (0, 0, c)),
            _const_spec((3, tc), lambda c, b: (0, c)),
            _const_spec((HY_ORDER, tc), lambda c, b: (0, c)),
        ],
        out_specs=pl.BlockSpec((None, None, L, tc), lambda c, b: (b, c, 0, 0)),
        out_shape=jax.ShapeDtypeStruct(_tile_major_shape(B, L, E), BF16),
        scratch_shapes=[pltpu.VMEM((r, F, tc), F32), pltpu.VMEM((r, 2 * F, tc), BF16),
                        pltpu.VMEM((r, 2 * F, tc), BF16)],
        compiler_params=_cparams(2, 56),
        name="hyena_long_conv",
    )(u, u, u, kf, tf, ti, cw, cb, fb)


def _attn_kernel(q_ref, k_ref, v_ref, o_ref, vt_s, kmax_s, p_s):
    @pl.when(pl.program_id(2) == 0)
    def _():
        vt_s[...] = v_ref[0].T
        kf = k_ref[0].astype(F32)
        row_sq = jnp.sum(kf * kf, axis=1, keepdims=True)
        kmax_s[...] = jnp.broadcast_to(jnp.max(row_sq, axis=0, keepdims=True), kmax_s.shape)

    k = k_ref[0]
    dims_nt = (((1,), (1,)), ((), ()))

    def q_head(g):
        return q_ref[0, :, g * HEAD_DIM:(g + 1) * HEAD_DIM]

    def finish(g, p_bf16, denom):
        ot = jnp.dot(vt_s[...], p_bf16, preferred_element_type=F32) * (1.0 / denom)
        o_ref[0, :, g * HEAD_DIM:(g + 1) * HEAD_DIM] = ot.T.astype(o_ref.dtype)

    ones = jnp.ones((V7X_SUBLANES, HEAD_DIM), BF16)
    denoms = []
    for g in range(GROUP):
        q = q_head(g)
        qf = q.astype(F32)
        q_sq = lax.dot_general(ones, (qf * qf).astype(BF16), dims_nt, preferred_element_type=F32)
        bound = BOUND_SLACK * jnp.sqrt(q_sq[0:1] * kmax_s[0:1])
        st = lax.dot_general(k, q, dims_nt, preferred_element_type=F32)
        p = jnp.exp2(st - bound)
        denoms.append(jnp.sum(p, axis=0, keepdims=True))
        p_s[g] = p.astype(BF16)
    for g in range(GROUP):
        finish(g, p_s[g], denoms[g])

    smallest = jnp.min(jnp.minimum(jnp.minimum(denoms[0], denoms[1]), jnp.minimum(denoms[2], denoms[3])))

    @pl.when(jnp.logical_not(smallest >= MIN_SAFE_DENOM))
    def _():
        for g in range(GROUP):
            st = lax.dot_general(k, q_head(g), dims_nt, preferred_element_type=F32)
            p = jnp.exp2(st - jnp.max(st, axis=0, keepdims=True))
            finish(g, p.astype(BF16), jnp.sum(p, axis=0, keepdims=True))


def _attention(u):
    B, L, _ = u.shape
    tq = min(L, ATTN_TILE_ELEMS // L)
    gw = GROUP * HEAD_DIM
    k_block = 2 * ATTN_WIDTH // HEAD_DIM
    v_block = (2 * ATTN_WIDTH + KV_WIDTH) // HEAD_DIM
    return pl.pallas_call(
        _attn_kernel,
        grid=(B, N_KV_HEADS, L // tq),
        in_specs=[
            pl.BlockSpec((1, tq, gw), lambda b, h, t: (b, t, h)),
            pl.BlockSpec((1, L, HEAD_DIM), lambda b, h, t: (b, 0, k_block + h)),
            pl.BlockSpec((1, L, HEAD_DIM), lambda b, h, t: (b, 0, v_block + h)),
        ],
        out_specs=pl.BlockSpec((1, tq, gw), lambda b, h, t: (b, t, h)),
        out_shape=jax.ShapeDtypeStruct((B, L, ATTN_WIDTH), BF16),
        scratch_shapes=[pltpu.VMEM((HEAD_DIM, L), BF16),
                        pltpu.VMEM((V7X_SUBLANES, tq), F32),
                        pltpu.VMEM((GROUP, L, tq), BF16)],
        compiler_params=_cparams(3, 56),
        name="gqa_attention",
    )(u, u, u)


def _class_dft_tables(L, r):
    F = L // r
    f = jnp.arange(F, dtype=jnp.int32)
    m = jnp.arange(F, dtype=jnp.int32)
    q = jnp.arange(r, dtype=jnp.int32)
    n = ((2 * f + 1)[None, :, None] * (r * m[None, None, :] + q[:, None, None])) % (4 * L)
    theta = n.astype(F32) * (math.pi / (2 * L))
    c = jnp.cos(theta)
    s = jnp.sin(theta)
    tf = jnp.concatenate([c, -s], axis=1).astype(BF16)
    return tf, tf.transpose(0, 2, 1)


def _filter_features(L):
    t = jnp.linspace(0.0, 1.0, L, dtype=F32)[:, None]
    w = 2.0 * math.pi * jnp.arange(L, dtype=F32)[:, None] / L
    f = jnp.linspace(1e-4, POS_BANDS - 1, POS_BANDS, dtype=F32)[None]
    feats = jnp.concatenate([t, jnp.cos(f * w), -jnp.sin(f * w)], axis=-1)
    return jnp.pad(feats, ((0, 0), (0, FILT_PAD_K - POS_EMB_DIM)))


def _rope_tables(L):
    rows = L // GRID_W
    row = jnp.repeat(jnp.arange(rows, dtype=F32), GRID_W)
    col = jnp.tile(jnp.arange(GRID_W, dtype=F32), rows)
    inv = ROPE_THETA ** (-jnp.arange(0, AXIS_DIM, 2, dtype=F32) / AXIS_DIM)
    ang = jnp.concatenate([row[:, None] * inv, col[:, None] * inv], axis=-1)
    c, s = jnp.cos(ang), jnp.sin(ang)
    return jnp.concatenate([c, c], axis=-1), jnp.concatenate([-s, s], axis=-1)


def _to_class_major(a, r):
    lead, L, C = a.shape[:-2], a.shape[-2], a.shape[-1]
    return a.reshape(lead + (L // r, r, C)).swapaxes(-3, -2).reshape(a.shape)


def _from_class_major(a, r):
    lead, L, C = a.shape[:-2], a.shape[-2], a.shape[-1]
    return a.reshape(lead + (r, L // r, C)).swapaxes(-3, -2).reshape(a.shape)


def _attn_in_params(p, j):
    w = p["at_w_in"][j]
    bi = p["at_b_in"][j]
    kv0 = ATTN_WIDTH
    g0 = ATTN_WIDTH + 2 * KV_WIDTH
    w = jnp.concatenate([w[:, :kv0], w[:, g0:], w[:, kv0:g0]], axis=1).astype(BF16)
    bi = jnp.concatenate([bi[:kv0], bi[g0:], bi[kv0:g0]])[None]
    return (w, bi, p["at_q_norm"][j][None], p["at_k_norm"][j][None])


def _trunk(x, p):
    B, L, _ = x.shape
    r = L // CLASS_ROWS
    tf, ti = _class_dft_tables(L, r)
    feats = _to_class_major(_filter_features(L), r)
    rope = tuple(_to_class_major(t, r) for t in _rope_tables(L))
    x = _to_class_major(x, r)

    def in_params(i):
        j = i // 2
        if i % 2 == 0:
            return (p["hy_w_in"][j].astype(BF16), p["hy_b_in"][j][None])
        return _attn_in_params(p, j)

    def out_params(i):
        j = i // 2
        w, b = (p["hy_w_out"], p["hy_b_out"]) if i % 2 == 0 else (p["at_w_out"], p["at_b_out"])
        return (w[j].astype(BF16), b[j][None], p["ln_g"][i][None], p["ln_b"][i][None])

    u = _hyena_proj(x, *in_params(0))
    for i in range(DEPTH):
        j = i // 2
        if i % 2 == 0:
            w1p = jnp.pad(p["hy_f_w1"][j], ((0, FILT_PAD_K - POS_EMB_DIM), (0, 0)))
            kf = _hyena_filter_spectrum(feats, w1p, p["hy_f_b1"][j][None], p["hy_f_freq"][j][None],
                                        p["hy_f_w2"][j], p["hy_f_b2"][j][None], p["hy_f_w3"][j],
                                        p["hy_decay"][j][None], tf, r)
            cw = p["hy_conv_w"][j].reshape(3, 3, HY_WIDTH).transpose(1, 0, 2)
            cb = p["hy_conv_b"][j].reshape(3, HY_WIDTH)
            y = _hyena_conv(u, kf, tf, ti, cw, cb, p["hy_filt_bias"][j], r)
            gate_block = 3 * HY_WIDTH // D_MODEL
            name = "hyena_out"
        else:
            y = _attention(u)
            gate_block = 1
            name = "attn_out"
        if i + 1 < DEPTH:
            x, u = _layer_boundary(y, u, gate_block, x, out_params(i), in_params(i + 1), rope,
                                   name + "_next_in")
        else:
            x = _out_proj_ln(y, u, gate_block, x, *out_params(i), name + "_ln")
    return _from_class_major(x, r)


def kernel(x_prompt, x_sample, hy_w_in, hy_b_in, hy_conv_w, hy_conv_b, hy_f_w1, hy_f_b1, hy_f_freq, hy_f_w2,
           hy_f_b2, hy_f_w3, hy_decay, hy_filt_bias, hy_w_out, hy_b_out, at_w_in, at_b_in, at_q_norm, at_k_norm,
           at_w_out, at_b_out, ln_g, ln_b):
    p = dict(hy_w_in=hy_w_in, hy_b_in=hy_b_in, hy_conv_w=hy_conv_w, hy_conv_b=hy_conv_b, hy_f_w1=hy_f_w1,
             hy_f_b1=hy_f_b1, hy_f_freq=hy_f_freq, hy_f_w2=hy_f_w2, hy_f_b2=hy_f_b2, hy_f_w3=hy_f_w3,
             hy_decay=hy_decay, hy_filt_bias=hy_filt_bias, hy_w_out=hy_w_out, hy_b_out=hy_b_out,
             at_w_in=at_w_in, at_b_in=at_b_in, at_q_norm=at_q_norm, at_k_norm=at_k_norm, at_w_out=at_w_out,
             at_b_out=at_b_out, ln_g=ln_g, ln_b=ln_b)
    return (_trunk(x_prompt, p), _trunk(x_sample, p))
```

```python
import cmath
import functools
import math

import jax
import jax.numpy as jnp
from jax import lax
from jax.experimental import pallas as pl
from jax.experimental.pallas import tpu as pltpu

F32 = jnp.float32
BF16 = jnp.bfloat16

D_MODEL = 1024
DEPTH = 4
GRID_W = 64
HY_WIDTH = D_MODEL
HY_ORDER = 2
POS_EMB_DIM = 33
POS_BANDS = (POS_EMB_DIM - 1) // 2
FILTER_WIDTH = 64
FILTER_EPS = 1e-6
HEAD_DIM = 128
N_Q_HEADS = D_MODEL // HEAD_DIM
N_KV_HEADS = 2
GROUP = N_Q_HEADS // N_KV_HEADS
ATTN_WIDTH = N_Q_HEADS * HEAD_DIM
KV_WIDTH = N_KV_HEADS * HEAD_DIM
AXIS_DIM = HEAD_DIM // 2
ROPE_THETA = 10000.0
RMS_EPS = 1e-6
LN_EPS = 1e-5
DEEPNORM_ALPHA = (2.0 * DEPTH) ** 0.25

V7X_LANES = 128
V7X_SUBLANES = 8
V7X_MXU_DIM = 256
V7X_VMEM_BYTES = 64 * 1024 * 1024

CLASS_ROWS = V7X_MXU_DIM
CONV_TC = V7X_MXU_DIM
SPEC_ROWS = 2 * V7X_SUBLANES
ATTN_TILE_ELEMS = 512 * 4096
OUT_TM = 512
BOUNDARY_SUB_ROWS = 256
HYENA_TM = 1024
FILT_PAD_K = 64
BOUND_SLACK = 1.02
MIN_SAFE_DENOM = 2.0 ** -90


def _cparams(n_axes, vmem_mb):
    return pltpu.CompilerParams(
        dimension_semantics=("arbitrary",) * n_axes,
        vmem_limit_bytes=vmem_mb * 1024 * 1024,
    )


def _const_spec(block, index_map):
    return pl.BlockSpec(block, index_map, pipeline_mode=pl.Buffered(1))


def _add(a, b):
    if a is None:
        return b
    if b is None:
        return a
    return a + b


def _sub(a, b):
    if b is None:
        return a
    if a is None:
        return -b
    return a - b


def _scale(a, c):
    return None if a is None else a * c


def _twiddle(x, w):
    re, im = x
    c, s = w.real, w.imag
    if abs(s) < 1e-12:
        return (re, im) if c > 0 else (_sub(None, re), _sub(None, im))
    if abs(c) < 1e-12:
        return (_sub(None, im), re) if s > 0 else (im, _sub(None, re))
    if abs(abs(c) - abs(s)) < 1e-12:
        a = abs(c)
        total = _add(re, im)
        if c > 0 and s > 0:
            return (_scale(_sub(re, im), a), _scale(total, a))
        if c > 0:
            return (_scale(total, a), _scale(_sub(im, re), a))
        if s > 0:
            return (_scale(total, -a), _scale(_sub(re, im), a))
        return (_scale(_sub(im, re), a), _scale(total, -a))
    nre = _sub(_scale(re, c), _scale(im, s))
    nim = _add(_scale(re, s), _scale(im, c))
    return (nre, nim)


def _fft(xs, sign, first_half_only=False):
    n = len(xs)
    if n == 1:
        return list(xs)
    ev = _fft(xs[0::2], sign)
    od = _fft(xs[1::2], sign)
    out = [None] * n
    for k in range(n // 2):
        t = _twiddle(od[k], cmath.exp(sign * 2j * math.pi * k / n))
        out[k] = (_add(ev[k][0], t[0]), _add(ev[k][1], t[1]))
        if not first_half_only:
            out[k + n // 2] = (_sub(ev[k][0], t[0]), _sub(ev[k][1], t[1]))
    return out[: n // 2] if first_half_only else out


def _store_col_tiles(o_ref, rows, acc):
    for c in range(acc.shape[1] // CONV_TC):
        o_ref[0, c, rows, :] = acc[:, c * CONV_TC:(c + 1) * CONV_TC].astype(o_ref.dtype)


def _load_col_tiles(ref, rows):
    return jnp.concatenate([ref[0, c, rows, :] for c in range(ref.shape[1])], axis=1)


def _tile_major_shape(B, L, n):
    return (B, n // CONV_TC, L, CONV_TC)


def _proj_kernel(x_ref, w_ref, b_ref, o_ref):
    x = x_ref[0].astype(BF16)
    acc = jnp.dot(x, w_ref[...], preferred_element_type=F32) + b_ref[...]
    _store_col_tiles(o_ref, slice(None), acc)


def _hyena_proj(x, w, b):
    B, L, _ = x.shape
    n_out = w.shape[1]
    tm = HYENA_TM
    return pl.pallas_call(
        _proj_kernel,
        grid=(B, L // tm),
        in_specs=[
            pl.BlockSpec((1, tm, D_MODEL), lambda b, t: (b, t, 0)),
            _const_spec((D_MODEL, n_out), lambda b, t: (0, 0)),
            _const_spec((1, n_out), lambda b, t: (0, 0)),
        ],
        out_specs=pl.BlockSpec((1, n_out // CONV_TC, tm, CONV_TC), lambda b, t: (b, 0, t, 0)),
        out_shape=jax.ShapeDtypeStruct(_tile_major_shape(B, L, n_out), BF16),
        compiler_params=_cparams(2, 48),
        name="hyena_in_proj",
    )(x, w, b)


def _attn_in_epilogue(acc, q_gain, k_gain, cos2, sin2, store):
    def head(col, gain, scale):
        xh = acc[:, col:col + HEAD_DIM]
        ms = jnp.mean(xh * xh, axis=-1, keepdims=True)
        xn = xh * lax.rsqrt(ms + RMS_EPS) * gain
        rot = xn * cos2 + pltpu.roll(xn, AXIS_DIM, axis=1) * sin2
        store(col, rot * scale if scale != 1.0 else rot)

    for h in range(N_Q_HEADS):
        head(h * HEAD_DIM, q_gain, HEAD_DIM ** -0.5 * math.log2(math.e))
    store(ATTN_WIDTH, acc[:, ATTN_WIDTH:2 * ATTN_WIDTH])
    for h in range(N_KV_HEADS):
        head(2 * ATTN_WIDTH + h * HEAD_DIM, k_gain, 1.0)
    v0 = 2 * ATTN_WIDTH + KV_WIDTH
    store(v0, acc[:, v0:v0 + KV_WIDTH])


def _gated_out_ln(y, g, x, w_ref, b_ref, lg_ref, lb_ref):
    y = y.astype(F32)
    g = g.astype(F32)
    yg = (y * (g * jax.nn.sigmoid(g))).astype(BF16)
    s = jnp.dot(yg, w_ref[...], preferred_element_type=F32) + b_ref[...]
    v = DEEPNORM_ALPHA * x + s
    mu = jnp.mean(v, axis=-1, keepdims=True)
    vc = v - mu
    var = jnp.mean(vc * vc, axis=-1, keepdims=True)
    return vc * lax.rsqrt(var + LN_EPS) * lg_ref[...] + lb_ref[...]


def _boundary_kernel(*refs, attn_next, sub_rows):
    y_ref, g_ref, x_ref, wo_ref, bo_ref, lg_ref, lb_ref, wi_ref, bi_ref = refs[:9]
    if attn_next:
        qn_ref, kn_ref, cos_ref, sin_ref, xo_ref, u_ref = refs[9:]
    else:
        xo_ref, u_ref = refs[9:]
    sub_tiles = [slice(r0, r0 + sub_rows) for r0 in range(0, x_ref.shape[1], sub_rows)]
    normed = []
    for rows in sub_tiles:
        if attn_next:
            y, g = _load_col_tiles(y_ref, rows), _load_col_tiles(g_ref, rows)
        else:
            y, g = y_ref[0, rows, :], g_ref[0, rows, :]
        xn = _gated_out_ln(y, g, x_ref[0, rows, :], wo_ref, bo_ref, lg_ref, lb_ref)
        xo_ref[0, rows, :] = xn
        normed.append(xn.astype(BF16))
    for rows, xb in zip(sub_tiles, normed):
        acc = jnp.dot(xb, wi_ref[...], preferred_element_type=F32) + bi_ref[...]
        if attn_next:
            def store(col, val, rows=rows):
                u_ref[0, rows, col:col + val.shape[1]] = val.astype(u_ref.dtype)

            _attn_in_epilogue(acc, qn_ref[...], kn_ref[...], cos_ref[rows, :], sin_ref[rows, :], store)
        else:
            _store_col_tiles(u_ref, rows, acc)


def _layer_boundary(y, u, gate_block, x, out_params, in_params, rope, name):
    B, L, _ = x.shape
    attn_next = len(in_params) == 4
    n_next = in_params[0].shape[1]
    tm = 2 * OUT_TM if attn_next else OUT_TM
    x_spec = pl.BlockSpec((1, tm, D_MODEL), lambda b_, t: (b_, t, 0))
    zero = lambda b_, t: (0, 0)
    if attn_next:
        tiles = D_MODEL // CONV_TC
        y_spec = pl.BlockSpec((1, tiles, tm, CONV_TC), lambda b_, t: (b_, 0, t, 0))
        g_spec = pl.BlockSpec((1, tiles, tm, CONV_TC), lambda b_, t: (b_, gate_block, t, 0))
        u_spec = pl.BlockSpec((1, tm, n_next), lambda b_, t: (b_, t, 0))
        u_shape = (B, L, n_next)
    else:
        y_spec = x_spec
        g_spec = pl.BlockSpec((1, tm, D_MODEL), lambda b_, t: (b_, t, gate_block))
        u_spec = pl.BlockSpec((1, n_next // CONV_TC, tm, CONV_TC), lambda b_, t: (b_, 0, t, 0))
        u_shape = _tile_major_shape(B, L, n_next)
    in_specs = [
        y_spec,
        g_spec,
        x_spec,
        _const_spec((D_MODEL, D_MODEL), zero),
        _const_spec((1, D_MODEL), zero),
        _const_spec((1, D_MODEL), zero),
        _const_spec((1, D_MODEL), zero),
        _const_spec((D_MODEL, n_next), zero),
        _const_spec((1, n_next), zero),
    ]
    args = [y, u, x, *out_params, *in_params]
    if attn_next:
        in_specs += [
            _const_spec((1, HEAD_DIM), zero),
            _const_spec((1, HEAD_DIM), zero),
            pl.BlockSpec((tm, HEAD_DIM), lambda b_, t: (t, 0)),
            pl.BlockSpec((tm, HEAD_DIM), lambda b_, t: (t, 0)),
        ]
        args += list(rope)
    return pl.pallas_call(
        functools.partial(_boundary_kernel, attn_next=attn_next, sub_rows=BOUNDARY_SUB_ROWS),
        grid=(B, L // tm),
        in_specs=in_specs,
        out_specs=[x_spec, u_spec],
        out_shape=[jax.ShapeDtypeStruct(x.shape, F32), jax.ShapeDtypeStruct(u_shape, BF16)],
        compiler_params=_cparams(2, 56),
        name=name,
    )(*args)


def _out_kernel(y_ref, g_ref, x_ref, w_ref, b_ref, lg_ref, lb_ref, o_ref):
    o_ref[0] = _gated_out_ln(y_ref[0], g_ref[0], x_ref[0], w_ref, b_ref, lg_ref, lb_ref)


def _out_proj_ln(y, u, gate_block, x, w, b, lg, lb, name):
    B, L, _ = x.shape
    tm = 2 * OUT_TM
    x_spec = pl.BlockSpec((1, tm, D_MODEL), lambda b_, t: (b_, t, 0))
    zero = lambda b_, t: (0, 0)
    return pl.pallas_call(
        _out_kernel,
        grid=(B, L // tm),
        in_specs=[
            x_spec,
            pl.BlockSpec((1, tm, D_MODEL), lambda b_, t: (b_, t, gate_block)),
            x_spec,
            _const_spec((D_MODEL, D_MODEL), zero),
            _const_spec((1, D_MODEL), zero),
            _const_spec((1, D_MODEL), zero),
            _const_spec((1, D_MODEL), zero),
        ],
        out_specs=x_spec,
        out_shape=jax.ShapeDtypeStruct(x.shape, F32),
        compiler_params=_cparams(2, 40),
        name=name,
    )(y, u, x, w, b, lg, lb)


def _class_slices(ref, q, F, rows, lanes):
    return (ref[q, rows, lanes], ref[q, pl.ds(rows.start + F, rows.size), lanes])


def _filter_kernel(feat_ref, w1_ref, b1_ref, fr_ref, w2_ref, b2_ref, w3f_ref, w3b_ref,
                   dcf_ref, dcb_ref, tf_ref, kf_ref, h_s, pa_s, pb_s, *, r, F, tc, inv_len):
    hi = lax.Precision.HIGHEST

    @pl.when((pl.program_id(0) == 0) & (pl.program_id(1) == 0))
    def _():
        fr = fr_ref[...]
        h = jnp.sin(fr * (jnp.dot(feat_ref[...], w1_ref[...], precision=hi, preferred_element_type=F32)
                          + b1_ref[...]))
        h = jnp.sin(fr * (jnp.dot(h, w2_ref[...], precision=hi, preferred_element_type=F32)
                          + b2_ref[...]))
        h_hi = h.astype(BF16)
        h_lo = (h - h_hi.astype(F32)).astype(BF16)
        w = FILTER_WIDTH
        h_s[:, 0:w] = h_hi
        h_s[:, w:2 * w] = h_hi
        h_s[:, 2 * w:3 * w] = h_lo
        h_s[:, 3 * w:4 * w] = jnp.zeros_like(h_lo)

    def last_layer(w_ref):
        w = w_ref[...]
        w_hi = w.astype(BF16)
        w_lo = (w - w_hi.astype(F32)).astype(BF16)
        w_cat = jnp.concatenate([w_hi, w_lo, w_hi, jnp.zeros_like(w_hi)], axis=0)
        return jnp.dot(h_s[...], w_cat, preferred_element_type=F32)

    t = feat_ref[:, 0:1]
    hf = last_layer(w3f_ref) * jnp.exp(-t * jnp.abs(dcf_ref[...]))
    hb = last_layer(w3b_ref) * jnp.exp(-t * jnp.abs(dcb_ref[...]))
    ss = jnp.sum(hf * hf + hb * hb, axis=0, keepdims=True)
    sc = lax.rsqrt(ss + FILTER_EPS) * inv_len
    row = lax.broadcasted_iota(jnp.int32, hb.shape, 0)
    taps = (hf * sc, jnp.where(row == 0, 0.0, hb * sc))

    for h, dst in zip(taps, (pa_s, pb_s)):
        for q in range(r):
            dst[q] = jnp.dot(tf_ref[q], h[q * F:(q + 1) * F, :].astype(BF16), preferred_element_type=F32)

    def body(i, carry):
        rows = pl.ds(pl.multiple_of(i * SPEC_ROWS, SPEC_ROWS), SPEC_ROWS)
        rows_im = pl.ds(rows.start + F, rows.size)
        for lc in range(tc // V7X_LANES):
            lanes = slice(lc * V7X_LANES, (lc + 1) * V7X_LANES)
            e = []
            for q in range(r):
                are, aim = _class_slices(pa_s, q, F, rows, lanes)
                bre, bim = _class_slices(pb_s, (r - q) % r, F, rows, lanes)
                e.append((are + bre, aim - bim))
            x = _fft(e, -1.0)
            for j in range(r):
                kf_ref[0, j, rows, lanes] = x[j][0].astype(kf_ref.dtype)
                kf_ref[0, j, rows_im, lanes] = x[j][1].astype(kf_ref.dtype)
        return carry

    lax.fori_loop(0, F // SPEC_ROWS, body, 0)


def _hyena_filter_spectrum(feats, w1p, b1, fr, w2, b2, w3, decay, tf, r):
    L = feats.shape[0]
    F = L // r
    E = HY_WIDTH
    tc = CONV_TC
    nct = E // tc
    zero = lambda o, c: (0, 0)
    col = lambda d: (lambda o, c: (0, (2 * o + d) * nct + c))
    kern = functools.partial(_filter_kernel, r=r, F=F, tc=tc, inv_len=1.0 / L)
    return pl.pallas_call(
        kern,
        grid=(HY_ORDER, nct),
        in_specs=[
            _const_spec((L, FILT_PAD_K), zero),
            _const_spec((FILT_PAD_K, FILTER_WIDTH), zero),
            _const_spec((1, FILTER_WIDTH), zero),
            _const_spec((1, FILTER_WIDTH), zero),
            _const_spec((FILTER_WIDTH, FILTER_WIDTH), zero),
            _const_spec((1, FILTER_WIDTH), zero),
            pl.BlockSpec((FILTER_WIDTH, tc), col(0)),
            pl.BlockSpec((FILTER_WIDTH, tc), col(1)),
            pl.BlockSpec((1, tc), col(0)),
            pl.BlockSpec((1, tc), col(1)),
            _const_spec((r, 2 * F, F), lambda o, c: (0, 0, 0)),
        ],
        out_specs=pl.BlockSpec((1, r, 2 * F, tc), lambda o, c: (o, 0, 0, c)),
        out_shape=jax.ShapeDtypeStruct((HY_ORDER, r, 2 * F, E), BF16),
        scratch_shapes=[pltpu.VMEM((L, 4 * FILTER_WIDTH), BF16),
                        pltpu.VMEM((r, 2 * F, tc), F32), pltpu.VMEM((r, 2 * F, tc), F32)],
        compiler_params=_cparams(2, 56),
        name="hyena_filter_spectrum",
    )(feats, w1p, b1, fr, w2, b2, w3, w3, decay, decay, tf)


def _short_conv_classes(u_ref, w, b, r, F):
    def cls(k):
        return u_ref[k * F:(k + 1) * F, :].astype(F32)

    first = cls(0)
    row = lax.broadcasted_iota(jnp.int32, first.shape, 0)
    prev = jnp.where(row == 0, 0.0, pltpu.roll(cls(r - 1), 1, axis=0))
    cur = first
    for q in range(r):
        if q < r - 1:
            nxt = cls(q + 1)
        else:
            nxt = jnp.where(row == F - 1, 0.0, pltpu.roll(first, F - 1, axis=0))
        yield q, prev * w[0:1] + cur * w[1:2] + nxt * w[2:3] + b
        prev, cur = cur, nxt


def _conv_kernel(uv_ref, ug0_ref, ug1_ref, kf_ref, tf_ref, ti_ref, cw_ref, cb_ref, fb_ref,
                 out_ref, z_s, p_s, a_s, *, r, F, tc):
    for q, v in _short_conv_classes(uv_ref, cw_ref[0], cb_ref[0:1, :], r, F):
        z_s[q] = v

    gate_refs = (ug0_ref, ug1_ref)
    for o in range(HY_ORDER):
        def spectral(rows, rows_im, o=o):
            for lc in range(tc // V7X_LANES):
                lanes = slice(lc * V7X_LANES, (lc + 1) * V7X_LANES)
                x = _fft([(p_s[q, rows, lanes], p_s[q, rows_im, lanes]) for q in range(r)], -1.0)
                y = []
                for j in range(r):
                    kre = kf_ref[o, j, rows, lanes]
                    kim = kf_ref[o, j, rows_im, lanes]
                    xre, xim = x[j]
                    y.append((xre * kre - xim * kim, xre * kim + xim * kre))
                a = _fft(y, 1.0)
                for q in range(r):
                    a_s[q, rows, lanes] = a[q][0]
                    a_s[q, rows_im, lanes] = a[q][1]

        for q in range(r):
            p_s[q] = jnp.dot(tf_ref[q], z_s[q].astype(BF16), preferred_element_type=F32).astype(p_s.dtype)

        def body(i, carry):
            rows = pl.ds(pl.multiple_of(i * SPEC_ROWS, SPEC_ROWS), SPEC_ROWS)
            spectral(rows, pl.ds(rows.start + F, rows.size))
            return carry

        lax.fori_loop(0, F // SPEC_ROWS, body, 0)

        gates = _short_conv_classes(gate_refs[o], cw_ref[1 + o], cb_ref[1 + o:2 + o, :], r, F)
        for q, g in gates:
            y = jnp.dot(ti_ref[q], a_s[q], preferred_element_type=F32)
            z_new = g * (y + z_s[q] * fb_ref[o:o + 1, :])
            if o == HY_ORDER - 1:
                out_ref[q * F:(q + 1) * F, :] = z_new.astype(out_ref.dtype)
            else:
                z_s[q] = z_new


def _hyena_conv(u, kf, tf, ti, cw, cb, fb, r):
    B, _, L, _ = u.shape
    F = L // r
    E = HY_WIDTH
    tc = CONV_TC
    nct = E // tc
    kern = functools.partial(_conv_kernel, r=r, F=F, tc=tc)
    sec = lambda s: pl.BlockSpec((None, None, L, tc), lambda c, b: (b, s * nct + c, 0, 0))
    return pl.pallas_call(
        kern,
        grid=(nct, B),
        in_specs=[
            sec(0), sec(1), sec(2),
            _const_spec((HY_ORDER, r, 2 * F, tc), lambda c, b: (0, 0, 0, c)),
            _const_spec((r, 2 * F, F), lambda c, b: (0, 0, 0)),
            _const_spec((r, F, 2 * F), lambda c, b: (0, 0, 0)),
            _const_spec((3, 3, tc), lambda c, b: (0, 0, c)),
            _const_spec((3, tc), lambda c, b: (0, c)),
            _const_spec((HY_ORDER, tc), lambda c, b: (0, c)),
        ],
        out_specs=pl.BlockSpec((None, None, L, tc), lambda c, b: (b, c, 0, 0)),
        out_shape=jax.ShapeDtypeStruct(_tile_major_shape(B, L, E), BF16),
        scratch_shapes=[pltpu.VMEM((r, F, tc), F32), pltpu.VMEM((r, 2 * F, tc), BF16),
                        pltpu.VMEM((r, 2 * F, tc), BF16)],
        compiler_params=_cparams(2, 56),
        name="hyena_long_conv",
    )(u, u, u, kf, tf, ti, cw, cb, fb)


def _attn_kernel(q_ref, k_ref, v_ref, o_ref, vt_s, kmax_s, p_s):
    @pl.when(pl.program_id(2) == 0)
    def _():
        vt_s[...] = v_ref[0].T
        kf = k_ref[0].astype(F32)
        row_sq = jnp.sum(kf * kf, axis=1, keepdims=True)
        kmax_s[...] = jnp.broadcast_to(jnp.max(row_sq, axis=0, keepdims=True), kmax_s.shape)

    k = k_ref[0]
    dims_nt = (((1,), (1,)), ((), ()))

    def q_head(g):
        return q_ref[0, :, g * HEAD_DIM:(g + 1) * HEAD_DIM]

    def finish(g, p_bf16, denom):
        ot = jnp.dot(vt_s[...], p_bf16, preferred_element_type=F32) * (1.0 / denom)
        o_ref[0, :, g * HEAD_DIM:(g + 1) * HEAD_DIM] = ot.T.astype(o_ref.dtype)

    ones = jnp.ones((V7X_SUBLANES, HEAD_DIM), BF16)
    denoms = []
    for g in range(GROUP):
        q = q_head(g)
        qf = q.astype(F32)
        q_sq = lax.dot_general(ones, (qf * qf).astype(BF16), dims_nt, preferred_element_type=F32)
        bound = BOUND_SLACK * jnp.sqrt(q_sq[0:1] * kmax_s[0:1])
        st = lax.dot_general(k, q, dims_nt, preferred_element_type=F32)
        p = jnp.exp2(st - bound)
        denoms.append(jnp.sum(p, axis=0, keepdims=True))
        p_s[g] = p.astype(BF16)
    for g in range(GROUP):
        finish(g, p_s[g], denoms[g])

    smallest = jnp.min(jnp.minimum(jnp.minimum(denoms[0], denoms[1]), jnp.minimum(denoms[2], denoms[3])))

    @pl.when(jnp.logical_not(smallest >= MIN_SAFE_DENOM))
    def _():
        for g in range(GROUP):
            st = lax.dot_general(k, q_head(g), dims_nt, preferred_element_type=F32)
            p = jnp.exp2(st - jnp.max(st, axis=0, keepdims=True))
            finish(g, p.astype(BF16), jnp.sum(p, axis=0, keepdims=True))


def _attention(u):
    B, L, _ = u.shape
    tq = min(L, ATTN_TILE_ELEMS // L)
    gw = GROUP * HEAD_DIM
    k_block = 2 * ATTN_WIDTH // HEAD_DIM
    v_block = (2 * ATTN_WIDTH + KV_WIDTH) // HEAD_DIM
    return pl.pallas_call(
        _attn_kernel,
        grid=(B, N_KV_HEADS, L // tq),
        in_specs=[
            pl.BlockSpec((1, tq, gw), lambda b, h, t: (b, t, h)),
            pl.BlockSpec((1, L, HEAD_DIM), lambda b, h, t: (b, 0, k_block + h)),
            pl.BlockSpec((1, L, HEAD_DIM), lambda b, h, t: (b, 0, v_block + h)),
        ],
        out_specs=pl.BlockSpec((1, tq, gw), lambda b, h, t: (b, t, h)),
        out_shape=jax.ShapeDtypeStruct((B, L, ATTN_WIDTH), BF16),
        scratch_shapes=[pltpu.VMEM((HEAD_DIM, L), BF16),
                        pltpu.VMEM((V7X_SUBLANES, tq), F32),
                        pltpu.VMEM((GROUP, L, tq), BF16)],
        compiler_params=_cparams(3, 56),
        name="gqa_attention",
    )(u, u, u)


def _class_dft_tables(L, r):
    F = L // r
    f = jnp.arange(F, dtype=jnp.int32)
    m = jnp.arange(F, dtype=jnp.int32)
    q = jnp.arange(r, dtype=jnp.int32)
    n = ((2 * f + 1)[None, :, None] * (r * m[None, None, :] + q[:, None, None])) % (4 * L)
    theta = n.astype(F32) * (math.pi / (2 * L))
    c = jnp.cos(theta)
    s = jnp.sin(theta)
    tf = jnp.concatenate([c, -s], axis=1).astype(BF16)
    return tf, tf.transpose(0, 2, 1)


def _filter_features(L):
    t = jnp.linspace(0.0, 1.0, L, dtype=F32)[:, None]
    w = 2.0 * math.pi * jnp.arange(L, dtype=F32)[:, None] / L
    f = jnp.linspace(1e-4, POS_BANDS - 1, POS_BANDS, dtype=F32)[None]
    feats = jnp.concatenate([t, jnp.cos(f * w), -jnp.sin(f * w)], axis=-1)
    return jnp.pad(feats, ((0, 0), (0, FILT_PAD_K - POS_EMB_DIM)))


def _rope_tables(L):
    rows = L // GRID_W
    row = jnp.repeat(jnp.arange(rows, dtype=F32), GRID_W)
    col = jnp.tile(jnp.arange(GRID_W, dtype=F32), rows)
    inv = ROPE_THETA ** (-jnp.arange(0, AXIS_DIM, 2, dtype=F32) / AXIS_DIM)
    ang = jnp.concatenate([row[:, None] * inv, col[:, None] * inv], axis=-1)
    c, s = jnp.cos(ang), jnp.sin(ang)
    return jnp.concatenate([c, c], axis=-1), jnp.concatenate([-s, s], axis=-1)


def _to_class_major(a, r):
    lead, L, C = a.shape[:-2], a.shape[-2], a.shape[-1]
    return a.reshape(lead + (L // r, r, C)).swapaxes(-3, -2).reshape(a.shape)


def _from_class_major(a, r):
    lead, L, C = a.shape[:-2], a.shape[-2], a.shape[-1]
    return a.reshape(lead + (r, L // r, C)).swapaxes(-3, -2).reshape(a.shape)


def _attn_in_params(p, j):
    w = p["at_w_in"][j]
    bi = p["at_b_in"][j]
    kv0 = ATTN_WIDTH
    g0 = ATTN_WIDTH + 2 * KV_WIDTH
    w = jnp.concatenate([w[:, :kv0], w[:, g0:], w[:, kv0:g0]], axis=1).astype(BF16)
    bi = jnp.concatenate([bi[:kv0], bi[g0:], bi[kv0:g0]])[None]
    return (w, bi, p["at_q_norm"][j][None], p["at_k_norm"][j][None])


def _trunk(x, p):
    B, L, _ = x.shape
    r = L // CLASS_ROWS
    tf, ti = _class_dft_tables(L, r)
    feats = _to_class_major(_filter_features(L), r)
    rope = tuple(_to_class_major(t, r) for t in _rope_tables(L))
    x = _to_class_major(x, r)

    def in_params(i):
        j = i // 2
        if i % 2 == 0:
            return (p["hy_w_in"][j].astype(BF16), p["hy_b_in"][j][None])
        return _attn_in_params(p, j)

    def out_params(i):
        j = i // 2
        w, b = (p["hy_w_out"], p["hy_b_out"]) if i % 2 == 0 else (p["at_w_out"], p["at_b_out"])
        return (w[j].astype(BF16), b[j][None], p["ln_g"][i][None], p["ln_b"][i][None])

    u = _hyena_proj(x, *in_params(0))
    for i in range(DEPTH):
        j = i // 2
        if i % 2 == 0:
            w1p = jnp.pad(p["hy_f_w1"][j], ((0, FILT_PAD_K - POS_EMB_DIM), (0, 0)))
            kf = _hyena_filter_spectrum(feats, w1p, p["hy_f_b1"][j][None], p["hy_f_freq"][j][None],
                                        p["hy_f_w2"][j], p["hy_f_b2"][j][None], p["hy_f_w3"][j],
                                        p["hy_decay"][j][None], tf, r)
            cw = p["hy_conv_w"][j].reshape(3, 3, HY_WIDTH).transpose(1, 0, 2)
            cb = p["hy_conv_b"][j].reshape(3, HY_WIDTH)
            y = _hyena_conv(u, kf, tf, ti, cw, cb, p["hy_filt_bias"][j], r)
            gate_block = 3 * HY_WIDTH // D_MODEL
            name = "hyena_out"
        else:
            y = _attention(u)
            gate_block = 1
            name = "attn_out"
        if i + 1 < DEPTH:
            x, u = _layer_boundary(y, u, gate_block, x, out_params(i), in_params(i + 1), rope,
                                   name + "_next_in")
        else:
            x = _out_proj_ln(y, u, gate_block, x, *out_params(i), name + "_ln")
    return _from_class_major(x, r)


def kernel(x_prompt, x_sample, hy_w_in, hy_b_in, hy_conv_w, hy_conv_b, hy_f_w1, hy_f_b1, hy_f_freq, hy_f_w2,
           hy_f_b2, hy_f_w3, hy_decay, hy_filt_bias, hy_w_out, hy_b_out, at_w_in, at_b_in, at_q_norm, at_k_norm,
           at_w_out, at_b_out, ln_g, ln_b):
    p = dict(hy_w_in=hy_w_in, hy_b_in=hy_b_in, hy_conv_w=hy_conv_w, hy_conv_b=hy_conv_b, hy_f_w1=hy_f_w1,
             hy_f_b1=hy_f_b1, hy_f_freq=hy_f_freq, hy_f_w2=hy_f_w2, hy_f_b2=hy_f_b2, hy_f_w3=hy_f_w3,
             hy_decay=hy_decay, hy_filt_bias=hy_filt_bias, hy_w_out=hy_w_out, hy_b_out=hy_b_out,
             at_w_in=at_w_in, at_b_in=at_b_in, at_q_norm=at_q_norm, at_k_norm=at_k_norm, at_w_out=at_w_out,
             at_b_out=at_b_out, ln_g=ln_g, ln_b=ln_b)
    return (_trunk(x_prompt, p), _trunk(x_sample, p))
```

```python
import cmath
import functools
import math

import jax
import jax.numpy as jnp
from jax import lax
from jax.experimental import pallas as pl
from jax.experimental.pallas import tpu as pltpu

F32 = jnp.float32
BF16 = jnp.bfloat16

D_MODEL = 1024
DEPTH = 4
GRID_W = 64
HY_WIDTH = D_MODEL
HY_ORDER = 2
POS_EMB_DIM = 33
POS_BANDS = (POS_EMB_DIM - 1) // 2
FILTER_WIDTH = 64
FILTER_EPS = 1e-6
HEAD_DIM = 128
N_Q_HEADS = D_MODEL // HEAD_DIM
N_KV_HEADS = 2
GROUP = N_Q_HEADS // N_KV_HEADS
ATTN_WIDTH = N_Q_HEADS * HEAD_DIM
KV_WIDTH = N_KV_HEADS * HEAD_DIM
AXIS_DIM = HEAD_DIM // 2
ROPE_THETA = 10000.0
RMS_EPS = 1e-6
LN_EPS = 1e-5
DEEPNORM_ALPHA = (2.0 * DEPTH) ** 0.25

V7X_LANES = 128
V7X_SUBLANES = 8
V7X_MXU_DIM = 256
V7X_VMEM_BYTES = 64 * 1024 * 1024

CLASS_ROWS = V7X_MXU_DIM
CONV_TC = V7X_MXU_DIM
SPEC_ROWS = 2 * V7X_SUBLANES
ATTN_TILE_ELEMS = 512 * 4096
OUT_TM = 512
BOUNDARY_SUB_ROWS = 256
HYENA_TM = 1024
FILT_PAD_K = 64
BOUND_SLACK = 1.02
MIN_SAFE_DENOM = 2.0 ** -90


def _cparams(n_axes, vmem_mb):
    return pltpu.CompilerParams(
        dimension_semantics=("arbitrary",) * n_axes,
        vmem_limit_bytes=vmem_mb * 1024 * 1024,
    )


def _const_spec(block, index_map):
    return pl.BlockSpec(block, index_map, pipeline_mode=pl.Buffered(1))


def _add(a, b):
    if a is None:
        return b
    if b is None:
        return a
    return a + b


def _sub(a, b):
    if b is None:
        return a
    if a is None:
        return -b
    return a - b


def _scale(a, c):
    return None if a is None else a * c


def _twiddle(x, w):
    re, im = x
    c, s = w.real, w.imag
    if abs(s) < 1e-12:
        return (re, im) if c > 0 else (_sub(None, re), _sub(None, im))
    if abs(c) < 1e-12:
        return (_sub(None, im), re) if s > 0 else (im, _sub(None, re))
    if abs(abs(c) - abs(s)) < 1e-12:
        a = abs(c)
        total = _add(re, im)
        if c > 0 and s > 0:
            return (_scale(_sub(re, im), a), _scale(total, a))
        if c > 0:
            return (_scale(total, a), _scale(_sub(im, re), a))
        if s > 0:
            return (_scale(total, -a), _scale(_sub(re, im), a))
        return (_scale(_sub(im, re), a), _scale(total, -a))
    nre = _sub(_scale(re, c), _scale(im, s))
    nim = _add(_scale(re, s), _scale(im, c))
    return (nre, nim)


def _fft(xs, sign, first_half_only=False):
    n = len(xs)
    if n == 1:
        return list(xs)
    ev = _fft(xs[0::2], sign)
    od = _fft(xs[1::2], sign)
    out = [None] * n
    for k in range(n // 2):
        t = _twiddle(od[k], cmath.exp(sign * 2j * math.pi * k / n))
        out[k] = (_add(ev[k][0], t[0]), _add(ev[k][1], t[1]))
        if not first_half_only:
            out[k + n // 2] = (_sub(ev[k][0], t[0]), _sub(ev[k][1], t[1]))
    return out[: n // 2] if first_half_only else out


def _store_col_tiles(o_ref, rows, acc):
    for c in range(acc.shape[1] // CONV_TC):
        o_ref[0, c, rows, :] = acc[:, c * CONV_TC:(c + 1) * CONV_TC].astype(o_ref.dtype)


def _load_col_tiles(ref, rows):
    return jnp.concatenate([ref[0, c, rows, :] for c in range(ref.shape[1])], axis=1)


def _tile_major_shape(B, L, n):
    return (B, n // CONV_TC, L, CONV_TC)


def _proj_kernel(x_ref, w_ref, b_ref, o_ref):
    x = x_ref[0].astype(BF16)
    acc = jnp.dot(x, w_ref[...], preferred_element_type=F32) + b_ref[...]
    _store_col_tiles(o_ref, slice(None), acc)


def _hyena_proj(x, w, b):
    B, L, _ = x.shape
    n_out = w.shape[1]
    tm = min(L, HYENA_TM)
    return pl.pallas_call(
        _proj_kernel,
        grid=(B, L // tm),
        in_specs=[
            pl.BlockSpec((1, tm, D_MODEL), lambda b, t: (b, t, 0)),
            _const_spec((D_MODEL, n_out), lambda b, t: (0, 0)),
            _const_spec((1, n_out), lambda b, t: (0, 0)),
        ],
        out_specs=pl.BlockSpec((1, n_out // CONV_TC, tm, CONV_TC), lambda b, t: (b, 0, t, 0)),
        out_shape=jax.ShapeDtypeStruct(_tile_major_shape(B, L, n_out), BF16),
        compiler_params=_cparams(2, 48),
        name="hyena_in_proj",
    )(x, w, b)


def _attn_in_epilogue(acc, q_gain, k_gain, cos2, sin2, store):
    def head(col, gain, scale):
        xh = acc[:, col:col + HEAD_DIM]
        ms = jnp.mean(xh * xh, axis=-1, keepdims=True)
        xn = xh * lax.rsqrt(ms + RMS_EPS) * gain
        rot = xn * cos2 + pltpu.roll(xn, AXIS_DIM, axis=1) * sin2
        store(col, rot * scale if scale != 1.0 else rot)

    for h in range(N_Q_HEADS):
        head(h * HEAD_DIM, q_gain, HEAD_DIM ** -0.5 * math.log2(math.e))
    store(ATTN_WIDTH, acc[:, ATTN_WIDTH:2 * ATTN_WIDTH])
    for h in range(N_KV_HEADS):
        head(2 * ATTN_WIDTH + h * HEAD_DIM, k_gain, 1.0)
    v0 = 2 * ATTN_WIDTH + KV_WIDTH
    store(v0, acc[:, v0:v0 + KV_WIDTH])


def _gated_out_ln(y, g, x, w_ref, b_ref, lg_ref, lb_ref):
    y = y.astype(F32)
    g = g.astype(F32)
    yg = (y * (g * jax.nn.sigmoid(g))).astype(BF16)
    s = jnp.dot(yg, w_ref[...], preferred_element_type=F32) + b_ref[...]
    v = DEEPNORM_ALPHA * x + s
    mu = jnp.mean(v, axis=-1, keepdims=True)
    vc = v - mu
    var = jnp.mean(vc * vc, axis=-1, keepdims=True)
    return vc * lax.rsqrt(var + LN_EPS) * lg_ref[...] + lb_ref[...]


def _boundary_kernel(*refs, attn_next, sub_rows):
    y_ref, g_ref, x_ref, wo_ref, bo_ref, lg_ref, lb_ref, wi_ref, bi_ref = refs[:9]
    if attn_next:
        qn_ref, kn_ref, cos_ref, sin_ref, xo_ref, u_ref = refs[9:]
    else:
        xo_ref, u_ref = refs[9:]
    sub_tiles = [slice(r0, r0 + sub_rows) for r0 in range(0, x_ref.shape[1], sub_rows)]
    normed = []
    for rows in sub_tiles:
        if attn_next:
            y, g = _load_col_tiles(y_ref, rows), _load_col_tiles(g_ref, rows)
        else:
            y, g = y_ref[0, rows, :], g_ref[0, rows, :]
        xn = _gated_out_ln(y, g, x_ref[0, rows, :], wo_ref, bo_ref, lg_ref, lb_ref)
        xo_ref[0, rows, :] = xn
        normed.append(xn.astype(BF16))
    for rows, xb in zip(sub_tiles, normed):
        acc = jnp.dot(xb, wi_ref[...], preferred_element_type=F32) + bi_ref[...]
        if attn_next:
            def store(col, val, rows=rows):
                u_ref[0, rows, col:col + val.shape[1]] = val.astype(u_ref.dtype)

            _attn_in_epilogue(acc, qn_ref[...], kn_ref[...], cos_ref[rows, :], sin_ref[rows, :], store)
        else:
            _store_col_tiles(u_ref, rows, acc)


def _layer_boundary(y, u, gate_block, x, out_params, in_params, rope, name):
    B, L, _ = x.shape
    attn_next = len(in_params) == 4
    n_next = in_params[0].shape[1]
    tm = min(L, 2 * OUT_TM if attn_next else OUT_TM)
    x_spec = pl.BlockSpec((1, tm, D_MODEL), lambda b_, t: (b_, t, 0))
    zero = lambda b_, t: (0, 0)
    if attn_next:
        tiles = D_MODEL // CONV_TC
        y_spec = pl.BlockSpec((1, tiles, tm, CONV_TC), lambda b_, t: (b_, 0, t, 0))
        g_spec = pl.BlockSpec((1, tiles, tm, CONV_TC), lambda b_, t: (b_, gate_block, t, 0))
        u_spec = pl.BlockSpec((1, tm, n_next), lambda b_, t: (b_, t, 0))
        u_shape = (B, L, n_next)
    else:
        y_spec = x_spec
        g_spec = pl.BlockSpec((1, tm, D_MODEL), lambda b_, t: (b_, t, gate_block))
        u_spec = pl.BlockSpec((1, n_next // CONV_TC, tm, CONV_TC), lambda b_, t: (b_, 0, t, 0))
        u_shape = _tile_major_shape(B, L, n_next)
    in_specs = [
        y_spec,
        g_spec,
        x_spec,
        _const_spec((D_MODEL, D_MODEL), zero),
        _const_spec((1, D_MODEL), zero),
        _const_spec((1, D_MODEL), zero),
        _const_spec((1, D_MODEL), zero),
        _const_spec((D_MODEL, n_next), zero),
        _const_spec((1, n_next), zero),
    ]
    args = [y, u, x, *out_params, *in_params]
    if attn_next:
        in_specs += [
            _const_spec((1, HEAD_DIM), zero),
            _const_spec((1, HEAD_DIM), zero),
            pl.BlockSpec((tm, HEAD_DIM), lambda b_, t: (t, 0)),
            pl.BlockSpec((tm, HEAD_DIM), lambda b_, t: (t, 0)),
        ]
        args += list(rope)
    return pl.pallas_call(
        functools.partial(_boundary_kernel, attn_next=attn_next, sub_rows=BOUNDARY_SUB_ROWS),
        grid=(B, L // tm),
        in_specs=in_specs,
        out_specs=[x_spec, u_spec],
        out_shape=[jax.ShapeDtypeStruct(x.shape, F32), jax.ShapeDtypeStruct(u_shape, BF16)],
        compiler_params=_cparams(2, 56),
        name=name,
    )(*args)


def _out_kernel(y_ref, g_ref, x_ref, w_ref, b_ref, lg_ref, lb_ref, o_ref):
    o_ref[0] = _gated_out_ln(y_ref[0], g_ref[0], x_ref[0], w_ref, b_ref, lg_ref, lb_ref)


def _out_proj_ln(y, u, gate_block, x, w, b, lg, lb, name):
    B, L, _ = x.shape
    tm = min(L, 2 * OUT_TM)
    x_spec = pl.BlockSpec((1, tm, D_MODEL), lambda b_, t: (b_, t, 0))
    zero = lambda b_, t: (0, 0)
    return pl.pallas_call(
        _out_kernel,
        grid=(B, L // tm),
        in_specs=[
            x_spec,
            pl.BlockSpec((1, tm, D_MODEL), lambda b_, t: (b_, t, gate_block)),
            x_spec,
            _const_spec((D_MODEL, D_MODEL), zero),
            _const_spec((1, D_MODEL), zero),
            _const_spec((1, D_MODEL), zero),
            _const_spec((1, D_MODEL), zero),
        ],
        out_specs=x_spec,
        out_shape=jax.ShapeDtypeStruct(x.shape, F32),
        compiler_params=_cparams(2, 40),
        name=name,
    )(y, u, x, w, b, lg, lb)


def _class_slices(ref, q, F, rows, lanes):
    return (ref[q, rows, lanes], ref[q, pl.ds(rows.start + F, rows.size), lanes])


def _filter_kernel(feat_ref, w1_ref, b1_ref, fr_ref, w2_ref, b2_ref, w3f_ref, w3b_ref,
                   dcf_ref, dcb_ref, tf_ref, kf_ref, h_s, pa_s, pb_s, *, r, F, tc, inv_len):
    hi = lax.Precision.HIGHEST

    @pl.when((pl.program_id(0) == 0) & (pl.program_id(1) == 0))
    def _():
        fr = fr_ref[...]
        h = jnp.sin(fr * (jnp.dot(feat_ref[...], w1_ref[...], precision=hi, preferred_element_type=F32)
                          + b1_ref[...]))
        h = jnp.sin(fr * (jnp.dot(h, w2_ref[...], precision=hi, preferred_element_type=F32)
                          + b2_ref[...]))
        h_hi = h.astype(BF16)
        h_lo = (h - h_hi.astype(F32)).astype(BF16)
        w = FILTER_WIDTH
        h_s[:, 0:w] = h_hi
        h_s[:, w:2 * w] = h_hi
        h_s[:, 2 * w:3 * w] = h_lo
        h_s[:, 3 * w:4 * w] = jnp.zeros_like(h_lo)

    def last_layer(w_ref):
        w = w_ref[...]
        w_hi = w.astype(BF16)
        w_lo = (w - w_hi.astype(F32)).astype(BF16)
        w_cat = jnp.concatenate([w_hi, w_lo, w_hi, jnp.zeros_like(w_hi)], axis=0)
        return jnp.dot(h_s[...], w_cat, preferred_element_type=F32)

    t = feat_ref[:, 0:1]
    hf = last_layer(w3f_ref) * jnp.exp(-t * jnp.abs(dcf_ref[...]))
    hb = last_layer(w3b_ref) * jnp.exp(-t * jnp.abs(dcb_ref[...]))
    ss = jnp.sum(hf * hf + hb * hb, axis=0, keepdims=True)
    sc = lax.rsqrt(ss + FILTER_EPS) * inv_len
    row = lax.broadcasted_iota(jnp.int32, hb.shape, 0)
    taps = (hf * sc, jnp.where(row == 0, 0.0, hb * sc))

    for h, dst in zip(taps, (pa_s, pb_s)):
        for q in range(r):
            dst[q] = jnp.dot(tf_ref[q], h[q * F:(q + 1) * F, :].astype(BF16), preferred_element_type=F32)

    def body(i, carry):
        rows = pl.ds(pl.multiple_of(i * SPEC_ROWS, SPEC_ROWS), SPEC_ROWS)
        rows_im = pl.ds(rows.start + F, rows.size)
        for lc in range(tc // V7X_LANES):
            lanes = slice(lc * V7X_LANES, (lc + 1) * V7X_LANES)
            e = []
            for q in range(r):
                are, aim = _class_slices(pa_s, q, F, rows, lanes)
                bre, bim = _class_slices(pb_s, (r - q) % r, F, rows, lanes)
                e.append((are + bre, aim - bim))
            x = _fft(e, -1.0)
            for j in range(r):
                kf_ref[0, j, rows, lanes] = x[j][0].astype(kf_ref.dtype)
                kf_ref[0, j, rows_im, lanes] = x[j][1].astype(kf_ref.dtype)
        return carry

    lax.fori_loop(0, F // SPEC_ROWS, body, 0)


def _hyena_filter_spectrum(feats, w1p, b1, fr, w2, b2, w3, decay, tf, r):
    L = feats.shape[0]
    F = L // r
    E = HY_WIDTH
    tc = CONV_TC
    nct = E // tc
    zero = lambda o, c: (0, 0)
    col = lambda d: (lambda o, c: (0, (2 * o + d) * nct + c))
    kern = functools.partial(_filter_kernel, r=r, F=F, tc=tc, inv_len=1.0 / L)
    return pl.pallas_call(
        kern,
        grid=(HY_ORDER, nct),
        in_specs=[
            _const_spec((L, FILT_PAD_K), zero),
            _const_spec((FILT_PAD_K, FILTER_WIDTH), zero),
            _const_spec((1, FILTER_WIDTH), zero),
            _const_spec((1, FILTER_WIDTH), zero),
            _const_spec((FILTER_WIDTH, FILTER_WIDTH), zero),
            _const_spec((1, FILTER_WIDTH), zero),
            pl.BlockSpec((FILTER_WIDTH, tc), col(0)),
            pl.BlockSpec((FILTER_WIDTH, tc), col(1)),
            pl.BlockSpec((1, tc), col(0)),
            pl.BlockSpec((1, tc), col(1)),
            _const_spec((r, 2 * F, F), lambda o, c: (0, 0, 0)),
        ],
        out_specs=pl.BlockSpec((1, r, 2 * F, tc), lambda o, c: (o, 0, 0, c)),
        out_shape=jax.ShapeDtypeStruct((HY_ORDER, r, 2 * F, E), BF16),
        scratch_shapes=[pltpu.VMEM((L, 4 * FILTER_WIDTH), BF16),
                        pltpu.VMEM((r, 2 * F, tc), F32), pltpu.VMEM((r, 2 * F, tc), F32)],
        compiler_params=_cparams(2, 56),
        name="hyena_filter_spectrum",
    )(feats, w1p, b1, fr, w2, b2, w3, w3, decay, decay, tf)


def _short_conv_classes(u_ref, w, b, r, F):
    def cls(k):
        return u_ref[k * F:(k + 1) * F, :].astype(F32)

    first = cls(0)
    row = lax.broadcasted_iota(jnp.int32, first.shape, 0)
    prev = jnp.where(row == 0, 0.0, pltpu.roll(cls(r - 1), 1, axis=0))
    cur = first
    for q in range(r):
        if q < r - 1:
            nxt = cls(q + 1)
        else:
            nxt = jnp.where(row == F - 1, 0.0, pltpu.roll(first, F - 1, axis=0))
        yield q, prev * w[0:1] + cur * w[1:2] + nxt * w[2:3] + b
        prev, cur = cur, nxt


def _conv_kernel(uv_ref, ug0_ref, ug1_ref, kf_ref, tf_ref, ti_ref, cw_ref, cb_ref, fb_ref,
                 out_ref, z_s, p_s, a_s, *, r, F, tc):
    for q, v in _short_conv_classes(uv_ref, cw_ref[0], cb_ref[0:1, :], r, F):
        z_s[q] = v

    gate_refs = (ug0_ref, ug1_ref)
    for o in range(HY_ORDER):
        def spectral(rows, rows_im, o=o):
            for lc in range(tc // V7X_LANES):
                lanes = slice(lc * V7X_LANES, (lc + 1) * V7X_LANES)
                x = _fft([(p_s[q, rows, lanes], p_s[q, rows_im, lanes]) for q in range(r)], -1.0)
                y = []
                for j in range(r):
                    kre = kf_ref[o, j, rows, lanes]
                    kim = kf_ref[o, j, rows_im, lanes]
                    xre, xim = x[j]
                    y.append((xre * kre - xim * kim, xre * kim + xim * kre))
                a = _fft(y, 1.0)
                for q in range(r):
                    a_s[q, rows, lanes] = a[q][0]
                    a_s[q, rows_im, lanes] = a[q][1]

        for q in range(r):
            p_s[q] = jnp.dot(tf_ref[q], z_s[q].astype(BF16), preferred_element_type=F32).astype(p_s.dtype)

        def body(i, carry):
            rows = pl.ds(pl.multiple_of(i * SPEC_ROWS, SPEC_ROWS), SPEC_ROWS)
            spectral(rows, pl.ds(rows.start + F, rows.size))
            return carry

        lax.fori_loop(0, F // SPEC_ROWS, body, 0)

        gates = _short_conv_classes(gate_refs[o], cw_ref[1 + o], cb_ref[1 + o:2 + o, :], r, F)
        for q, g in gates:
            y = jnp.dot(ti_ref[q], a_s[q], preferred_element_type=F32)
            z_new = g * (y + z_s[q] * fb_ref[o:o + 1, :])
            if o == HY_ORDER - 1:
                out_ref[q * F:(q + 1) * F, :] = z_new.astype(out_ref.dtype)
            else:
                z_s[q] = z_new


def _hyena_conv(u, kf, tf, ti, cw, cb, fb, r):
    B, _, L, _ = u.shape
    F = L // r
    E = HY_WIDTH
    tc = CONV_TC
    nct = E // tc
    kern = functools.partial(_conv_kernel, r=r, F=F, tc=tc)
    sec = lambda s: pl.BlockSpec((None, None, L, tc), lambda c, b: (b, s * nct + c, 0, 0))
    return pl.pallas_call(
        kern,
        grid=(nct, B),
        in_specs=[
            sec(0), sec(1), sec(2),
            _const_spec((HY_ORDER, r, 2 * F, tc), lambda c, b: (0, 0, 0, c)),
            _const_spec((r, 2 * F, F), lambda c, b: (0, 0, 0)),
            _const_spec((r, F, 2 * F), lambda c, b: (0, 0, 0)),
            _const_spec((3, 3, tc), lambda c, b: (0, 0, c)),
            _const_spec((3, tc), lambda c, b: (0, c)),
            _const_spec((HY_ORDER, tc), lambda c, b: (0, c)),
        ],
        out_specs=pl.BlockSpec((None, None, L, tc), lambda c, b: (b, c, 0, 0)),
        out_shape=jax.ShapeDtypeStruct(_tile_major_shape(B, L, E), BF16),
        scratch_shapes=[pltpu.VMEM((r, F, tc), F32), pltpu.VMEM((r, 2 * F, tc), BF16),
                        pltpu.VMEM((r, 2 * F, tc), BF16)],
        compiler_params=_cparams(2, 56),
        name="hyena_long_conv",
    )(u, u, u, kf, tf, ti, cw, cb, fb)


def _attn_kernel(bound_ref, q_ref, k_ref, v_ref, o_ref, vt_s, p_s):
    @pl.when(pl.program_id(2) == 0)
    def _():
        vt_s[...] = v_ref[0].T

    k = k_ref[0]
    bound = bound_ref[...]
    dims_nt = (((1,), (1,)), ((), ()))

    def q_head(g):
        return q_ref[0, :, g * HEAD_DIM:(g + 1) * HEAD_DIM]

    def finish(g, p_bf16, denom):
        ot = jnp.dot(vt_s[...], p_bf16, preferred_element_type=F32) * (1.0 / denom)
        o_ref[0, :, g * HEAD_DIM:(g + 1) * HEAD_DIM] = ot.T.astype(o_ref.dtype)

    denoms = []
    for g in range(GROUP):
        st = lax.dot_general(k, q_head(g), dims_nt, preferred_element_type=F32)
        p = jnp.exp2(st - bound)
        denoms.append(jnp.sum(p, axis=0, keepdims=True))
        p_s[g] = p.astype(BF16)
    for g in range(GROUP):
        finish(g, p_s[g], denoms[g])

    smallest = jnp.min(jnp.minimum(jnp.minimum(denoms[0], denoms[1]), jnp.minimum(denoms[2], denoms[3])))

    @pl.when(jnp.logical_not(smallest >= MIN_SAFE_DENOM))
    def _():
        for g in range(GROUP):
            st = lax.dot_general(k, q_head(g), dims_nt, preferred_element_type=F32)
            p = jnp.exp2(st - jnp.max(st, axis=0, keepdims=True))
            finish(g, p.astype(BF16), jnp.sum(p, axis=0, keepdims=True))


def _score_bound(q_gain, k_gain):
    unit = BOUND_SLACK * math.log2(math.e) * HEAD_DIM ** 0.5
    return unit * jnp.max(jnp.abs(q_gain)) * jnp.max(jnp.abs(k_gain))


def _attention(u, q_gain, k_gain):
    B, L, _ = u.shape
    tq = min(L, ATTN_TILE_ELEMS // L)
    bound = jnp.full((1, tq), _score_bound(q_gain, k_gain), F32)
    gw = GROUP * HEAD_DIM
    k_block = 2 * ATTN_WIDTH // HEAD_DIM
    v_block = (2 * ATTN_WIDTH + KV_WIDTH) // HEAD_DIM
    return pl.pallas_call(
        _attn_kernel,
        grid=(B, N_KV_HEADS, L // tq),
        in_specs=[
            _const_spec((1, tq), lambda b, h, t: (0, 0)),
            pl.BlockSpec((1, tq, gw), lambda b, h, t: (b, t, h)),
            pl.BlockSpec((1, L, HEAD_DIM), lambda b, h, t: (b, 0, k_block + h)),
            pl.BlockSpec((1, L, HEAD_DIM), lambda b, h, t: (b, 0, v_block + h)),
        ],
        out_specs=pl.BlockSpec((1, tq, gw), lambda b, h, t: (b, t, h)),
        out_shape=jax.ShapeDtypeStruct((B, L, ATTN_WIDTH), BF16),
        scratch_shapes=[pltpu.VMEM((HEAD_DIM, L), BF16),
                        pltpu.VMEM((GROUP, L, tq), BF16)],
        compiler_params=_cparams(3, 56),
        name="gqa_attention",
    )(bound, u, u, u)


def _class_dft_tables(L, r):
    F = L // r
    f = jnp.arange(F, dtype=jnp.int32)
    m = jnp.arange(F, dtype=jnp.int32)
    q = jnp.arange(r, dtype=jnp.int32)
    n = ((2 * f + 1)[None, :, None] * (r * m[None, None, :] + q[:, None, None])) % (4 * L)
    theta = n.astype(F32) * (math.pi / (2 * L))
    c = jnp.cos(theta)
    s = jnp.sin(theta)
    tf = jnp.concatenate([c, -s], axis=1).astype(BF16)
    return tf, tf.transpose(0, 2, 1)


def _filter_features(L):
    t = jnp.linspace(0.0, 1.0, L, dtype=F32)[:, None]
    w = 2.0 * math.pi * jnp.arange(L, dtype=F32)[:, None] / L
    f = jnp.linspace(1e-4, POS_BANDS - 1, POS_BANDS, dtype=F32)[None]
    feats = jnp.concatenate([t, jnp.cos(f * w), -jnp.sin(f * w)], axis=-1)
    return jnp.pad(feats, ((0, 0), (0, FILT_PAD_K - POS_EMB_DIM)))


def _rope_tables(L):
    rows = L // GRID_W
    row = jnp.repeat(jnp.arange(rows, dtype=F32), GRID_W)
    col = jnp.tile(jnp.arange(GRID_W, dtype=F32), rows)
    inv = ROPE_THETA ** (-jnp.arange(0, AXIS_DIM, 2, dtype=F32) / AXIS_DIM)
    ang = jnp.concatenate([row[:, None] * inv, col[:, None] * inv], axis=-1)
    c, s = jnp.cos(ang), jnp.sin(ang)
    return jnp.concatenate([c, c], axis=-1), jnp.concatenate([-s, s], axis=-1)


def _to_class_major(a, r):
    lead, L, C = a.shape[:-2], a.shape[-2], a.shape[-1]
    return a.reshape(lead + (L // r, r, C)).swapaxes(-3, -2).reshape(a.shape)


def _from_class_major(a, r):
    lead, L, C = a.shape[:-2], a.shape[-2], a.shape[-1]
    return a.reshape(lead + (r, L // r, C)).swapaxes(-3, -2).reshape(a.shape)


def _attn_in_params(p, j):
    w = p["at_w_in"][j]
    bi = p["at_b_in"][j]
    kv0 = ATTN_WIDTH
    g0 = ATTN_WIDTH + 2 * KV_WIDTH
    w = jnp.concatenate([w[:, :kv0], w[:, g0:], w[:, kv0:g0]], axis=1).astype(BF16)
    bi = jnp.concatenate([bi[:kv0], bi[g0:], bi[kv0:g0]])[None]
    return (w, bi, p["at_q_norm"][j][None], p["at_k_norm"][j][None])


def _trunk(x, p):
    B, L, _ = x.shape
    r = L // CLASS_ROWS
    tf, ti = _class_dft_tables(L, r)
    feats = _to_class_major(_filter_features(L), r)
    rope = tuple(_to_class_major(t, r) for t in _rope_tables(L))
    x = _to_class_major(x, r)

    def in_params(i):
        j = i // 2
        if i % 2 == 0:
            return (p["hy_w_in"][j].astype(BF16), p["hy_b_in"][j][None])
        return _attn_in_params(p, j)

    def out_params(i):
        j = i // 2
        w, b = (p["hy_w_out"], p["hy_b_out"]) if i % 2 == 0 else (p["at_w_out"], p["at_b_out"])
        return (w[j].astype(BF16), b[j][None], p["ln_g"][i][None], p["ln_b"][i][None])

    u = _hyena_proj(x, *in_params(0))
    for i in range(DEPTH):
        j = i // 2
        if i % 2 == 0:
            w1p = jnp.pad(p["hy_f_w1"][j], ((0, FILT_PAD_K - POS_EMB_DIM), (0, 0)))
            kf = _hyena_filter_spectrum(feats, w1p, p["hy_f_b1"][j][None], p["hy_f_freq"][j][None],
                                        p["hy_f_w2"][j], p["hy_f_b2"][j][None], p["hy_f_w3"][j],
                                        p["hy_decay"][j][None], tf, r)
            cw = p["hy_conv_w"][j].reshape(3, 3, HY_WIDTH).transpose(1, 0, 2)
            cb = p["hy_conv_b"][j].reshape(3, HY_WIDTH)
            y = _hyena_conv(u, kf, tf, ti, cw, cb, p["hy_filt_bias"][j], r)
            gate_block = 3 * HY_WIDTH // D_MODEL
            name = "hyena_out"
        else:
            y = _attention(u, p["at_q_norm"][j], p["at_k_norm"][j])
            gate_block = 1
            name = "attn_out"
        if i + 1 < DEPTH:
            x, u = _layer_boundary(y, u, gate_block, x, out_params(i), in_params(i + 1), rope,
                                   name + "_next_in")
        else:
            x = _out_proj_ln(y, u, gate_block, x, *out_params(i), name + "_ln")
    return _from_class_major(x, r)


def kernel(x_prompt, x_sample, hy_w_in, hy_b_in, hy_conv_w, hy_conv_b, hy_f_w1, hy_f_b1, hy_f_freq, hy_f_w2,
           hy_f_b2, hy_f_w3, hy_decay, hy_filt_bias, hy_w_out, hy_b_out, at_w_in, at_b_in, at_q_norm, at_k_norm,
           at_w_out, at_b_out, ln_g, ln_b):
    p = dict(hy_w_in=hy_w_in, hy_b_in=hy_b_in, hy_conv_w=hy_conv_w, hy_conv_b=hy_conv_b, hy_f_w1=hy_f_w1,
             hy_f_b1=hy_f_b1, hy_f_freq=hy_f_freq, hy_f_w2=hy_f_w2, hy_f_b2=hy_f_b2, hy_f_w3=hy_f_w3,
             hy_decay=hy_decay, hy_filt_bias=hy_filt_bias, hy_w_out=hy_w_out, hy_b_out=hy_b_out,
             at_w_in=at_w_in, at_b_in=at_b_in, at_q_norm=at_q_norm, at_k_norm=at_k_norm, at_w_out=at_w_out,
             at_b_out=at_b_out, ln_g=ln_g, ln_b=ln_b)
    return (_trunk(x_prompt, p), _trunk(x_sample, p))
```

```python
import cmath
import functools
import math

import jax
import jax.numpy as jnp
from jax import lax
from jax.experimental import pallas as pl
from jax.experimental.pallas import tpu as pltpu

F32 = jnp.float32
BF16 = jnp.bfloat16

D_MODEL = 1024
DEPTH = 4
GRID_W = 64
HY_WIDTH = D_MODEL
HY_ORDER = 2
POS_EMB_DIM = 33
POS_BANDS = (POS_EMB_DIM - 1) // 2
FILTER_WIDTH = 64
FILTER_EPS = 1e-6
HEAD_DIM = 128
N_Q_HEADS = D_MODEL // HEAD_DIM
N_KV_HEADS = 2
GROUP = N_Q_HEADS // N_KV_HEADS
ATTN_WIDTH = N_Q_HEADS * HEAD_DIM
KV_WIDTH = N_KV_HEADS * HEAD_DIM
AXIS_DIM = HEAD_DIM // 2
ROPE_THETA = 10000.0
RMS_EPS = 1e-6
LN_EPS = 1e-5
DEEPNORM_ALPHA = (2.0 * DEPTH) ** 0.25

V7X_LANES = 128
V7X_SUBLANES = 8
V7X_MXU_DIM = 256
V7X_VMEM_BYTES = 64 * 1024 * 1024

CLASS_ROWS = V7X_MXU_DIM
CONV_TC = V7X_MXU_DIM
SPEC_ROWS = 2 * V7X_SUBLANES
ATTN_TILE_ELEMS = 512 * 4096
OUT_TM = 512
BOUNDARY_SUB_ROWS = 256
HYENA_TM = 1024
FILT_PAD_K = 64
BOUND_SLACK = 1.02
MIN_SAFE_DENOM = 2.0 ** -90


def _cparams(n_axes, vmem_mb):
    return pltpu.CompilerParams(
        dimension_semantics=("arbitrary",) * n_axes,
        vmem_limit_bytes=vmem_mb * 1024 * 1024,
    )


def _const_spec(block, index_map):
    return pl.BlockSpec(block, index_map, pipeline_mode=pl.Buffered(1))


def _add(a, b):
    if a is None:
        return b
    if b is None:
        return a
    return a + b


def _sub(a, b):
    if b is None:
        return a
    if a is None:
        return -b
    return a - b


def _scale(a, c):
    return None if a is None else a * c


def _twiddle(x, w):
    re, im = x
    c, s = w.real, w.imag
    if abs(s) < 1e-12:
        return (re, im) if c > 0 else (_sub(None, re), _sub(None, im))
    if abs(c) < 1e-12:
        return (_sub(None, im), re) if s > 0 else (im, _sub(None, re))
    if abs(abs(c) - abs(s)) < 1e-12:
        a = abs(c)
        total = _add(re, im)
        if c > 0 and s > 0:
            return (_scale(_sub(re, im), a), _scale(total, a))
        if c > 0:
            return (_scale(total, a), _scale(_sub(im, re), a))
        if s > 0:
            return (_scale(total, -a), _scale(_sub(re, im), a))
        return (_scale(_sub(im, re), a), _scale(total, -a))
    nre = _sub(_scale(re, c), _scale(im, s))
    nim = _add(_scale(re, s), _scale(im, c))
    return (nre, nim)


def _fft(xs, sign, first_half_only=False):
    n = len(xs)
    if n == 1:
        return list(xs)
    ev = _fft(xs[0::2], sign)
    od = _fft(xs[1::2], sign)
    out = [None] * n
    for k in range(n // 2):
        t = _twiddle(od[k], cmath.exp(sign * 2j * math.pi * k / n))
        out[k] = (_add(ev[k][0], t[0]), _add(ev[k][1], t[1]))
        if not first_half_only:
            out[k + n // 2] = (_sub(ev[k][0], t[0]), _sub(ev[k][1], t[1]))
    return out[: n // 2] if first_half_only else out


def _store_col_tiles(o_ref, rows, acc):
    for c in range(acc.shape[1] // CONV_TC):
        o_ref[0, c, rows, :] = acc[:, c * CONV_TC:(c + 1) * CONV_TC].astype(o_ref.dtype)


def _load_col_tiles(ref, rows):
    return jnp.concatenate([ref[0, c, rows, :] for c in range(ref.shape[1])], axis=1)


def _tile_major_shape(B, L, n):
    return (B, n // CONV_TC, L, CONV_TC)


def _proj_kernel(x_ref, w_ref, b_ref, o_ref):
    x = x_ref[0].astype(BF16)
    acc = jnp.dot(x, w_ref[...], preferred_element_type=F32) + b_ref[...]
    _store_col_tiles(o_ref, slice(None), acc)


def _hyena_proj(x, w, b):
    B, L, _ = x.shape
    n_out = w.shape[1]
    tm = min(L, HYENA_TM)
    return pl.pallas_call(
        _proj_kernel,
        grid=(B, L // tm),
        in_specs=[
            pl.BlockSpec((1, tm, D_MODEL), lambda b, t: (b, t, 0)),
            _const_spec((D_MODEL, n_out), lambda b, t: (0, 0)),
            _const_spec((1, n_out), lambda b, t: (0, 0)),
        ],
        out_specs=pl.BlockSpec((1, n_out // CONV_TC, tm, CONV_TC), lambda b, t: (b, 0, t, 0)),
        out_shape=jax.ShapeDtypeStruct(_tile_major_shape(B, L, n_out), BF16),
        compiler_params=_cparams(2, 48),
        name="hyena_in_proj",
    )(x, w, b)


def _attn_in_epilogue(acc, q_gain, k_gain, cos2, sin2, store):
    def head(col, gain, scale):
        xh = acc[:, col:col + HEAD_DIM]
        ms = jnp.mean(xh * xh, axis=-1, keepdims=True)
        xn = xh * lax.rsqrt(ms + RMS_EPS) * gain
        rot = xn * cos2 + pltpu.roll(xn, AXIS_DIM, axis=1) * sin2
        store(col, rot * scale if scale != 1.0 else rot)

    for h in range(N_Q_HEADS):
        head(h * HEAD_DIM, q_gain, HEAD_DIM ** -0.5 * math.log2(math.e))
    store(ATTN_WIDTH, acc[:, ATTN_WIDTH:2 * ATTN_WIDTH])
    for h in range(N_KV_HEADS):
        head(2 * ATTN_WIDTH + h * HEAD_DIM, k_gain, 1.0)
    v0 = 2 * ATTN_WIDTH + KV_WIDTH
    store(v0, acc[:, v0:v0 + KV_WIDTH])


def _gated_out_ln(y, g, x, w_ref, b_ref, lg_ref, lb_ref):
    y = y.astype(F32)
    g = g.astype(F32)
    yg = (y * (g * jax.nn.sigmoid(g))).astype(BF16)
    s = jnp.dot(yg, w_ref[...], preferred_element_type=F32) + b_ref[...]
    v = DEEPNORM_ALPHA * x + s
    mu = jnp.mean(v, axis=-1, keepdims=True)
    vc = v - mu
    var = jnp.mean(vc * vc, axis=-1, keepdims=True)
    return vc * lax.rsqrt(var + LN_EPS) * lg_ref[...] + lb_ref[...]


def _boundary_kernel(*refs, attn_next, sub_rows):
    y_ref, g_ref, x_ref, wo_ref, bo_ref, lg_ref, lb_ref, wi_ref, bi_ref = refs[:9]
    if attn_next:
        qn_ref, kn_ref, cos_ref, sin_ref, xo_ref, u_ref = refs[9:]
    else:
        xo_ref, u_ref = refs[9:]
    sub_tiles = [slice(r0, r0 + sub_rows) for r0 in range(0, x_ref.shape[1], sub_rows)]
    normed = []
    for rows in sub_tiles:
        if attn_next:
            y, g = _load_col_tiles(y_ref, rows), _load_col_tiles(g_ref, rows)
        else:
            y, g = y_ref[0, rows, :], g_ref[0, rows, :]
        xn = _gated_out_ln(y, g, x_ref[0, rows, :], wo_ref, bo_ref, lg_ref, lb_ref)
        xo_ref[0, rows, :] = xn
        normed.append(xn.astype(BF16))
    for rows, xb in zip(sub_tiles, normed):
        acc = jnp.dot(xb, wi_ref[...], preferred_element_type=F32) + bi_ref[...]
        if attn_next:
            def store(col, val, rows=rows):
                u_ref[0, rows, col:col + val.shape[1]] = val.astype(u_ref.dtype)

            _attn_in_epilogue(acc, qn_ref[...], kn_ref[...], cos_ref[rows, :], sin_ref[rows, :], store)
        else:
            _store_col_tiles(u_ref, rows, acc)


def _layer_boundary(y, u, gate_block, x, out_params, in_params, rope, name):
    B, L, _ = x.shape
    attn_next = len(in_params) == 4
    n_next = in_params[0].shape[1]
    tm = min(L, 2 * OUT_TM if attn_next else OUT_TM)
    x_spec = pl.BlockSpec((1, tm, D_MODEL), lambda b_, t: (b_, t, 0))
    zero = lambda b_, t: (0, 0)
    if attn_next:
        tiles = D_MODEL // CONV_TC
        y_spec = pl.BlockSpec((1, tiles, tm, CONV_TC), lambda b_, t: (b_, 0, t, 0))
        g_spec = pl.BlockSpec((1, tiles, tm, CONV_TC), lambda b_, t: (b_, gate_block, t, 0))
        u_spec = pl.BlockSpec((1, tm, n_next), lambda b_, t: (b_, t, 0))
        u_shape = (B, L, n_next)
    else:
        y_spec = x_spec
        g_spec = pl.BlockSpec((1, tm, D_MODEL), lambda b_, t: (b_, t, gate_block))
        u_spec = pl.BlockSpec((1, n_next // CONV_TC, tm, CONV_TC), lambda b_, t: (b_, 0, t, 0))
        u_shape = _tile_major_shape(B, L, n_next)
    in_specs = [
        y_spec,
        g_spec,
        x_spec,
        _const_spec((D_MODEL, D_MODEL), zero),
        _const_spec((1, D_MODEL), zero),
        _const_spec((1, D_MODEL), zero),
        _const_spec((1, D_MODEL), zero),
        _const_spec((D_MODEL, n_next), zero),
        _const_spec((1, n_next), zero),
    ]
    args = [y, u, x, *out_params, *in_params]
    if attn_next:
        in_specs += [
            _const_spec((1, HEAD_DIM), zero),
            _const_spec((1, HEAD_DIM), zero),
            pl.BlockSpec((tm, HEAD_DIM), lambda b_, t: (t, 0)),
            pl.BlockSpec((tm, HEAD_DIM), lambda b_, t: (t, 0)),
        ]
        args += list(rope)
    return pl.pallas_call(
        functools.partial(_boundary_kernel, attn_next=attn_next, sub_rows=BOUNDARY_SUB_ROWS),
        grid=(B, L // tm),
        in_specs=in_specs,
        out_specs=[x_spec, u_spec],
        out_shape=[jax.ShapeDtypeStruct(x.shape, F32), jax.ShapeDtypeStruct(u_shape, BF16)],
        compiler_params=_cparams(2, 56),
        name=name,
    )(*args)


def _final_out_kernel(y_ref, g_ref, x_ref, w_ref, b_ref, lg_ref, lb_ref, o_ref, *, r):
    mt = y_ref.shape[2]
    rows = lambda ref: jnp.concatenate([ref[0, q] for q in range(r)], axis=0)
    res = _gated_out_ln(rows(y_ref), rows(g_ref), rows(x_ref), w_ref, b_ref, lg_ref, lb_ref)
    for q in range(r):
        o_ref[0, :, q, :] = res[q * mt:(q + 1) * mt]


def _final_out_proj_ln(y, u, gate_block, x, w, b, lg, lb, r, name):
    B, L, _ = x.shape
    F = L // r
    mt = min(F, 2 * OUT_TM // r)
    cm_spec = lambda blk: pl.BlockSpec((1, r, mt, D_MODEL), lambda b_, t: (b_, 0, t, blk))
    zero = lambda b_, t: (0, 0)
    out = pl.pallas_call(
        functools.partial(_final_out_kernel, r=r),
        grid=(B, F // mt),
        in_specs=[
            cm_spec(0),
            cm_spec(gate_block),
            cm_spec(0),
            _const_spec((D_MODEL, D_MODEL), zero),
            _const_spec((1, D_MODEL), zero),
            _const_spec((1, D_MODEL), zero),
            _const_spec((1, D_MODEL), zero),
        ],
        out_specs=pl.BlockSpec((1, mt, r, D_MODEL), lambda b_, t: (b_, t, 0, 0)),
        out_shape=jax.ShapeDtypeStruct((B, F, r, D_MODEL), F32),
        compiler_params=_cparams(2, 40),
        name=name,
    )(y.reshape(B, r, F, D_MODEL), u.reshape(B, r, F, u.shape[-1]), x.reshape(B, r, F, D_MODEL),
      w, b, lg, lb)
    return out.reshape(B, L, D_MODEL)


def _class_slices(ref, q, F, rows, lanes):
    return (ref[q, rows, lanes], ref[q, pl.ds(rows.start + F, rows.size), lanes])


def _filter_kernel(feat_ref, w1_ref, b1_ref, fr_ref, w2_ref, b2_ref, w3f_ref, w3b_ref,
                   dcf_ref, dcb_ref, tf_ref, kf_ref, h_s, pa_s, pb_s, *, r, F, tc, inv_len):
    hi = lax.Precision.HIGHEST

    @pl.when((pl.program_id(0) == 0) & (pl.program_id(1) == 0))
    def _():
        fr = fr_ref[...]
        h = jnp.sin(fr * (jnp.dot(feat_ref[...], w1_ref[...], precision=hi, preferred_element_type=F32)
                          + b1_ref[...]))
        h = jnp.sin(fr * (jnp.dot(h, w2_ref[...], precision=hi, preferred_element_type=F32)
                          + b2_ref[...]))
        h_hi = h.astype(BF16)
        h_lo = (h - h_hi.astype(F32)).astype(BF16)
        w = FILTER_WIDTH
        h_s[:, 0:w] = h_hi
        h_s[:, w:2 * w] = h_hi
        h_s[:, 2 * w:3 * w] = h_lo
        h_s[:, 3 * w:4 * w] = jnp.zeros_like(h_lo)

    def last_layer(w_ref):
        w = w_ref[...]
        w_hi = w.astype(BF16)
        w_lo = (w - w_hi.astype(F32)).astype(BF16)
        w_cat = jnp.concatenate([w_hi, w_lo, w_hi, jnp.zeros_like(w_hi)], axis=0)
        return jnp.dot(h_s[...], w_cat, preferred_element_type=F32)

    t = feat_ref[:, 0:1]
    hf = last_layer(w3f_ref) * jnp.exp(-t * jnp.abs(dcf_ref[...]))
    hb = last_layer(w3b_ref) * jnp.exp(-t * jnp.abs(dcb_ref[...]))
    ss = jnp.sum(hf * hf + hb * hb, axis=0, keepdims=True)
    sc = lax.rsqrt(ss + FILTER_EPS) * inv_len
    row = lax.broadcasted_iota(jnp.int32, hb.shape, 0)
    taps = (hf * sc, jnp.where(row == 0, 0.0, hb * sc))

    for h, dst in zip(taps, (pa_s, pb_s)):
        for q in range(r):
            dst[q] = jnp.dot(tf_ref[q], h[q * F:(q + 1) * F, :].astype(BF16), preferred_element_type=F32)

    def body(i, carry):
        rows = pl.ds(pl.multiple_of(i * SPEC_ROWS, SPEC_ROWS), SPEC_ROWS)
        rows_im = pl.ds(rows.start + F, rows.size)
        for lc in range(tc // V7X_LANES):
            lanes = slice(lc * V7X_LANES, (lc + 1) * V7X_LANES)
            e = []
            for q in range(r):
                are, aim = _class_slices(pa_s, q, F, rows, lanes)
                bre, bim = _class_slices(pb_s, (r - q) % r, F, rows, lanes)
                e.append((are + bre, aim - bim))
            x = _fft(e, -1.0)
            for j in range(r):
                kf_ref[0, j, rows, lanes] = x[j][0].astype(kf_ref.dtype)
                kf_ref[0, j, rows_im, lanes] = x[j][1].astype(kf_ref.dtype)
        return carry

    lax.fori_loop(0, F // SPEC_ROWS, body, 0)


def _hyena_filter_spectrum(feats, w1p, b1, fr, w2, b2, w3, decay, tf, r):
    L = feats.shape[0]
    F = L // r
    E = HY_WIDTH
    tc = CONV_TC
    nct = E // tc
    zero = lambda o, c: (0, 0)
    col = lambda d: (lambda o, c: (0, (2 * o + d) * nct + c))
    kern = functools.partial(_filter_kernel, r=r, F=F, tc=tc, inv_len=1.0 / L)
    return pl.pallas_call(
        kern,
        grid=(HY_ORDER, nct),
        in_specs=[
            _const_spec((L, FILT_PAD_K), zero),
            _const_spec((FILT_PAD_K, FILTER_WIDTH), zero),
            _const_spec((1, FILTER_WIDTH), zero),
            _const_spec((1, FILTER_WIDTH), zero),
            _const_spec((FILTER_WIDTH, FILTER_WIDTH), zero),
            _const_spec((1, FILTER_WIDTH), zero),
            pl.BlockSpec((FILTER_WIDTH, tc), col(0)),
            pl.BlockSpec((FILTER_WIDTH, tc), col(1)),
            pl.BlockSpec((1, tc), col(0)),
            pl.BlockSpec((1, tc), col(1)),
            _const_spec((r, 2 * F, F), lambda o, c: (0, 0, 0)),
        ],
        out_specs=pl.BlockSpec((1, r, 2 * F, tc), lambda o, c: (o, 0, 0, c)),
        out_shape=jax.ShapeDtypeStruct((HY_ORDER, r, 2 * F, E), BF16),
        scratch_shapes=[pltpu.VMEM((L, 4 * FILTER_WIDTH), BF16),
                        pltpu.VMEM((r, 2 * F, tc), F32), pltpu.VMEM((r, 2 * F, tc), F32)],
        compiler_params=_cparams(2, 56),
        name="hyena_filter_spectrum",
    )(feats, w1p, b1, fr, w2, b2, w3, w3, decay, decay, tf)


def _short_conv_classes(u_ref, w, b, r, F):
    def cls(k):
        return u_ref[k * F:(k + 1) * F, :].astype(F32)

    first = cls(0)
    row = lax.broadcasted_iota(jnp.int32, first.shape, 0)
    prev = jnp.where(row == 0, 0.0, pltpu.roll(cls(r - 1), 1, axis=0))
    cur = first
    for q in range(r):
        if q < r - 1:
            nxt = cls(q + 1)
        else:
            nxt = jnp.where(row == F - 1, 0.0, pltpu.roll(first, F - 1, axis=0))
        yield q, prev * w[0:1] + cur * w[1:2] + nxt * w[2:3] + b
        prev, cur = cur, nxt


def _conv_kernel(uv_ref, ug0_ref, ug1_ref, kf_ref, tf_ref, ti_ref, cw_ref, cb_ref, fb_ref,
                 out_ref, z_s, p_s, a_s, *, r, F, tc):
    for q, v in _short_conv_classes(uv_ref, cw_ref[0], cb_ref[0:1, :], r, F):
        z_s[q] = v

    gate_refs = (ug0_ref, ug1_ref)
    for o in range(HY_ORDER):
        def spectral(rows, rows_im, o=o):
            for lc in range(tc // V7X_LANES):
                lanes = slice(lc * V7X_LANES, (lc + 1) * V7X_LANES)
                x = _fft([(p_s[q, rows, lanes], p_s[q, rows_im, lanes]) for q in range(r)], -1.0)
                y = []
                for j in range(r):
                    kre = kf_ref[o, j, rows, lanes]
                    kim = kf_ref[o, j, rows_im, lanes]
                    xre, xim = x[j]
                    y.append((xre * kre - xim * kim, xre * kim + xim * kre))
                a = _fft(y, 1.0)
                for q in range(r):
                    a_s[q, rows, lanes] = a[q][0]
                    a_s[q, rows_im, lanes] = a[q][1]

        for q in range(r):
            p_s[q] = jnp.dot(tf_ref[q], z_s[q].astype(BF16), preferred_element_type=F32).astype(p_s.dtype)

        def body(i, carry):
            rows = pl.ds(pl.multiple_of(i * SPEC_ROWS, SPEC_ROWS), SPEC_ROWS)
            spectral(rows, pl.ds(rows.start + F, rows.size))
            return carry

        lax.fori_loop(0, F // SPEC_ROWS, body, 0)

        gates = _short_conv_classes(gate_refs[o], cw_ref[1 + o], cb_ref[1 + o:2 + o, :], r, F)
        for q, g in gates:
            y = jnp.dot(ti_ref[q], a_s[q], preferred_element_type=F32)
            z_new = g * (y + z_s[q] * fb_ref[o:o + 1, :])
            if o == HY_ORDER - 1:
                out_ref[q * F:(q + 1) * F, :] = z_new.astype(out_ref.dtype)
            else:
                z_s[q] = z_new


def _hyena_conv(u, kf, tf, ti, cw, cb, fb, r):
    B, _, L, _ = u.shape
    F = L // r
    E = HY_WIDTH
    tc = CONV_TC
    nct = E // tc
    kern = functools.partial(_conv_kernel, r=r, F=F, tc=tc)
    sec = lambda s: pl.BlockSpec((None, None, L, tc), lambda c, b: (b, s * nct + c, 0, 0))
    return pl.pallas_call(
        kern,
        grid=(nct, B),
        in_specs=[
            sec(0), sec(1), sec(2),
            _const_spec((HY_ORDER, r, 2 * F, tc), lambda c, b: (0, 0, 0, c)),
            _const_spec((r, 2 * F, F), lambda c, b: (0, 0, 0)),
            _const_spec((r, F, 2 * F), lambda c, b: (0, 0, 0)),
            _const_spec((3, 3, tc), lambda c, b: (0, 0, c)),
            _const_spec((3, tc), lambda c, b: (0, c)),
            _const_spec((HY_ORDER, tc), lambda c, b: (0, c)),
        ],
        out_specs=pl.BlockSpec((None, None, L, tc), lambda c, b: (b, c, 0, 0)),
        out_shape=jax.ShapeDtypeStruct(_tile_major_shape(B, L, E), BF16),
        scratch_shapes=[pltpu.VMEM((r, F, tc), F32), pltpu.VMEM((r, 2 * F, tc), BF16),
                        pltpu.VMEM((r, 2 * F, tc), BF16)],
        compiler_params=_cparams(2, 56),
        name="hyena_long_conv",
    )(u, u, u, kf, tf, ti, cw, cb, fb)


def _attn_kernel(bound_ref, q_ref, k_ref, v_ref, o_ref, vt_s, p_s):
    @pl.when(pl.program_id(2) == 0)
    def _():
        vt_s[...] = v_ref[0].T

    k = k_ref[0]
    bound = bound_ref[...]
    dims_nt = (((1,), (1,)), ((), ()))

    def q_head(g):
        return q_ref[0, :, g * HEAD_DIM:(g + 1) * HEAD_DIM]

    def finish(g, p_bf16, denom):
        ot = jnp.dot(vt_s[...], p_bf16, preferred_element_type=F32) * (1.0 / denom)
        o_ref[0, :, g * HEAD_DIM:(g + 1) * HEAD_DIM] = ot.T.astype(o_ref.dtype)

    denoms = []
    for g in range(GROUP):
        st = lax.dot_general(k, q_head(g), dims_nt, preferred_element_type=F32)
        p = jnp.exp2(st - bound)
        denoms.append(jnp.sum(p, axis=0, keepdims=True))
        p_s[g] = p.astype(BF16)
    for g in range(GROUP):
        finish(g, p_s[g], denoms[g])

    smallest = jnp.min(jnp.minimum(jnp.minimum(denoms[0], denoms[1]), jnp.minimum(denoms[2], denoms[3])))

    @pl.when(jnp.logical_not(smallest >= MIN_SAFE_DENOM))
    def _():
        for g in range(GROUP):
            st = lax.dot_general(k, q_head(g), dims_nt, preferred_element_type=F32)
            p = jnp.exp2(st - jnp.max(st, axis=0, keepdims=True))
            finish(g, p.astype(BF16), jnp.sum(p, axis=0, keepdims=True))


def _score_bound(q_gain, k_gain):
    unit = BOUND_SLACK * math.log2(math.e) * HEAD_DIM ** 0.5
    return unit * jnp.max(jnp.abs(q_gain)) * jnp.max(jnp.abs(k_gain))


def _attention(u, q_gain, k_gain):
    B, L, _ = u.shape
    tq = min(L, ATTN_TILE_ELEMS // L)
    bound = jnp.full((1, tq), _score_bound(q_gain, k_gain), F32)
    gw = GROUP * HEAD_DIM
    k_block = 2 * ATTN_WIDTH // HEAD_DIM
    v_block = (2 * ATTN_WIDTH + KV_WIDTH) // HEAD_DIM
    return pl.pallas_call(
        _attn_kernel,
        grid=(B, N_KV_HEADS, L // tq),
        in_specs=[
            _const_spec((1, tq), lambda b, h, t: (0, 0)),
            pl.BlockSpec((1, tq, gw), lambda b, h, t: (b, t, h)),
            pl.BlockSpec((1, L, HEAD_DIM), lambda b, h, t: (b, 0, k_block + h)),
            pl.BlockSpec((1, L, HEAD_DIM), lambda b, h, t: (b, 0, v_block + h)),
        ],
        out_specs=pl.BlockSpec((1, tq, gw), lambda b, h, t: (b, t, h)),
        out_shape=jax.ShapeDtypeStruct((B, L, ATTN_WIDTH), BF16),
        scratch_shapes=[pltpu.VMEM((HEAD_DIM, L), BF16),
                        pltpu.VMEM((GROUP, L, tq), BF16)],
        compiler_params=_cparams(3, 56),
        name="gqa_attention",
    )(bound, u, u, u)


def _class_dft_tables(L, r):
    F = L // r
    f = jnp.arange(F, dtype=jnp.int32)
    m = jnp.arange(F, dtype=jnp.int32)
    q = jnp.arange(r, dtype=jnp.int32)
    n = ((2 * f + 1)[None, :, None] * (r * m[None, None, :] + q[:, None, None])) % (4 * L)
    theta = n.astype(F32) * (math.pi / (2 * L))
    c = jnp.cos(theta)
    s = jnp.sin(theta)
    tf = jnp.concatenate([c, -s], axis=1).astype(BF16)
    return tf, tf.transpose(0, 2, 1)


def _filter_features(L):
    t = jnp.linspace(0.0, 1.0, L, dtype=F32)[:, None]
    w = 2.0 * math.pi * jnp.arange(L, dtype=F32)[:, None] / L
    f = jnp.linspace(1e-4, POS_BANDS - 1, POS_BANDS, dtype=F32)[None]
    feats = jnp.concatenate([t, jnp.cos(f * w), -jnp.sin(f * w)], axis=-1)
    return jnp.pad(feats, ((0, 0), (0, FILT_PAD_K - POS_EMB_DIM)))


def _rope_tables(L):
    rows = L // GRID_W
    row = jnp.repeat(jnp.arange(rows, dtype=F32), GRID_W)
    col = jnp.tile(jnp.arange(GRID_W, dtype=F32), rows)
    inv = ROPE_THETA ** (-jnp.arange(0, AXIS_DIM, 2, dtype=F32) / AXIS_DIM)
    ang = jnp.concatenate([row[:, None] * inv, col[:, None] * inv], axis=-1)
    c, s = jnp.cos(ang), jnp.sin(ang)
    return jnp.concatenate([c, c], axis=-1), jnp.concatenate([-s, s], axis=-1)


def _to_class_major(a, r):
    lead, L, C = a.shape[:-2], a.shape[-2], a.shape[-1]
    return a.reshape(lead + (L // r, r, C)).swapaxes(-3, -2).reshape(a.shape)


def _attn_in_params(p, j):
    w = p["at_w_in"][j]
    bi = p["at_b_in"][j]
    kv0 = ATTN_WIDTH
    g0 = ATTN_WIDTH + 2 * KV_WIDTH
    w = jnp.concatenate([w[:, :kv0], w[:, g0:], w[:, kv0:g0]], axis=1).astype(BF16)
    bi = jnp.concatenate([bi[:kv0], bi[g0:], bi[kv0:g0]])[None]
    return (w, bi, p["at_q_norm"][j][None], p["at_k_norm"][j][None])


def _trunk(x, p):
    B, L, _ = x.shape
    r = L // CLASS_ROWS
    tf, ti = _class_dft_tables(L, r)
    feats = _to_class_major(_filter_features(L), r)
    rope = tuple(_to_class_major(t, r) for t in _rope_tables(L))
    x = _to_class_major(x, r)

    def in_params(i):
        j = i // 2
        if i % 2 == 0:
            return (p["hy_w_in"][j].astype(BF16), p["hy_b_in"][j][None])
        return _attn_in_params(p, j)

    def out_params(i):
        j = i // 2
        w, b = (p["hy_w_out"], p["hy_b_out"]) if i % 2 == 0 else (p["at_w_out"], p["at_b_out"])
        return (w[j].astype(BF16), b[j][None], p["ln_g"][i][None], p["ln_b"][i][None])

    u = _hyena_proj(x, *in_params(0))
    for i in range(DEPTH):
        j = i // 2
        if i % 2 == 0:
            w1p = jnp.pad(p["hy_f_w1"][j], ((0, FILT_PAD_K - POS_EMB_DIM), (0, 0)))
            kf = _hyena_filter_spectrum(feats, w1p, p["hy_f_b1"][j][None], p["hy_f_freq"][j][None],
                                        p["hy_f_w2"][j], p["hy_f_b2"][j][None], p["hy_f_w3"][j],
                                        p["hy_decay"][j][None], tf, r)
            cw = p["hy_conv_w"][j].reshape(3, 3, HY_WIDTH).transpose(1, 0, 2)
            cb = p["hy_conv_b"][j].reshape(3, HY_WIDTH)
            y = _hyena_conv(u, kf, tf, ti, cw, cb, p["hy_filt_bias"][j], r)
            gate_block = 3 * HY_WIDTH // D_MODEL
            name = "hyena_out"
        else:
            y = _attention(u, p["at_q_norm"][j], p["at_k_norm"][j])
            gate_block = 1
            name = "attn_out"
        if i + 1 < DEPTH:
            x, u = _layer_boundary(y, u, gate_block, x, out_params(i), in_params(i + 1), rope,
                                   name + "_next_in")
        else:
            x = _final_out_proj_ln(y, u, gate_block, x, *out_params(i), r, name + "_ln")
    return x


def kernel(x_prompt, x_sample, hy_w_in, hy_b_in, hy_conv_w, hy_conv_b, hy_f_w1, hy_f_b1, hy_f_freq, hy_f_w2,
           hy_f_b2, hy_f_w3, hy_decay, hy_filt_bias, hy_w_out, hy_b_out, at_w_in, at_b_in, at_q_norm, at_k_norm,
           at_w_out, at_b_out, ln_g, ln_b):
    p = dict(hy_w_in=hy_w_in, hy_b_in=hy_b_in, hy_conv_w=hy_conv_w, hy_conv_b=hy_conv_b, hy_f_w1=hy_f_w1,
             hy_f_b1=hy_f_b1, hy_f_freq=hy_f_freq, hy_f_w2=hy_f_w2, hy_f_b2=hy_f_b2, hy_f_w3=hy_f_w3,
             hy_decay=hy_decay, hy_filt_bias=hy_filt_bias, hy_w_out=hy_w_out, hy_b_out=hy_b_out,
             at_w_in=at_w_in, at_b_in=at_b_in, at_q_norm=at_q_norm, at_k_norm=at_k_norm, at_w_out=at_w_out,
             at_b_out=at_b_out, ln_g=ln_g, ln_b=ln_b)
    return (_trunk(x_prompt, p), _trunk(x_sample, p))
```

```python
import cmath
import functools
import math

import jax
import jax.numpy as jnp
from jax import lax
from jax.experimental import pallas as pl
from jax.experimental.pallas import tpu as pltpu

F32 = jnp.float32
BF16 = jnp.bfloat16

D_MODEL = 1024
DEPTH = 4
GRID_W = 64
HY_WIDTH = D_MODEL
HY_ORDER = 2
POS_EMB_DIM = 33
POS_BANDS = (POS_EMB_DIM - 1) // 2
FILTER_WIDTH = 64
FILTER_EPS = 1e-6
HEAD_DIM = 128
N_Q_HEADS = D_MODEL // HEAD_DIM
N_KV_HEADS = 2
GROUP = N_Q_HEADS // N_KV_HEADS
ATTN_WIDTH = N_Q_HEADS * HEAD_DIM
KV_WIDTH = N_KV_HEADS * HEAD_DIM
AXIS_DIM = HEAD_DIM // 2
ROPE_THETA = 10000.0
RMS_EPS = 1e-6
LN_EPS = 1e-5
DEEPNORM_ALPHA = (2.0 * DEPTH) ** 0.25

V7X_LANES = 128
V7X_SUBLANES = 8
V7X_MXU_DIM = 256
V7X_VMEM_BYTES = 64 * 1024 * 1024

CLASS_ROWS = V7X_MXU_DIM
CONV_TC = V7X_MXU_DIM
SPEC_ROWS = 2 * V7X_SUBLANES
ATTN_TILE_ELEMS = 512 * 4096
OUT_TM = 512
BOUNDARY_SUB_ROWS = 256
HYENA_TM = 1024
FILT_PAD_K = 64
BOUND_SLACK = 1.02
MIN_SAFE_DENOM = 2.0 ** -90


def _cparams(n_axes, vmem_mb):
    return pltpu.CompilerParams(
        dimension_semantics=("arbitrary",) * n_axes,
        vmem_limit_bytes=vmem_mb * 1024 * 1024,
    )


def _const_spec(block, index_map):
    return pl.BlockSpec(block, index_map, pipeline_mode=pl.Buffered(1))


def _add(a, b):
    if a is None:
        return b
    if b is None:
        return a
    return a + b


def _sub(a, b):
    if b is None:
        return a
    if a is None:
        return -b
    return a - b


def _scale(a, c):
    return None if a is None else a * c


def _twiddle(x, w):
    re, im = x
    c, s = w.real, w.imag
    if abs(s) < 1e-12:
        return (re, im) if c > 0 else (_sub(None, re), _sub(None, im))
    if abs(c) < 1e-12:
        return (_sub(None, im), re) if s > 0 else (im, _sub(None, re))
    if abs(abs(c) - abs(s)) < 1e-12:
        a = abs(c)
        total = _add(re, im)
        if c > 0 and s > 0:
            return (_scale(_sub(re, im), a), _scale(total, a))
        if c > 0:
            return (_scale(total, a), _scale(_sub(im, re), a))
        if s > 0:
            return (_scale(total, -a), _scale(_sub(re, im), a))
        return (_scale(_sub(im, re), a), _scale(total, -a))
    nre = _sub(_scale(re, c), _scale(im, s))
    nim = _add(_scale(re, s), _scale(im, c))
    return (nre, nim)


def _fft(xs, sign, first_half_only=False):
    n = len(xs)
    if n == 1:
        return list(xs)
    ev = _fft(xs[0::2], sign)
    od = _fft(xs[1::2], sign)
    out = [None] * n
    for k in range(n // 2):
        t = _twiddle(od[k], cmath.exp(sign * 2j * math.pi * k / n))
        out[k] = (_add(ev[k][0], t[0]), _add(ev[k][1], t[1]))
        if not first_half_only:
            out[k + n // 2] = (_sub(ev[k][0], t[0]), _sub(ev[k][1], t[1]))
    return out[: n // 2] if first_half_only else out


def _store_col_tiles(o_ref, rows, acc):
    for c in range(acc.shape[1] // CONV_TC):
        o_ref[0, c, rows, :] = acc[:, c * CONV_TC:(c + 1) * CONV_TC].astype(o_ref.dtype)


def _load_col_tiles(ref, rows):
    return jnp.concatenate([ref[0, c, rows, :] for c in range(ref.shape[1])], axis=1)


def _tile_major_shape(B, L, n):
    return (B, n // CONV_TC, L, CONV_TC)


def _proj_kernel(x_ref, w_ref, b_ref, o_ref):
    x = x_ref[0].astype(BF16)
    acc = jnp.dot(x, w_ref[...], preferred_element_type=F32) + b_ref[...]
    _store_col_tiles(o_ref, slice(None), acc)


def _hyena_proj(x, w, b):
    B, L, _ = x.shape
    n_out = w.shape[1]
    tm = min(L, HYENA_TM)
    return pl.pallas_call(
        _proj_kernel,
        grid=(B, L // tm),
        in_specs=[
            pl.BlockSpec((1, tm, D_MODEL), lambda b, t: (b, t, 0)),
            _const_spec((D_MODEL, n_out), lambda b, t: (0, 0)),
            _const_spec((1, n_out), lambda b, t: (0, 0)),
        ],
        out_specs=pl.BlockSpec((1, n_out // CONV_TC, tm, CONV_TC), lambda b, t: (b, 0, t, 0)),
        out_shape=jax.ShapeDtypeStruct(_tile_major_shape(B, L, n_out), BF16),
        compiler_params=_cparams(2, 48),
        name="hyena_in_proj",
    )(x, w, b)


def _attn_in_epilogue(acc, q_gain, k_gain, cos2, sin2, store):
    def head(col, gain, scale):
        xh = acc[:, col:col + HEAD_DIM]
        ms = jnp.mean(xh * xh, axis=-1, keepdims=True)
        xn = xh * lax.rsqrt(ms + RMS_EPS) * gain
        rot = xn * cos2 + pltpu.roll(xn, AXIS_DIM, axis=1) * sin2
        store(col, rot * scale if scale != 1.0 else rot)

    for h in range(N_Q_HEADS):
        head(h * HEAD_DIM, q_gain, HEAD_DIM ** -0.5 * math.log2(math.e))
    store(ATTN_WIDTH, acc[:, ATTN_WIDTH:2 * ATTN_WIDTH])
    for h in range(N_KV_HEADS):
        head(2 * ATTN_WIDTH + h * HEAD_DIM, k_gain, 1.0)
    v0 = 2 * ATTN_WIDTH + KV_WIDTH
    store(v0, acc[:, v0:v0 + KV_WIDTH])


def _gated_out_ln(y, g, x, w_ref, b_ref, lg_ref, lb_ref):
    y = y.astype(F32)
    g = g.astype(F32)
    h = 0.5 * g
    yg = (y * (h + h * jnp.tanh(h))).astype(BF16)
    s = jnp.dot(yg, w_ref[...], preferred_element_type=F32) + b_ref[...]
    v = DEEPNORM_ALPHA * x + s
    mu = jnp.mean(v, axis=-1, keepdims=True)
    vc = v - mu
    var = jnp.mean(vc * vc, axis=-1, keepdims=True)
    return vc * lax.rsqrt(var + LN_EPS) * lg_ref[...] + lb_ref[...]


def _boundary_kernel(*refs, attn_next, sub_rows):
    y_ref, g_ref, x_ref, wo_ref, bo_ref, lg_ref, lb_ref, wi_ref, bi_ref = refs[:9]
    if attn_next:
        qn_ref, kn_ref, cos_ref, sin_ref, xo_ref, u_ref = refs[9:]
    else:
        xo_ref, u_ref = refs[9:]
    sub_tiles = [slice(r0, r0 + sub_rows) for r0 in range(0, x_ref.shape[1], sub_rows)]
    normed = []
    for rows in sub_tiles:
        if attn_next:
            y, g = _load_col_tiles(y_ref, rows), _load_col_tiles(g_ref, rows)
        else:
            y, g = y_ref[0, rows, :], g_ref[0, rows, :]
        xn = _gated_out_ln(y, g, x_ref[0, rows, :], wo_ref, bo_ref, lg_ref, lb_ref)
        xo_ref[0, rows, :] = xn
        normed.append(xn.astype(BF16))
    for rows, xb in zip(sub_tiles, normed):
        acc = jnp.dot(xb, wi_ref[...], preferred_element_type=F32) + bi_ref[...]
        if attn_next:
            def store(col, val, rows=rows):
                u_ref[0, rows, col:col + val.shape[1]] = val.astype(u_ref.dtype)

            _attn_in_epilogue(acc, qn_ref[...], kn_ref[...], cos_ref[rows, :], sin_ref[rows, :], store)
        else:
            _store_col_tiles(u_ref, rows, acc)


def _layer_boundary(y, u, gate_block, x, out_params, in_params, rope, name):
    B, L, _ = x.shape
    attn_next = len(in_params) == 4
    n_next = in_params[0].shape[1]
    tm = min(L, 2 * OUT_TM if attn_next else OUT_TM)
    x_spec = pl.BlockSpec((1, tm, D_MODEL), lambda b_, t: (b_, t, 0))
    zero = lambda b_, t: (0, 0)
    if attn_next:
        tiles = D_MODEL // CONV_TC
        y_spec = pl.BlockSpec((1, tiles, tm, CONV_TC), lambda b_, t: (b_, 0, t, 0))
        g_spec = pl.BlockSpec((1, tiles, tm, CONV_TC), lambda b_, t: (b_, gate_block, t, 0))
        u_spec = pl.BlockSpec((1, tm, n_next), lambda b_, t: (b_, t, 0))
        u_shape = (B, L, n_next)
    else:
        y_spec = x_spec
        g_spec = pl.BlockSpec((1, tm, D_MODEL), lambda b_, t: (b_, t, gate_block))
        u_spec = pl.BlockSpec((1, n_next // CONV_TC, tm, CONV_TC), lambda b_, t: (b_, 0, t, 0))
        u_shape = _tile_major_shape(B, L, n_next)
    in_specs = [
        y_spec,
        g_spec,
        x_spec,
        _const_spec((D_MODEL, D_MODEL), zero),
        _const_spec((1, D_MODEL), zero),
        _const_spec((1, D_MODEL), zero),
        _const_spec((1, D_MODEL), zero),
        _const_spec((D_MODEL, n_next), zero),
        _const_spec((1, n_next), zero),
    ]
    args = [y, u, x, *out_params, *in_params]
    if attn_next:
        in_specs += [
            _const_spec((1, HEAD_DIM), zero),
            _const_spec((1, HEAD_DIM), zero),
            pl.BlockSpec((tm, HEAD_DIM), lambda b_, t: (t, 0)),
            pl.BlockSpec((tm, HEAD_DIM), lambda b_, t: (t, 0)),
        ]
        args += list(rope)
    return pl.pallas_call(
        functools.partial(_boundary_kernel, attn_next=attn_next, sub_rows=BOUNDARY_SUB_ROWS),
        grid=(B, L // tm),
        in_specs=in_specs,
        out_specs=[x_spec, u_spec],
        out_shape=[jax.ShapeDtypeStruct(x.shape, F32), jax.ShapeDtypeStruct(u_shape, BF16)],
        compiler_params=_cparams(2, 56),
        name=name,
    )(*args)


def _final_out_kernel(y_ref, g_ref, x_ref, w_ref, b_ref, lg_ref, lb_ref, o_ref, *, r):
    mt = y_ref.shape[2]
    rows = lambda ref: jnp.concatenate([ref[0, q] for q in range(r)], axis=0)
    res = _gated_out_ln(rows(y_ref), rows(g_ref), rows(x_ref), w_ref, b_ref, lg_ref, lb_ref)
    for q in range(r):
        o_ref[0, :, q, :] = res[q * mt:(q + 1) * mt]


def _final_out_proj_ln(y, u, gate_block, x, w, b, lg, lb, r, name):
    B, L, _ = x.shape
    F = L // r
    mt = min(F, 2 * OUT_TM // r)
    cm_spec = lambda blk: pl.BlockSpec((1, r, mt, D_MODEL), lambda b_, t: (b_, 0, t, blk))
    zero = lambda b_, t: (0, 0)
    out = pl.pallas_call(
        functools.partial(_final_out_kernel, r=r),
        grid=(B, F // mt),
        in_specs=[
            cm_spec(0),
            cm_spec(gate_block),
            cm_spec(0),
            _const_spec((D_MODEL, D_MODEL), zero),
            _const_spec((1, D_MODEL), zero),
            _const_spec((1, D_MODEL), zero),
            _const_spec((1, D_MODEL), zero),
        ],
        out_specs=pl.BlockSpec((1, mt, r, D_MODEL), lambda b_, t: (b_, t, 0, 0)),
        out_shape=jax.ShapeDtypeStruct((B, F, r, D_MODEL), F32),
        compiler_params=_cparams(2, 40),
        name=name,
    )(y.reshape(B, r, F, D_MODEL), u.reshape(B, r, F, u.shape[-1]), x.reshape(B, r, F, D_MODEL),
      w, b, lg, lb)
    return out.reshape(B, L, D_MODEL)


def _class_slices(ref, q, F, rows, lanes):
    return (ref[q, rows, lanes], ref[q, pl.ds(rows.start + F, rows.size), lanes])


def _filter_kernel(feat_ref, w1_ref, b1_ref, fr_ref, w2_ref, b2_ref, w3f_ref, w3b_ref,
                   dcf_ref, dcb_ref, tf_ref, kf_ref, h_s, pa_s, pb_s, *, r, F, tc, inv_len):
    hi = lax.Precision.HIGHEST

    @pl.when((pl.program_id(0) == 0) & (pl.program_id(1) == 0))
    def _():
        fr = fr_ref[...]
        h = jnp.sin(fr * (jnp.dot(feat_ref[...], w1_ref[...], precision=hi, preferred_element_type=F32)
                          + b1_ref[...]))
        h = jnp.sin(fr * (jnp.dot(h, w2_ref[...], precision=hi, preferred_element_type=F32)
                          + b2_ref[...]))
        h_hi = h.astype(BF16)
        h_lo = (h - h_hi.astype(F32)).astype(BF16)
        w = FILTER_WIDTH
        h_s[:, 0:w] = h_hi
        h_s[:, w:2 * w] = h_hi
        h_s[:, 2 * w:3 * w] = h_lo
        h_s[:, 3 * w:4 * w] = jnp.zeros_like(h_lo)

    def last_layer(w_ref):
        w = w_ref[...]
        w_hi = w.astype(BF16)
        w_lo = (w - w_hi.astype(F32)).astype(BF16)
        w_cat = jnp.concatenate([w_hi, w_lo, w_hi, jnp.zeros_like(w_hi)], axis=0)
        return jnp.dot(h_s[...], w_cat, preferred_element_type=F32)

    t = feat_ref[:, 0:1]
    hf = last_layer(w3f_ref) * jnp.exp(-t * jnp.abs(dcf_ref[...]))
    hb = last_layer(w3b_ref) * jnp.exp(-t * jnp.abs(dcb_ref[...]))
    ss = jnp.sum(hf * hf + hb * hb, axis=0, keepdims=True)
    sc = lax.rsqrt(ss + FILTER_EPS) * inv_len
    row = lax.broadcasted_iota(jnp.int32, hb.shape, 0)
    taps = (hf * sc, jnp.where(row == 0, 0.0, hb * sc))

    for h, dst in zip(taps, (pa_s, pb_s)):
        for q in range(r):
            dst[q] = jnp.dot(tf_ref[q], h[q * F:(q + 1) * F, :].astype(BF16), preferred_element_type=F32)

    def body(i, carry):
        rows = pl.ds(pl.multiple_of(i * SPEC_ROWS, SPEC_ROWS), SPEC_ROWS)
        rows_im = pl.ds(rows.start + F, rows.size)
        for lc in range(tc // V7X_LANES):
            lanes = slice(lc * V7X_LANES, (lc + 1) * V7X_LANES)
            e = []
            for q in range(r):
                are, aim = _class_slices(pa_s, q, F, rows, lanes)
                bre, bim = _class_slices(pb_s, (r - q) % r, F, rows, lanes)
                e.append((are + bre, aim - bim))
            x = _fft(e, -1.0)
            for j in range(r):
                kf_ref[0, j, rows, lanes] = x[j][0].astype(kf_ref.dtype)
                kf_ref[0, j, rows_im, lanes] = x[j][1].astype(kf_ref.dtype)
        return carry

    lax.fori_loop(0, F // SPEC_ROWS, body, 0)


def _hyena_filter_spectrum(feats, w1p, b1, fr, w2, b2, w3, decay, tf, r):
    L = feats.shape[0]
    F = L // r
    E = HY_WIDTH
    tc = CONV_TC
    nct = E // tc
    zero = lambda o, c: (0, 0)
    col = lambda d: (lambda o, c: (0, (2 * o + d) * nct + c))
    kern = functools.partial(_filter_kernel, r=r, F=F, tc=tc, inv_len=1.0 / L)
    return pl.pallas_call(
        kern,
        grid=(HY_ORDER, nct),
        in_specs=[
            _const_spec((L, FILT_PAD_K), zero),
            _const_spec((FILT_PAD_K, FILTER_WIDTH), zero),
            _const_spec((1, FILTER_WIDTH), zero),
            _const_spec((1, FILTER_WIDTH), zero),
            _const_spec((FILTER_WIDTH, FILTER_WIDTH), zero),
            _const_spec((1, FILTER_WIDTH), zero),
            pl.BlockSpec((FILTER_WIDTH, tc), col(0)),
            pl.BlockSpec((FILTER_WIDTH, tc), col(1)),
            pl.BlockSpec((1, tc), col(0)),
            pl.BlockSpec((1, tc), col(1)),
            _const_spec((r, 2 * F, F), lambda o, c: (0, 0, 0)),
        ],
        out_specs=pl.BlockSpec((1, r, 2 * F, tc), lambda o, c: (o, 0, 0, c)),
        out_shape=jax.ShapeDtypeStruct((HY_ORDER, r, 2 * F, E), BF16),
        scratch_shapes=[pltpu.VMEM((L, 4 * FILTER_WIDTH), BF16),
                        pltpu.VMEM((r, 2 * F, tc), F32), pltpu.VMEM((r, 2 * F, tc), F32)],
        compiler_params=_cparams(2, 56),
        name="hyena_filter_spectrum",
    )(feats, w1p, b1, fr, w2, b2, w3, w3, decay, decay, tf)


def _short_conv_classes(u_ref, w, b, r, F):
    def cls(k):
        return u_ref[k * F:(k + 1) * F, :].astype(F32)

    first = cls(0)
    row = lax.broadcasted_iota(jnp.int32, first.shape, 0)
    prev = jnp.where(row == 0, 0.0, pltpu.roll(cls(r - 1), 1, axis=0))
    cur = first
    for q in range(r):
        if q < r - 1:
            nxt = cls(q + 1)
        else:
            nxt = jnp.where(row == F - 1, 0.0, pltpu.roll(first, F - 1, axis=0))
        yield q, prev * w[0:1] + cur * w[1:2] + nxt * w[2:3] + b
        prev, cur = cur, nxt


def _conv_kernel(uv_ref, ug0_ref, ug1_ref, kf_ref, tf_ref, ti_ref, cw_ref, cb_ref, fb_ref,
                 out_ref, z_s, p_s, a_s, *, r, F, tc):
    for q, v in _short_conv_classes(uv_ref, cw_ref[0], cb_ref[0:1, :], r, F):
        z_s[q] = v

    gate_refs = (ug0_ref, ug1_ref)
    for o in range(HY_ORDER):
        def spectral(rows, rows_im, o=o):
            for lc in range(tc // V7X_LANES):
                lanes = slice(lc * V7X_LANES, (lc + 1) * V7X_LANES)
                x = _fft([(p_s[q, rows, lanes], p_s[q, rows_im, lanes]) for q in range(r)], -1.0)
                y = []
                for j in range(r):
                    kre = kf_ref[o, j, rows, lanes]
                    kim = kf_ref[o, j, rows_im, lanes]
                    xre, xim = x[j]
                    y.append((xre * kre - xim * kim, xre * kim + xim * kre))
                a = _fft(y, 1.0)
                for q in range(r):
                    a_s[q, rows, lanes] = a[q][0]
                    a_s[q, rows_im, lanes] = a[q][1]

        for q in range(r):
            p_s[q] = jnp.dot(tf_ref[q], z_s[q].astype(BF16), preferred_element_type=F32).astype(p_s.dtype)

        def body(i, carry):
            rows = pl.ds(pl.multiple_of(i * SPEC_ROWS, SPEC_ROWS), SPEC_ROWS)
            spectral(rows, pl.ds(rows.start + F, rows.size))
            return carry

        lax.fori_loop(0, F // SPEC_ROWS, body, 0)

        gates = _short_conv_classes(gate_refs[o], cw_ref[1 + o], cb_ref[1 + o:2 + o, :], r, F)
        for q, g in gates:
            y = jnp.dot(ti_ref[q], a_s[q], preferred_element_type=F32)
            z_new = g * (y + z_s[q] * fb_ref[o:o + 1, :])
            if o == HY_ORDER - 1:
                out_ref[q * F:(q + 1) * F, :] = z_new.astype(out_ref.dtype)
            else:
                z_s[q] = z_new


def _hyena_conv(u, kf, tf, ti, cw, cb, fb, r):
    B, _, L, _ = u.shape
    F = L // r
    E = HY_WIDTH
    tc = CONV_TC
    nct = E // tc
    kern = functools.partial(_conv_kernel, r=r, F=F, tc=tc)
    sec = lambda s: pl.BlockSpec((None, None, L, tc), lambda c, b: (b, s * nct + c, 0, 0))
    return pl.pallas_call(
        kern,
        grid=(nct, B),
        in_specs=[
            sec(0), sec(1), sec(2),
            _const_spec((HY_ORDER, r, 2 * F, tc), lambda c, b: (0, 0, 0, c)),
            _const_spec((r, 2 * F, F), lambda c, b: (0, 0, 0)),
            _const_spec((r, F, 2 * F), lambda c, b: (0, 0, 0)),
            _const_spec((3, 3, tc), lambda c, b: (0, 0, c)),
            _const_spec((3, tc), lambda c, b: (0, c)),
            _const_spec((HY_ORDER, tc), lambda c, b: (0, c)),
        ],
        out_specs=pl.BlockSpec((None, None, L, tc), lambda c, b: (b, c, 0, 0)),
        out_shape=jax.ShapeDtypeStruct(_tile_major_shape(B, L, E), BF16),
        scratch_shapes=[pltpu.VMEM((r, F, tc), F32), pltpu.VMEM((r, 2 * F, tc), BF16),
                        pltpu.VMEM((r, 2 * F, tc), BF16)],
        compiler_params=_cparams(2, 56),
        name="hyena_long_conv",
    )(u, u, u, kf, tf, ti, cw, cb, fb)


def _attn_kernel(bound_ref, q_ref, k_ref, v_ref, o_ref, vt_s, p_s):
    @pl.when(pl.program_id(2) == 0)
    def _():
        vt_s[...] = v_ref[0].T

    k = k_ref[0]
    bound = bound_ref[...]
    dims_nt = (((1,), (1,)), ((), ()))

    def q_head(g):
        return q_ref[0, :, g * HEAD_DIM:(g + 1) * HEAD_DIM]

    def finish(g, p_bf16, denom):
        ot = jnp.dot(vt_s[...], p_bf16, preferred_element_type=F32) * (1.0 / denom)
        o_ref[0, :, g * HEAD_DIM:(g + 1) * HEAD_DIM] = ot.T.astype(o_ref.dtype)

    denoms = []
    for g in range(GROUP):
        st = lax.dot_general(k, q_head(g), dims_nt, preferred_element_type=F32)
        p = jnp.exp2(st - bound)
        denoms.append(jnp.sum(p, axis=0, keepdims=True))
        p_s[g] = p.astype(BF16)
    for g in range(GROUP):
        finish(g, p_s[g], denoms[g])

    smallest = jnp.min(jnp.minimum(jnp.minimum(denoms[0], denoms[1]), jnp.minimum(denoms[2], denoms[3])))

    @pl.when(jnp.logical_not(smallest >= MIN_SAFE_DENOM))
    def _():
        for g in range(GROUP):
            st = lax.dot_general(k, q_head(g), dims_nt, preferred_element_type=F32)
            p = jnp.exp2(st - jnp.max(st, axis=0, keepdims=True))
            finish(g, p.astype(BF16), jnp.sum(p, axis=0, keepdims=True))


def _score_bound(q_gain, k_gain):
    unit = BOUND_SLACK * math.log2(math.e) * HEAD_DIM ** 0.5
    return unit * jnp.max(jnp.abs(q_gain)) * jnp.max(jnp.abs(k_gain))


def _attention(u, q_gain, k_gain):
    B, L, _ = u.shape
    tq = min(L, ATTN_TILE_ELEMS // L)
    bound = jnp.full((1, tq), _score_bound(q_gain, k_gain), F32)
    gw = GROUP * HEAD_DIM
    k_block = 2 * ATTN_WIDTH // HEAD_DIM
    v_block = (2 * ATTN_WIDTH + KV_WIDTH) // HEAD_DIM
    return pl.pallas_call(
        _attn_kernel,
        grid=(B, N_KV_HEADS, L // tq),
        in_specs=[
            _const_spec((1, tq), lambda b, h, t: (0, 0)),
            pl.BlockSpec((1, tq, gw), lambda b, h, t: (b, t, h)),
            pl.BlockSpec((1, L, HEAD_DIM), lambda b, h, t: (b, 0, k_block + h)),
            pl.BlockSpec((1, L, HEAD_DIM), lambda b, h, t: (b, 0, v_block + h)),
        ],
        out_specs=pl.BlockSpec((1, tq, gw), lambda b, h, t: (b, t, h)),
        out_shape=jax.ShapeDtypeStruct((B, L, ATTN_WIDTH), BF16),
        scratch_shapes=[pltpu.VMEM((HEAD_DIM, L), BF16),
                        pltpu.VMEM((GROUP, L, tq), BF16)],
        compiler_params=_cparams(3, 56),
        name="gqa_attention",
    )(bound, u, u, u)


def _class_dft_tables(L, r):
    F = L // r
    f = jnp.arange(F, dtype=jnp.int32)
    m = jnp.arange(F, dtype=jnp.int32)
    q = jnp.arange(r, dtype=jnp.int32)
    n = ((2 * f + 1)[None, :, None] * (r * m[None, None, :] + q[:, None, None])) % (4 * L)
    theta = n.astype(F32) * (math.pi / (2 * L))
    c = jnp.cos(theta)
    s = jnp.sin(theta)
    tf = jnp.concatenate([c, -s], axis=1).astype(BF16)
    return tf, tf.transpose(0, 2, 1)


def _filter_features(L):
    t = jnp.linspace(0.0, 1.0, L, dtype=F32)[:, None]
    w = 2.0 * math.pi * jnp.arange(L, dtype=F32)[:, None] / L
    f = jnp.linspace(1e-4, POS_BANDS - 1, POS_BANDS, dtype=F32)[None]
    feats = jnp.concatenate([t, jnp.cos(f * w), -jnp.sin(f * w)], axis=-1)
    return jnp.pad(feats, ((0, 0), (0, FILT_PAD_K - POS_EMB_DIM)))


def _rope_tables(L):
    rows = L // GRID_W
    row = jnp.repeat(jnp.arange(rows, dtype=F32), GRID_W)
    col = jnp.tile(jnp.arange(GRID_W, dtype=F32), rows)
    inv = ROPE_THETA ** (-jnp.arange(0, AXIS_DIM, 2, dtype=F32) / AXIS_DIM)
    ang = jnp.concatenate([row[:, None] * inv, col[:, None] * inv], axis=-1)
    c, s = jnp.cos(ang), jnp.sin(ang)
    return jnp.concatenate([c, c], axis=-1), jnp.concatenate([-s, s], axis=-1)


def _to_class_major(a, r):
    lead, L, C = a.shape[:-2], a.shape[-2], a.shape[-1]
    return a.reshape(lead + (L // r, r, C)).swapaxes(-3, -2).reshape(a.shape)


def _attn_in_params(p, j):
    w = p["at_w_in"][j]
    bi = p["at_b_in"][j]
    kv0 = ATTN_WIDTH
    g0 = ATTN_WIDTH + 2 * KV_WIDTH
    w = jnp.concatenate([w[:, :kv0], w[:, g0:], w[:, kv0:g0]], axis=1).astype(BF16)
    bi = jnp.concatenate([bi[:kv0], bi[g0:], bi[kv0:g0]])[None]
    return (w, bi, p["at_q_norm"][j][None], p["at_k_norm"][j][None])


def _trunk(x, p):
    B, L, _ = x.shape
    r = L // CLASS_ROWS
    tf, ti = _class_dft_tables(L, r)
    feats = _to_class_major(_filter_features(L), r)
    rope = tuple(_to_class_major(t, r) for t in _rope_tables(L))
    x = _to_class_major(x, r)

    def in_params(i):
        j = i // 2
        if i % 2 == 0:
            return (p["hy_w_in"][j].astype(BF16), p["hy_b_in"][j][None])
        return _attn_in_params(p, j)

    def out_params(i):
        j = i // 2
        w, b = (p["hy_w_out"], p["hy_b_out"]) if i % 2 == 0 else (p["at_w_out"], p["at_b_out"])
        return (w[j].astype(BF16), b[j][None], p["ln_g"][i][None], p["ln_b"][i][None])

    u = _hyena_proj(x, *in_params(0))
    for i in range(DEPTH):
        j = i // 2
        if i % 2 == 0:
            w1p = jnp.pad(p["hy_f_w1"][j], ((0, FILT_PAD_K - POS_EMB_DIM), (0, 0)))
            kf = _hyena_filter_spectrum(feats, w1p, p["hy_f_b1"][j][None], p["hy_f_freq"][j][None],
                                        p["hy_f_w2"][j], p["hy_f_b2"][j][None], p["hy_f_w3"][j],
                                        p["hy_decay"][j][None], tf, r)
            cw = p["hy_conv_w"][j].reshape(3, 3, HY_WIDTH).transpose(1, 0, 2)
            cb = p["hy_conv_b"][j].reshape(3, HY_WIDTH)
            y = _hyena_conv(u, kf, tf, ti, cw, cb, p["hy_filt_bias"][j], r)
            gate_block = 3 * HY_WIDTH // D_MODEL
            name = "hyena_out"
        else:
            y = _attention(u, p["at_q_norm"][j], p["at_k_norm"][j])
            gate_block = 1
            name = "attn_out"
        if i + 1 < DEPTH:
            x, u = _layer_boundary(y, u, gate_block, x, out_params(i), in_params(i + 1), rope,
                                   name + "_next_in")
        else:
            x = _final_out_proj_ln(y, u, gate_block, x, *out_params(i), r, name + "_ln")
    return x


def kernel(x_prompt, x_sample, hy_w_in, hy_b_in, hy_conv_w, hy_conv_b, hy_f_w1, hy_f_b1, hy_f_freq, hy_f_w2,
           hy_f_b2, hy_f_w3, hy_decay, hy_filt_bias, hy_w_out, hy_b_out, at_w_in, at_b_in, at_q_norm, at_k_norm,
           at_w_out, at_b_out, ln_g, ln_b):
    p = dict(hy_w_in=hy_w_in, hy_b_in=hy_b_in, hy_conv_w=hy_conv_w, hy_conv_b=hy_conv_b, hy_f_w1=hy_f_w1,
             hy_f_b1=hy_f_b1, hy_f_freq=hy_f_freq, hy_f_w2=hy_f_w2, hy_f_b2=hy_f_b2, hy_f_w3=hy_f_w3,
             hy_decay=hy_decay, hy_filt_bias=hy_filt_bias, hy_w_out=hy_w_out, hy_b_out=hy_b_out,
             at_w_in=at_w_in, at_b_in=at_b_in, at_q_norm=at_q_norm, at_k_norm=at_k_norm, at_w_out=at_w_out,
             at_b_out=at_b_out, ln_g=ln_g, ln_b=ln_b)
    return (_trunk(x_prompt, p), _trunk(x_sample, p))
```

```python
import cmath
import functools
import math

import jax
import jax.numpy as jnp
from jax import lax
from jax.experimental import pallas as pl
from jax.experimental.pallas import tpu as pltpu

F32 = jnp.float32
BF16 = jnp.bfloat16

D_MODEL = 1024
DEPTH = 4
GRID_W = 64
HY_WIDTH = D_MODEL
HY_ORDER = 2
POS_EMB_DIM = 33
POS_BANDS = (POS_EMB_DIM - 1) // 2
FILTER_WIDTH = 64
FILTER_EPS = 1e-6
HEAD_DIM = 128
N_Q_HEADS = D_MODEL // HEAD_DIM
N_KV_HEADS = 2
GROUP = N_Q_HEADS // N_KV_HEADS
ATTN_WIDTH = N_Q_HEADS * HEAD_DIM
KV_WIDTH = N_KV_HEADS * HEAD_DIM
AXIS_DIM = HEAD_DIM // 2
ROPE_THETA = 10000.0
RMS_EPS = 1e-6
LN_EPS = 1e-5
DEEPNORM_ALPHA = (2.0 * DEPTH) ** 0.25

V7X_LANES = 128
V7X_SUBLANES = 8
V7X_MXU_DIM = 256

CLASS_ROWS = V7X_MXU_DIM
CONV_TC = V7X_MXU_DIM
SPEC_ROWS = 2 * V7X_SUBLANES
ATTN_TILE_ELEMS = 512 * 4096
OUT_TM = 512
BOUNDARY_SUB_ROWS = 256
HYENA_TM = 1024
FILT_PAD_K = 64
BOUND_SLACK = 1.02
MIN_SAFE_DENOM = 2.0 ** -90


def _cparams(n_axes, vmem_mb):
    return pltpu.CompilerParams(
        dimension_semantics=("arbitrary",) * n_axes,
        vmem_limit_bytes=vmem_mb * 1024 * 1024,
    )


def _const_spec(block, index_map):
    return pl.BlockSpec(block, index_map, pipeline_mode=pl.Buffered(1))


def _add(a, b):
    if a is None:
        return b
    if b is None:
        return a
    return a + b


def _sub(a, b):
    if b is None:
        return a
    if a is None:
        return -b
    return a - b


def _scale(a, c):
    return None if a is None else a * c


def _twiddle(x, w):
    re, im = x
    c, s = w.real, w.imag
    if abs(s) < 1e-12:
        return (re, im) if c > 0 else (_sub(None, re), _sub(None, im))
    if abs(c) < 1e-12:
        return (_sub(None, im), re) if s > 0 else (im, _sub(None, re))
    if abs(abs(c) - abs(s)) < 1e-12:
        a = abs(c)
        total = _add(re, im)
        if c > 0 and s > 0:
            return (_scale(_sub(re, im), a), _scale(total, a))
        if c > 0:
            return (_scale(total, a), _scale(_sub(im, re), a))
        if s > 0:
            return (_scale(total, -a), _scale(_sub(re, im), a))
        return (_scale(_sub(im, re), a), _scale(total, -a))
    nre = _sub(_scale(re, c), _scale(im, s))
    nim = _add(_scale(re, s), _scale(im, c))
    return (nre, nim)


def _fft(xs, sign):
    n = len(xs)
    if n == 1:
        return list(xs)
    ev = _fft(xs[0::2], sign)
    od = _fft(xs[1::2], sign)
    out = [None] * n
    for k in range(n // 2):
        t = _twiddle(od[k], cmath.exp(sign * 2j * math.pi * k / n))
        out[k] = (_add(ev[k][0], t[0]), _add(ev[k][1], t[1]))
        out[k + n // 2] = (_sub(ev[k][0], t[0]), _sub(ev[k][1], t[1]))
    return out


def _store_col_tiles(o_ref, rows, acc):
    for c in range(acc.shape[1] // CONV_TC):
        o_ref[0, c, rows, :] = acc[:, c * CONV_TC:(c + 1) * CONV_TC].astype(o_ref.dtype)


def _load_col_tiles(ref, rows):
    return jnp.concatenate([ref[0, c, rows, :] for c in range(ref.shape[1])], axis=1)


def _tile_major_shape(B, L, n):
    return (B, n // CONV_TC, L, CONV_TC)


def _proj_kernel(x_ref, w_ref, b_ref, o_ref):
    x = x_ref[0].astype(BF16)
    acc = jnp.dot(x, w_ref[...], preferred_element_type=F32) + b_ref[...]
    _store_col_tiles(o_ref, slice(None), acc)


def _hyena_proj(x, w, b):
    B, L, _ = x.shape
    n_out = w.shape[1]
    tm = min(L, HYENA_TM)
    return pl.pallas_call(
        _proj_kernel,
        grid=(B, L // tm),
        in_specs=[
            pl.BlockSpec((1, tm, D_MODEL), lambda b, t: (b, t, 0)),
            _const_spec((D_MODEL, n_out), lambda b, t: (0, 0)),
            _const_spec((1, n_out), lambda b, t: (0, 0)),
        ],
        out_specs=pl.BlockSpec((1, n_out // CONV_TC, tm, CONV_TC), lambda b, t: (b, 0, t, 0)),
        out_shape=jax.ShapeDtypeStruct(_tile_major_shape(B, L, n_out), BF16),
        compiler_params=_cparams(2, 48),
        name="hyena_in_proj",
    )(x, w, b)


def _attn_in_epilogue(acc, q_gain, k_gain, cos2, sin2, store):
    def head(col, gain, scale):
        xh = acc[:, col:col + HEAD_DIM]
        ms = jnp.mean(xh * xh, axis=-1, keepdims=True)
        xn = xh * lax.rsqrt(ms + RMS_EPS) * gain
        rot = xn * cos2 + pltpu.roll(xn, AXIS_DIM, axis=1) * sin2
        store(col, rot * scale if scale != 1.0 else rot)

    for h in range(N_Q_HEADS):
        head(h * HEAD_DIM, q_gain, HEAD_DIM ** -0.5 * math.log2(math.e))
    store(ATTN_WIDTH, acc[:, ATTN_WIDTH:2 * ATTN_WIDTH])
    for h in range(N_KV_HEADS):
        head(2 * ATTN_WIDTH + h * HEAD_DIM, k_gain, 1.0)
    v0 = 2 * ATTN_WIDTH + KV_WIDTH
    store(v0, acc[:, v0:v0 + KV_WIDTH])


def _gated_out_ln(y, g, x, w_ref, b_ref, lg_ref, lb_ref):
    y = y.astype(F32)
    g = g.astype(F32)
    h = 0.5 * g
    yg = (y * (h + h * jnp.tanh(h))).astype(BF16)
    s = jnp.dot(yg, w_ref[...], preferred_element_type=F32) + b_ref[...]
    v = DEEPNORM_ALPHA * x + s
    mu = jnp.mean(v, axis=-1, keepdims=True)
    vc = v - mu
    var = jnp.mean(vc * vc, axis=-1, keepdims=True)
    return vc * lax.rsqrt(var + LN_EPS) * lg_ref[...] + lb_ref[...]


def _boundary_kernel(*refs, attn_next, sub_rows):
    y_ref, g_ref, x_ref, wo_ref, bo_ref, lg_ref, lb_ref, wi_ref, bi_ref = refs[:9]
    if attn_next:
        qn_ref, kn_ref, cos_ref, sin_ref, xo_ref, u_ref = refs[9:]
    else:
        xo_ref, u_ref = refs[9:]
    sub_tiles = [slice(r0, r0 + sub_rows) for r0 in range(0, x_ref.shape[1], sub_rows)]
    normed = []
    for rows in sub_tiles:
        if attn_next:
            y, g = _load_col_tiles(y_ref, rows), _load_col_tiles(g_ref, rows)
        else:
            y, g = y_ref[0, rows, :], g_ref[0, rows, :]
        xn = _gated_out_ln(y, g, x_ref[0, rows, :], wo_ref, bo_ref, lg_ref, lb_ref)
        xo_ref[0, rows, :] = xn
        normed.append(xn.astype(BF16))
    for rows, xb in zip(sub_tiles, normed):
        acc = jnp.dot(xb, wi_ref[...], preferred_element_type=F32) + bi_ref[...]
        if attn_next:
            def store(col, val, rows=rows):
                u_ref[0, rows, col:col + val.shape[1]] = val.astype(u_ref.dtype)

            _attn_in_epilogue(acc, qn_ref[...], kn_ref[...], cos_ref[rows, :], sin_ref[rows, :], store)
        else:
            _store_col_tiles(u_ref, rows, acc)


def _layer_boundary(y, u, gate_block, x, out_params, in_params, rope, name):
    B, L, _ = x.shape
    attn_next = len(in_params) == 4
    n_next = in_params[0].shape[1]
    tm = min(L, 2 * OUT_TM if attn_next else OUT_TM)
    x_spec = pl.BlockSpec((1, tm, D_MODEL), lambda b_, t: (b_, t, 0))
    zero = lambda b_, t: (0, 0)
    if attn_next:
        tiles = D_MODEL // CONV_TC
        y_spec = pl.BlockSpec((1, tiles, tm, CONV_TC), lambda b_, t: (b_, 0, t, 0))
        g_spec = pl.BlockSpec((1, tiles, tm, CONV_TC), lambda b_, t: (b_, gate_block, t, 0))
        u_spec = pl.BlockSpec((1, tm, n_next), lambda b_, t: (b_, t, 0))
        u_shape = (B, L, n_next)
    else:
        y_spec = x_spec
        g_spec = pl.BlockSpec((1, tm, D_MODEL), lambda b_, t: (b_, t, gate_block))
        u_spec = pl.BlockSpec((1, n_next // CONV_TC, tm, CONV_TC), lambda b_, t: (b_, 0, t, 0))
        u_shape = _tile_major_shape(B, L, n_next)
    in_specs = [
        y_spec,
        g_spec,
        x_spec,
        _const_spec((D_MODEL, D_MODEL), zero),
        _const_spec((1, D_MODEL), zero),
        _const_spec((1, D_MODEL), zero),
        _const_spec((1, D_MODEL), zero),
        _const_spec((D_MODEL, n_next), zero),
        _const_spec((1, n_next), zero),
    ]
    args = [y, u, x, *out_params, *in_params]
    if attn_next:
        in_specs += [
            _const_spec((1, HEAD_DIM), zero),
            _const_spec((1, HEAD_DIM), zero),
            pl.BlockSpec((tm, HEAD_DIM), lambda b_, t: (t, 0)),
            pl.BlockSpec((tm, HEAD_DIM), lambda b_, t: (t, 0)),
        ]
        args += list(rope)
    return pl.pallas_call(
        functools.partial(_boundary_kernel, attn_next=attn_next, sub_rows=BOUNDARY_SUB_ROWS),
        grid=(B, L // tm),
        in_specs=in_specs,
        out_specs=[x_spec, u_spec],
        out_shape=[jax.ShapeDtypeStruct(x.shape, F32), jax.ShapeDtypeStruct(u_shape, BF16)],
        compiler_params=_cparams(2, 56),
        name=name,
    )(*args)


def _final_out_kernel(y_ref, g_ref, x_ref, w_ref, b_ref, lg_ref, lb_ref, o_ref, *, r):
    mt = y_ref.shape[2]
    rows = lambda ref: jnp.concatenate([ref[0, q] for q in range(r)], axis=0)
    res = _gated_out_ln(rows(y_ref), rows(g_ref), rows(x_ref), w_ref, b_ref, lg_ref, lb_ref)
    for q in range(r):
        o_ref[0, :, q, :] = res[q * mt:(q + 1) * mt]


def _final_out_proj_ln(y, u, gate_block, x, w, b, lg, lb, r, name):
    B, L, _ = x.shape
    F = L // r
    mt = min(F, 2 * OUT_TM // r)
    cm_spec = lambda blk: pl.BlockSpec((1, r, mt, D_MODEL), lambda b_, t: (b_, 0, t, blk))
    zero = lambda b_, t: (0, 0)
    out = pl.pallas_call(
        functools.partial(_final_out_kernel, r=r),
        grid=(B, F // mt),
        in_specs=[
            cm_spec(0),
            cm_spec(gate_block),
            cm_spec(0),
            _const_spec((D_MODEL, D_MODEL), zero),
            _const_spec((1, D_MODEL), zero),
            _const_spec((1, D_MODEL), zero),
            _const_spec((1, D_MODEL), zero),
        ],
        out_specs=pl.BlockSpec((1, mt, r, D_MODEL), lambda b_, t: (b_, t, 0, 0)),
        out_shape=jax.ShapeDtypeStruct((B, F, r, D_MODEL), F32),
        compiler_params=_cparams(2, 40),
        name=name,
    )(y.reshape(B, r, F, D_MODEL), u.reshape(B, r, F, u.shape[-1]), x.reshape(B, r, F, D_MODEL),
      w, b, lg, lb)
    return out.reshape(B, L, D_MODEL)


def _class_slices(ref, q, F, rows, lanes):
    return (ref[q, rows, lanes], ref[q, pl.ds(rows.start + F, rows.size), lanes])


def _filter_kernel(feat_ref, w1_ref, b1_ref, fr_ref, w2_ref, b2_ref, w3f_ref, w3b_ref,
                   dcf_ref, dcb_ref, tf_ref, kf_ref, h_s, pa_s, pb_s, *, r, F, tc, inv_len):
    hi = lax.Precision.HIGHEST

    @pl.when((pl.program_id(0) == 0) & (pl.program_id(1) == 0))
    def _():
        fr = fr_ref[...]
        h = jnp.sin(fr * (jnp.dot(feat_ref[...], w1_ref[...], precision=hi, preferred_element_type=F32)
                          + b1_ref[...]))
        h = jnp.sin(fr * (jnp.dot(h, w2_ref[...], precision=hi, preferred_element_type=F32)
                          + b2_ref[...]))
        h_hi = h.astype(BF16)
        h_lo = (h - h_hi.astype(F32)).astype(BF16)
        w = FILTER_WIDTH
        h_s[:, 0:w] = h_hi
        h_s[:, w:2 * w] = h_hi
        h_s[:, 2 * w:3 * w] = h_lo
        h_s[:, 3 * w:4 * w] = jnp.zeros_like(h_lo)

    def last_layer(w_ref):
        w = w_ref[...]
        w_hi = w.astype(BF16)
        w_lo = (w - w_hi.astype(F32)).astype(BF16)
        w_cat = jnp.concatenate([w_hi, w_lo, w_hi, jnp.zeros_like(w_hi)], axis=0)
        return jnp.dot(h_s[...], w_cat, preferred_element_type=F32)

    t = feat_ref[:, 0:1]
    hf = last_layer(w3f_ref) * jnp.exp(-t * jnp.abs(dcf_ref[...]))
    hb = last_layer(w3b_ref) * jnp.exp(-t * jnp.abs(dcb_ref[...]))
    ss = jnp.sum(hf * hf + hb * hb, axis=0, keepdims=True)
    sc = lax.rsqrt(ss + FILTER_EPS) * inv_len
    row = lax.broadcasted_iota(jnp.int32, hb.shape, 0)
    taps = (hf * sc, jnp.where(row == 0, 0.0, hb * sc))

    for h, dst in zip(taps, (pa_s, pb_s)):
        for q in range(r):
            dst[q] = jnp.dot(tf_ref[q], h[q * F:(q + 1) * F, :].astype(BF16), preferred_element_type=F32)

    def body(i, carry):
        rows = pl.ds(pl.multiple_of(i * SPEC_ROWS, SPEC_ROWS), SPEC_ROWS)
        rows_im = pl.ds(rows.start + F, rows.size)
        for lc in range(tc // V7X_LANES):
            lanes = slice(lc * V7X_LANES, (lc + 1) * V7X_LANES)
            e = []
            for q in range(r):
                are, aim = _class_slices(pa_s, q, F, rows, lanes)
                bre, bim = _class_slices(pb_s, (r - q) % r, F, rows, lanes)
                e.append((are + bre, aim - bim))
            x = _fft(e, -1.0)
            for j in range(r):
                kf_ref[0, j, rows, lanes] = x[j][0].astype(kf_ref.dtype)
                kf_ref[0, j, rows_im, lanes] = x[j][1].astype(kf_ref.dtype)
        return carry

    lax.fori_loop(0, F // SPEC_ROWS, body, 0)


def _hyena_filter_spectrum(feats, w1p, b1, fr, w2, b2, w3, decay, tf, r):
    L = feats.shape[0]
    F = L // r
    E = HY_WIDTH
    tc = CONV_TC
    nct = E // tc
    zero = lambda o, c: (0, 0)
    col = lambda d: (lambda o, c: (0, (2 * o + d) * nct + c))
    kern = functools.partial(_filter_kernel, r=r, F=F, tc=tc, inv_len=1.0 / L)
    return pl.pallas_call(
        kern,
        grid=(HY_ORDER, nct),
        in_specs=[
            _const_spec((L, FILT_PAD_K), zero),
            _const_spec((FILT_PAD_K, FILTER_WIDTH), zero),
            _const_spec((1, FILTER_WIDTH), zero),
            _const_spec((1, FILTER_WIDTH), zero),
            _const_spec((FILTER_WIDTH, FILTER_WIDTH), zero),
            _const_spec((1, FILTER_WIDTH), zero),
            pl.BlockSpec((FILTER_WIDTH, tc), col(0)),
            pl.BlockSpec((FILTER_WIDTH, tc), col(1)),
            pl.BlockSpec((1, tc), col(0)),
            pl.BlockSpec((1, tc), col(1)),
            _const_spec((r, 2 * F, F), lambda o, c: (0, 0, 0)),
        ],
        out_specs=pl.BlockSpec((1, r, 2 * F, tc), lambda o, c: (o, 0, 0, c)),
        out_shape=jax.ShapeDtypeStruct((HY_ORDER, r, 2 * F, E), BF16),
        scratch_shapes=[pltpu.VMEM((L, 4 * FILTER_WIDTH), BF16),
                        pltpu.VMEM((r, 2 * F, tc), F32), pltpu.VMEM((r, 2 * F, tc), F32)],
        compiler_params=_cparams(2, 56),
        name="hyena_filter_spectrum",
    )(feats, w1p, b1, fr, w2, b2, w3, w3, decay, decay, tf)


def _short_conv_classes(u_ref, w, b, r, F):
    def cls(k):
        return u_ref[k * F:(k + 1) * F, :].astype(F32)

    first = cls(0)
    row = lax.broadcasted_iota(jnp.int32, first.shape, 0)
    prev = jnp.where(row == 0, 0.0, pltpu.roll(cls(r - 1), 1, axis=0))
    cur = first
    for q in range(r):
        if q < r - 1:
            nxt = cls(q + 1)
        else:
            nxt = jnp.where(row == F - 1, 0.0, pltpu.roll(first, F - 1, axis=0))
        yield q, prev * w[0:1] + cur * w[1:2] + nxt * w[2:3] + b
        prev, cur = cur, nxt


def _conv_kernel(uv_ref, ug0_ref, ug1_ref, kf_ref, tf_ref, ti_ref, cw_ref, cb_ref, fb_ref,
                 out_ref, z_s, p_s, a_s, *, r, F, tc):
    for q, v in _short_conv_classes(uv_ref, cw_ref[0], cb_ref[0:1, :], r, F):
        z_s[q] = v

    gate_refs = (ug0_ref, ug1_ref)
    for o in range(HY_ORDER):
        def spectral(rows, rows_im, o=o):
            for lc in range(tc // V7X_LANES):
                lanes = slice(lc * V7X_LANES, (lc + 1) * V7X_LANES)
                x = _fft([(p_s[q, rows, lanes], p_s[q, rows_im, lanes]) for q in range(r)], -1.0)
                y = []
                for j in range(r):
                    kre = kf_ref[o, j, rows, lanes]
                    kim = kf_ref[o, j, rows_im, lanes]
                    xre, xim = x[j]
                    y.append((xre * kre - xim * kim, xre * kim + xim * kre))
                a = _fft(y, 1.0)
                for q in range(r):
                    a_s[q, rows, lanes] = a[q][0]
                    a_s[q, rows_im, lanes] = a[q][1]

        for q in range(r):
            p_s[q] = jnp.dot(tf_ref[q], z_s[q].astype(BF16), preferred_element_type=F32).astype(p_s.dtype)

        def body(i, carry):
            rows = pl.ds(pl.multiple_of(i * SPEC_ROWS, SPEC_ROWS), SPEC_ROWS)
            spectral(rows, pl.ds(rows.start + F, rows.size))
            return carry

        lax.fori_loop(0, F // SPEC_ROWS, body, 0)

        gates = _short_conv_classes(gate_refs[o], cw_ref[1 + o], cb_ref[1 + o:2 + o, :], r, F)
        for q, g in gates:
            y = jnp.dot(ti_ref[q], a_s[q], preferred_element_type=F32)
            z_new = g * (y + z_s[q] * fb_ref[o:o + 1, :])
            if o == HY_ORDER - 1:
                out_ref[q * F:(q + 1) * F, :] = z_new.astype(out_ref.dtype)
            else:
                z_s[q] = z_new


def _hyena_conv(u, kf, tf, ti, cw, cb, fb, r):
    B, _, L, _ = u.shape
    F = L // r
    E = HY_WIDTH
    tc = CONV_TC
    nct = E // tc
    kern = functools.partial(_conv_kernel, r=r, F=F, tc=tc)
    sec = lambda s: pl.BlockSpec((None, None, L, tc), lambda c, b: (b, s * nct + c, 0, 0))
    return pl.pallas_call(
        kern,
        grid=(nct, B),
        in_specs=[
            sec(0), sec(1), sec(2),
            _const_spec((HY_ORDER, r, 2 * F, tc), lambda c, b: (0, 0, 0, c)),
            _const_spec((r, 2 * F, F), lambda c, b: (0, 0, 0)),
            _const_spec((r, F, 2 * F), lambda c, b: (0, 0, 0)),
            _const_spec((3, 3, tc), lambda c, b: (0, 0, c)),
            _const_spec((3, tc), lambda c, b: (0, c)),
            _const_spec((HY_ORDER, tc), lambda c, b: (0, c)),
        ],
        out_specs=pl.BlockSpec((None, None, L, tc), lambda c, b: (b, c, 0, 0)),
        out_shape=jax.ShapeDtypeStruct(_tile_major_shape(B, L, E), BF16),
        scratch_shapes=[pltpu.VMEM((r, F, tc), F32), pltpu.VMEM((r, 2 * F, tc), BF16),
                        pltpu.VMEM((r, 2 * F, tc), BF16)],
        compiler_params=_cparams(2, 56),
        name="hyena_long_conv",
    )(u, u, u, kf, tf, ti, cw, cb, fb)


def _attn_kernel(bound_ref, q_ref, k_ref, v_ref, o_ref, vt_s, p_s):
    @pl.when(pl.program_id(2) == 0)
    def _():
        vt_s[...] = v_ref[0].T

    k = k_ref[0]
    bound = bound_ref[...]
    dims_nt = (((1,), (1,)), ((), ()))

    def q_head(g):
        return q_ref[0, :, g * HEAD_DIM:(g + 1) * HEAD_DIM]

    def finish(g, p_bf16, denom):
        ot = jnp.dot(vt_s[...], p_bf16, preferred_element_type=F32) * (1.0 / denom)
        o_ref[0, :, g * HEAD_DIM:(g + 1) * HEAD_DIM] = ot.T.astype(o_ref.dtype)

    denoms = []
    for g in range(GROUP):
        st = lax.dot_general(k, q_head(g), dims_nt, preferred_element_type=F32)
        p = jnp.exp2(st - bound)
        denoms.append(jnp.sum(p, axis=0, keepdims=True))
        p_s[g] = p.astype(BF16)
    for g in range(GROUP):
        finish(g, p_s[g], denoms[g])

    smallest = jnp.min(jnp.minimum(jnp.minimum(denoms[0], denoms[1]), jnp.minimum(denoms[2], denoms[3])))

    @pl.when(jnp.logical_not(smallest >= MIN_SAFE_DENOM))
    def _():
        for g in range(GROUP):
            st = lax.dot_general(k, q_head(g), dims_nt, preferred_element_type=F32)
            p = jnp.exp2(st - jnp.max(st, axis=0, keepdims=True))
            finish(g, p.astype(BF16), jnp.sum(p, axis=0, keepdims=True))


def _score_bound(q_gain, k_gain):
    unit = BOUND_SLACK * math.log2(math.e) * HEAD_DIM ** 0.5
    return unit * jnp.max(jnp.abs(q_gain)) * jnp.max(jnp.abs(k_gain))


def _attention(u, q_gain, k_gain):
    B, L, _ = u.shape
    tq = min(L, ATTN_TILE_ELEMS // L)
    bound = jnp.full((1, tq), _score_bound(q_gain, k_gain), F32)
    gw = GROUP * HEAD_DIM
    k_block = 2 * ATTN_WIDTH // HEAD_DIM
    v_block = (2 * ATTN_WIDTH + KV_WIDTH) // HEAD_DIM
    return pl.pallas_call(
        _attn_kernel,
        grid=(B, N_KV_HEADS, L // tq),
        in_specs=[
            _const_spec((1, tq), lambda b, h, t: (0, 0)),
            pl.BlockSpec((1, tq, gw), lambda b, h, t: (b, t, h)),
            pl.BlockSpec((1, L, HEAD_DIM), lambda b, h, t: (b, 0, k_block + h)),
            pl.BlockSpec((1, L, HEAD_DIM), lambda b, h, t: (b, 0, v_block + h)),
        ],
        out_specs=pl.BlockSpec((1, tq, gw), lambda b, h, t: (b, t, h)),
        out_shape=jax.ShapeDtypeStruct((B, L, ATTN_WIDTH), BF16),
        scratch_shapes=[pltpu.VMEM((HEAD_DIM, L), BF16),
                        pltpu.VMEM((GROUP, L, tq), BF16)],
        compiler_params=_cparams(3, 56),
        name="gqa_attention",
    )(bound, u, u, u)


def _class_dft_tables(L, r):
    F = L // r
    f = jnp.arange(F, dtype=jnp.int32)
    m = jnp.arange(F, dtype=jnp.int32)
    q = jnp.arange(r, dtype=jnp.int32)
    n = ((2 * f + 1)[None, :, None] * (r * m[None, None, :] + q[:, None, None])) % (4 * L)
    theta = n.astype(F32) * (math.pi / (2 * L))
    c = jnp.cos(theta)
    s = jnp.sin(theta)
    tf = jnp.concatenate([c, -s], axis=1).astype(BF16)
    return tf, tf.transpose(0, 2, 1)


def _filter_features(L):
    t = jnp.linspace(0.0, 1.0, L, dtype=F32)[:, None]
    w = 2.0 * math.pi * jnp.arange(L, dtype=F32)[:, None] / L
    f = jnp.linspace(1e-4, POS_BANDS - 1, POS_BANDS, dtype=F32)[None]
    feats = jnp.concatenate([t, jnp.cos(f * w), -jnp.sin(f * w)], axis=-1)
    return jnp.pad(feats, ((0, 0), (0, FILT_PAD_K - POS_EMB_DIM)))


def _rope_tables(L):
    rows = L // GRID_W
    row = jnp.repeat(jnp.arange(rows, dtype=F32), GRID_W)
    col = jnp.tile(jnp.arange(GRID_W, dtype=F32), rows)
    inv = ROPE_THETA ** (-jnp.arange(0, AXIS_DIM, 2, dtype=F32) / AXIS_DIM)
    ang = jnp.concatenate([row[:, None] * inv, col[:, None] * inv], axis=-1)
    c, s = jnp.cos(ang), jnp.sin(ang)
    return jnp.concatenate([c, c], axis=-1), jnp.concatenate([-s, s], axis=-1)


def _to_class_major(a, r):
    lead, L, C = a.shape[:-2], a.shape[-2], a.shape[-1]
    return a.reshape(lead + (L // r, r, C)).swapaxes(-3, -2).reshape(a.shape)


def _attn_in_params(p, j):
    w = p["at_w_in"][j]
    bi = p["at_b_in"][j]
    kv0 = ATTN_WIDTH
    g0 = ATTN_WIDTH + 2 * KV_WIDTH
    w = jnp.concatenate([w[:, :kv0], w[:, g0:], w[:, kv0:g0]], axis=1).astype(BF16)
    bi = jnp.concatenate([bi[:kv0], bi[g0:], bi[kv0:g0]])[None]
    return (w, bi, p["at_q_norm"][j][None], p["at_k_norm"][j][None])


def _trunk(x, p):
    B, L, _ = x.shape
    r = L // CLASS_ROWS
    tf, ti = _class_dft_tables(L, r)
    feats = _to_class_major(_filter_features(L), r)
    rope = tuple(_to_class_major(t, r) for t in _rope_tables(L))
    x = _to_class_major(x, r)

    def in_params(i):
        j = i // 2
        if i % 2 == 0:
            return (p["hy_w_in"][j].astype(BF16), p["hy_b_in"][j][None])
        return _attn_in_params(p, j)

    def out_params(i):
        j = i // 2
        w, b = (p["hy_w_out"], p["hy_b_out"]) if i % 2 == 0 else (p["at_w_out"], p["at_b_out"])
        return (w[j].astype(BF16), b[j][None], p["ln_g"][i][None], p["ln_b"][i][None])

    u = _hyena_proj(x, *in_params(0))
    for i in range(DEPTH):
        j = i // 2
        if i % 2 == 0:
            w1p = jnp.pad(p["hy_f_w1"][j], ((0, FILT_PAD_K - POS_EMB_DIM), (0, 0)))
            kf = _hyena_filter_spectrum(feats, w1p, p["hy_f_b1"][j][None], p["hy_f_freq"][j][None],
                                        p["hy_f_w2"][j], p["hy_f_b2"][j][None], p["hy_f_w3"][j],
                                        p["hy_decay"][j][None], tf, r)
            cw = p["hy_conv_w"][j].reshape(3, 3, HY_WIDTH).transpose(1, 0, 2)
            cb = p["hy_conv_b"][j].reshape(3, HY_WIDTH)
            y = _hyena_conv(u, kf, tf, ti, cw, cb, p["hy_filt_bias"][j], r)
            gate_block = 3 * HY_WIDTH // D_MODEL
            name = "hyena_out"
        else:
            y = _attention(u, p["at_q_norm"][j], p["at_k_norm"][j])
            gate_block = 1
            name = "attn_out"
        if i + 1 < DEPTH:
            x, u = _layer_boundary(y, u, gate_block, x, out_params(i), in_params(i + 1), rope,
                                   name + "_next_in")
        else:
            x = _final_out_proj_ln(y, u, gate_block, x, *out_params(i), r, name + "_ln")
    return x


def kernel(x_prompt, x_sample, hy_w_in, hy_b_in, hy_conv_w, hy_conv_b, hy_f_w1, hy_f_b1, hy_f_freq, hy_f_w2,
           hy_f_b2, hy_f_w3, hy_decay, hy_filt_bias, hy_w_out, hy_b_out, at_w_in, at_b_in, at_q_norm, at_k_norm,
           at_w_out, at_b_out, ln_g, ln_b):
    p = dict(hy_w_in=hy_w_in, hy_b_in=hy_b_in, hy_conv_w=hy_conv_w, hy_conv_b=hy_conv_b, hy_f_w1=hy_f_w1,
             hy_f_b1=hy_f_b1, hy_f_freq=hy_f_freq, hy_f_w2=hy_f_w2, hy_f_b2=hy_f_b2, hy_f_w3=hy_f_w3,
             hy_decay=hy_decay, hy_filt_bias=hy_filt_bias, hy_w_out=hy_w_out, hy_b_out=hy_b_out,
             at_w_in=at_w_in, at_b_in=at_b_in, at_q_norm=at_q_norm, at_k_norm=at_k_norm, at_w_out=at_w_out,
             at_b_out=at_b_out, ln_g=ln_g, ln_b=ln_b)
    return (_trunk(x_prompt, p), _trunk(x_sample, p))
```
